```python
import math
import jax, jax.numpy as jnp
from jax import lax
import numpy as np

D_MODEL = 4096
BATCH = 4
SEQ = 2048
DEPTH = 2

GRID_W = 64
CTX_LEN = 256
N_EVEN = (DEPTH + 1) // 2
N_ODD = DEPTH // 2
NORM_EPS = 1e-6

POOL_WINDOWS = (2, 4, 8, 16)
POOL_GROUP = D_MODEL // 16
POOL_WIDTH = len(POOL_WINDOWS) * POOL_GROUP
GDN_HEAD_DIM = 128
GDN_HEADS = (D_MODEL - POOL_WIDTH) // GDN_HEAD_DIM
GDN_WIDTH = GDN_HEADS * GDN_HEAD_DIM
GDN_CONV = 5
GDN_CHUNK = 64
QKV_COLS = 3 * GDN_WIDTH
STATE_COLS = QKV_COLS + 4 * GDN_HEADS
IN_COLS = STATE_COLS + POOL_WIDTH + GDN_WIDTH
MIX_WIDTH = POOL_WIDTH + GDN_WIDTH
DIFF_HEAD_DIM = 128
DIFF_HEADS = D_MODEL // (2 * DIFF_HEAD_DIM)
ROPE_BASE = 10000.0
Q_BLOCK = 128
N_EXPERTS = 16
EC_CAPACITY = 2
D_EXPERT = 5 * D_MODEL // 16

kernel_name = 'hybrid_pool_gdn_diffattn_ecmoe_dit'


def rmsnorm(x, g):
    xf = x.astype(jnp.float32)
    y = xf * lax.rsqrt(jnp.mean(xf * xf, axis=-1, keepdims=True) + NORM_EPS)
    return (y * g.astype(jnp.float32)).astype(x.dtype)


def modulate(h, shift, scale):
    return h * (1 + scale) + shift


def l2norm(t):
    tf = t.astype(jnp.float32)
    return (tf * lax.rsqrt(jnp.sum(tf * tf, axis=-1, keepdims=True) + NORM_EPS)).astype(t.dtype)


def centred_mean(u, w):
    B, L, C = u.shape
    cs = jnp.concatenate([jnp.zeros((B, 1, C), jnp.float32), jnp.cumsum(u.astype(jnp.float32), axis=1)], axis=1)
    t = jnp.arange(L)
    lo = jnp.clip(t - w // 2, 0, L)
    hi = jnp.clip(t + w // 2, 0, L)
    cnt = (hi - lo).astype(jnp.float32)
    return ((cs[:, hi] - cs[:, lo]) / cnt[None, :, None]).astype(u.dtype)


def pool_mix(u, pool_w, pool_scale):
    B, L, _ = u.shape
    ug = u.reshape(B, L, len(POOL_WINDOWS), POOL_GROUP)
    means = jnp.stack([centred_mean(ug[:, :, i], w) for i, w in enumerate(POOL_WINDOWS)], axis=2)
    y = jnp.einsum('blgc,gce->blge', means - ug, pool_w).reshape(B, L, POOL_WIDTH)
    return y * pool_scale


def short_conv(u, w):
    y = lax.conv_general_dilated(u, w.astype(u.dtype)[:, None, :], window_strides=(1,),
                                 padding=[(GDN_CONV // 2, GDN_CONV // 2)],
                                 dimension_numbers=('NWC', 'WIO', 'NWC'),
                                 feature_group_count=u.shape[-1])
    return jax.nn.silu(y)


def gated_delta_chunked(q, k, v, g, beta, s0):
    out_dtype = v.dtype
    B, L, H, dk = q.shape
    dv = v.shape[-1]
    C = GDN_CHUNK
    N = L // C
    f32 = jnp.float32

    def chunked(t):
        t = t.astype(f32).reshape(B, N, C, H, *t.shape[3:])
        return jnp.moveaxis(t, 3, 1)

    q = chunked(q) * dk ** -0.5
    k = chunked(k)
    v = chunked(v)
    beta = chunked(beta)
    g = jnp.cumsum(chunked(g), axis=-1)
    pos = jnp.arange(C)
    lower = pos[:, None] >= pos[None, :]
    strict = pos[:, None] > pos[None, :]
    decay = jnp.exp(jnp.where(lower, g[..., :, None] - g[..., None, :], -jnp.inf))
    kb = k * beta[..., None]
    a_mat = jnp.where(strict, jnp.einsum('bhnid,bhnjd->bhnij', kb, k) * decay, 0.0)
    rhs = jnp.concatenate([v * beta[..., None], kb * jnp.exp(g)[..., None]], axis=-1)
    sol = lax.linalg.triangular_solve(a_mat + jnp.eye(C, dtype=f32), rhs, left_side=True, lower=True,
                                      unit_diagonal=True)
    u, w = sol[..., :dv], sol[..., dv:]
    attn = jnp.einsum('bhnid,bhnjd->bhnij', q, k) * decay
    q_dec = q * jnp.exp(g)[..., None]
    k_dec = k * jnp.exp(g[..., -1:] - g)[..., None]
    g_end = jnp.exp(g[..., -1])

    def step(s, inp):
        u_n, w_n, q_n, k_n, a_n, ge_n = inp
        v_new = u_n - jnp.einsum('bhcd,bhde->bhce', w_n, s)
        o_n = jnp.einsum('bhcd,bhde->bhce', q_n, s) + jnp.einsum('bhij,bhje->bhie', a_n, v_new)
        s = s * ge_n[..., None, None] + jnp.einsum('bhcd,bhce->bhde', k_n, v_new)
        return s, o_n

    xs = tuple(jnp.moveaxis(t, 2, 0) for t in (u, w, q_dec, k_dec, attn, g_end))
    s_end, o = lax.scan(step, s0, xs)
    o = jnp.transpose(o, (1, 0, 3, 2, 4)).reshape(B, L, H, dv)
    return o.astype(out_dtype), s_end


def bidir_delta(q, k, v, g_f, beta_f, g_b, beta_b, s_f, s_b):
    o_f, s_f = gated_delta_chunked(q, k, v, g_f, beta_f, s_f)
    flip = lambda t: jnp.flip(t, axis=1)
    o_b, s_b = gated_delta_chunked(flip(q), flip(k), flip(v), flip(g_b), flip(beta_b), s_b)
    return o_f + flip(o_b), s_f, s_b


def pool_gdn_mixer(h_ctx, h_lat, w_in, pool_w, pool_scale, conv_w, a_log_f, a_log_b, dt_bias_f, dt_bias_b,
                   gdn_norm, w_out, need_ctx_out):
    f32 = jnp.float32

    def gdn_inputs(p):
        B, L, _ = p.shape
        qkv = short_conv(p[..., :QKV_COLS], conv_w)
        q, k, v = jnp.split(qkv, 3, axis=-1)
        q = l2norm(q.reshape(B, L, GDN_HEADS, GDN_HEAD_DIM))
        k = l2norm(k.reshape(B, L, GDN_HEADS, GDN_HEAD_DIM))
        v = v.reshape(B, L, GDN_HEADS, GDN_HEAD_DIM)
        a_f, a_b, b_f, b_b = jnp.split(p[..., QKV_COLS:STATE_COLS].astype(f32), 4, axis=-1)
        g_f = -jnp.exp(a_log_f.astype(f32)) * jax.nn.softplus(a_f + dt_bias_f.astype(f32))
        g_b = -jnp.exp(a_log_b.astype(f32)) * jax.nn.softplus(a_b + dt_bias_b.astype(f32))
        return q, k, v, g_f, jax.nn.sigmoid(b_f), g_b, jax.nn.sigmoid(b_b)

    def merge_out(p, o):
        B, L, _ = p.shape
        z = p[..., STATE_COLS + POOL_WIDTH:].reshape(B, L, GDN_HEADS, GDN_HEAD_DIM)
        y_gdn = (rmsnorm(o, gdn_norm) * jax.nn.silu(z)).reshape(B, L, GDN_WIDTH)
        y_pool = pool_mix(p[..., STATE_COLS:STATE_COLS + POOL_WIDTH], pool_w, pool_scale)
        return jnp.concatenate([y_pool, y_gdn], axis=-1) @ w_out

    B = h_lat.shape[0]
    zero = jnp.zeros((B, GDN_HEADS, GDN_HEAD_DIM, GDN_HEAD_DIM), f32)
    p_ctx = h_ctx @ (w_in if need_ctx_out else w_in[:, :STATE_COLS])
    o_ctx, s_f, s_b = bidir_delta(*gdn_inputs(p_ctx), zero, zero)
    p_lat = h_lat @ w_in
    o_lat, _, _ = bidir_delta(*gdn_inputs(p_lat), s_f, s_b)
    y_lat = merge_out(p_lat, o_lat)
    y_ctx = merge_out(p_ctx, o_ctx) if need_ctx_out else None
    return y_ctx, y_lat


def axial_rope_tables(rows):
    pos_row = jnp.repeat(jnp.arange(rows), GRID_W).astype(jnp.float32)
    pos_col = jnp.tile(jnp.arange(GRID_W), rows).astype(jnp.float32)
    quarter = DIFF_HEAD_DIM // 4
    inv_freq = ROPE_BASE ** (-jnp.arange(quarter, dtype=jnp.float32) / quarter)
    ang_r = pos_row[:, None] * inv_freq[None, :]
    ang_c = pos_col[:, None] * inv_freq[None, :]
    ang = jnp.concatenate([ang_r, ang_r, ang_c, ang_c], axis=-1)
    return jnp.cos(ang), jnp.sin(ang)


def axial_rope(x, cos, sin):
    quarter = DIFF_HEAD_DIM // 4
    xs = x.reshape(*x.shape[:-1], 2, 2, quarter)
    rot = jnp.stack([-xs[..., 1, :], xs[..., 0, :]], axis=-2).reshape(x.shape)
    return x * cos[None, :, None, None, :] + rot * sin[None, :, None, None, :]


def diff_attention(q, k, v, lam):
    B, Lq, H, _, d = q.shape
    nb = Lq // Q_BLOCK
    qb = jnp.moveaxis(q.reshape(B, nb, Q_BLOCK, H, 2, d), 1, 0)
    scale = d ** -0.5

    def block(q_blk):
        s = jnp.einsum('bqhmd,bkhmd->bhmqk', q_blk, k).astype(jnp.float32) * scale
        p = jax.nn.softmax(s, axis=-1)
        a = p[:, :, 0] - lam * p[:, :, 1]
        return jnp.einsum('bhqk,bkhe->bqhe', a.astype(v.dtype), v)

    o = lax.map(block, qb)
    return jnp.moveaxis(o, 0, 1).reshape(B, Lq, H, 2 * d)


def diff_attn_mixer(h_ctx, h_lat, w_qkv, lam_q1, lam_k1, lam_q2, lam_k2, subln, w_out, lambda_init, cos, sin,
                    need_ctx_out):
    f32 = jnp.float32
    lam = (jnp.exp(jnp.sum(lam_q1.astype(f32) * lam_k1.astype(f32)))
           - jnp.exp(jnp.sum(lam_q2.astype(f32) * lam_k2.astype(f32))) + lambda_init)
    qk_heads = lambda t: t.reshape(t.shape[0], t.shape[1], DIFF_HEADS, 2, DIFF_HEAD_DIM)
    v_heads = lambda t: t.reshape(t.shape[0], t.shape[1], DIFF_HEADS, 2 * DIFF_HEAD_DIM)

    def finish(o):
        B, L = o.shape[:2]
        o = rmsnorm(o, subln) * (1 - lambda_init)
        return o.reshape(B, L, D_MODEL) @ w_out

    q_l, k_l, v_l = jnp.split(h_lat @ w_qkv, 3, axis=-1)
    q_l = axial_rope(qk_heads(q_l), cos, sin)
    k_l = axial_rope(qk_heads(k_l), cos, sin)
    v_l = v_heads(v_l)
    k_c, v_c = jnp.split(h_ctx @ w_qkv[:, D_MODEL:], 2, axis=-1)
    k_c, v_c = qk_heads(k_c), v_heads(v_c)
    y_lat = finish(diff_attention(q_l, jnp.concatenate([k_l, k_c], axis=1),
                                  jnp.concatenate([v_l, v_c], axis=1), lam))
    y_ctx = finish(diff_attention(qk_heads(h_ctx @ w_qkv[:, :D_MODEL]), k_c, v_c, lam)) if need_ctx_out else None
    return y_ctx, y_lat


def expert_choice_ffn(h, router, w1, w3, w2):
    B, L, _ = h.shape
    cap = EC_CAPACITY * L // N_EXPERTS
    aff = jax.nn.softmax((h @ router).astype(jnp.float32), axis=-1)
    gates, idx = lax.top_k(jnp.transpose(aff, (0, 2, 1)), cap)
    bidx = jnp.arange(B)[:, None, None]
    xe = h[bidx, idx]
    hid = jax.nn.silu(jnp.einsum('becd,edf->becf', xe, w1)) * jnp.einsum('becd,edf->becf', xe, w3)
    ye = jnp.einsum('becf,efd->becd', hid, w2) * gates[..., None].astype(h.dtype)
    return jnp.zeros_like(h).at[bidx, idx].add(ye)


def setup_inputs(seed: int = 0) -> dict:
    key = jax.random.key(seed)
    k = jax.random.split(key, 32)
    f32 = jnp.float32
    D = D_MODEL

    def nrm(i, shape, scale):
        return jax.random.normal(k[i], shape, f32) * scale

    def gain(i, shape):
        return 1.0 + nrm(i, shape, 0.05)

    a_log = jnp.log(jax.random.uniform(k[11], (2, N_EVEN, GDN_HEADS), f32, minval=1.0, maxval=16.0))
    dt = jnp.exp(jax.random.uniform(k[12], (2, N_EVEN, GDN_HEADS), f32,
                                    minval=math.log(1e-3), maxval=math.log(1e-1)))
    dt_bias = dt + jnp.log(-jnp.expm1(-dt))
    return {
        'x': nrm(0, (BATCH, SEQ, D), 1.0),
        'c': nrm(1, (BATCH, D), 1.0),
        'ctx': nrm(2, (BATCH, CTX_LEN, D), 1.0),
        'c_ctx': nrm(3, (D,), 1.0),
        'ada_w': nrm(4, (DEPTH, D, 6 * D), 0.5 * D ** -0.5),
        'ada_b': nrm(5, (DEPTH, 6 * D), 0.02),
        'norm_mix': gain(6, (DEPTH, D)),
        'norm_ffn': gain(7, (DEPTH, D)),
        'w_in': nrm(8, (N_EVEN, D, IN_COLS), D ** -0.5),
        'pool_w': nrm(9, (N_EVEN, len(POOL_WINDOWS), POOL_GROUP, POOL_GROUP), POOL_GROUP ** -0.5),
        'pool_scale': gain(10, (N_EVEN, POOL_WIDTH)),
        'conv_w': nrm(13, (N_EVEN, GDN_CONV, QKV_COLS), GDN_CONV ** -0.5),
        'a_log_f': a_log[0],
        'a_log_b': a_log[1],
        'dt_bias_f': dt_bias[0],
        'dt_bias_b': dt_bias[1],
        'gdn_norm': gain(14, (N_EVEN, GDN_HEAD_DIM)),
        'w_out_ab': nrm(15, (N_EVEN, MIX_WIDTH, D), MIX_WIDTH ** -0.5),
        'w_qkv': nrm(16, (N_ODD, D, 3 * D), D ** -0.5),
        'lam_q1': nrm(17, (N_ODD, DIFF_HEAD_DIM), 0.1),
        'lam_k1': nrm(18, (N_ODD, DIFF_HEAD_DIM), 0.1),
        'lam_q2': nrm(19, (N_ODD, DIFF_HEAD_DIM), 0.1),
        'lam_k2': nrm(20, (N_ODD, DIFF_HEAD_DIM), 0.1),
        'subln': gain(21, (N_ODD, 2 * DIFF_HEAD_DIM)),
        'w_out_c': nrm(22, (N_ODD, D, D), D ** -0.5),
        'router': nrm(23, (DEPTH, D, N_EXPERTS), D ** -0.5),
        'w1': nrm(24, (DEPTH, N_EXPERTS, D, D_EXPERT), D ** -0.5),
        'w3': nrm(25, (DEPTH, N_EXPERTS, D, D_EXPERT), D ** -0.5),
        'w2': nrm(26, (DEPTH, N_EXPERTS, D_EXPERT, D), D_EXPERT ** -0.5),
        'final_norm': gain(27, (D,)),
    }


def reference(x, c, ctx, c_ctx, ada_w, ada_b, norm_mix, norm_ffn,
              w_in, pool_w, pool_scale, conv_w, a_log_f, a_log_b, dt_bias_f, dt_bias_b, gdn_norm, w_out_ab,
              w_qkv, lam_q1, lam_k1, lam_q2, lam_k2, subln, w_out_c,
              router, w1, w3, w2, final_norm):
    rows = x.shape[1] // GRID_W
    cos, sin = axial_rope_tables(rows)
    cos, sin = cos.astype(x.dtype), sin.astype(x.dtype)
    cond_lat = jax.nn.silu(c)
    cond_ctx = jax.nn.silu(c_ctx)[None]
    x_lat, x_ctx = x, ctx
    for layer in range(DEPTH):
        last = layer == DEPTH - 1
        j = layer // 2
        m_lat = jnp.split((cond_lat @ ada_w[layer] + ada_b[layer])[:, None, :], 6, axis=-1)
        m_ctx = jnp.split((cond_ctx @ ada_w[layer] + ada_b[layer])[:, None, :], 6, axis=-1)
        h_lat = modulate(rmsnorm(x_lat, norm_mix[layer]), m_lat[0], m_lat[1])
        h_ctx = modulate(rmsnorm(x_ctx, norm_mix[layer]), m_ctx[0], m_ctx[1])
        if layer % 2 == 0:
            y_ctx, y_lat = pool_gdn_mixer(h_ctx, h_lat, w_in[j], pool_w[j], pool_scale[j], conv_w[j],
                                          a_log_f[j], a_log_b[j], dt_bias_f[j], dt_bias_b[j], gdn_norm[j],
                                          w_out_ab[j], not last)
        else:
            lambda_init = 0.8 - 0.6 * math.exp(-0.3 * layer)
            y_ctx, y_lat = diff_attn_mixer(h_ctx, h_lat, w_qkv[j], lam_q1[j], lam_k1[j], lam_q2[j], lam_k2[j],
                                           subln[j], w_out_c[j], lambda_init, cos, sin, not last)
        x_lat = x_lat + m_lat[2] * y_lat
        h2_lat = modulate(rmsnorm(x_lat, norm_ffn[layer]), m_lat[3], m_lat[4])
        x_lat = x_lat + m_lat[5] * expert_choice_ffn(h2_lat, router[layer], w1[layer], w3[layer], w2[layer])
        if not last:
            x_ctx = x_ctx + m_ctx[2] * y_ctx
            h2_ctx = modulate(rmsnorm(x_ctx, norm_ffn[layer]), m_ctx[3], m_ctx[4])
            x_ctx = x_ctx + m_ctx[5] * expert_choice_ffn(h2_ctx, router[layer], w1[layer], w3[layer], w2[layer])
    return rmsnorm(x_lat, final_norm)
```

```python
import functools
import math

import jax
import jax.numpy as jnp
from jax import lax
from jax.experimental import pallas as pl
from jax.experimental.pallas import tpu as pltpu

D_MODEL = 4096
GRID_W = 64
NORM_EPS = 1e-6
POOL_WINDOWS = (2, 4, 8, 16)
POOL_GROUP = D_MODEL // 16
POOL_WIDTH = len(POOL_WINDOWS) * POOL_GROUP
GDN_HEAD_DIM = 128
GDN_HEADS = (D_MODEL - POOL_WIDTH) // GDN_HEAD_DIM
GDN_WIDTH = GDN_HEADS * GDN_HEAD_DIM
GDN_CONV = 5
GDN_CHUNK = 64
QKV_COLS = 3 * GDN_WIDTH
STATE_COLS = QKV_COLS + 4 * GDN_HEADS
DIFF_HEAD_DIM = 128
DIFF_HEADS = D_MODEL // (2 * DIFF_HEAD_DIM)
ROPE_BASE = 10000.0
Q_BLOCK = 128
N_EXPERTS = 16
EC_CAPACITY = 2

VMEM_LIMIT_BYTES = 56 * 1024 * 1024


def _matmul_body(a_ref, b_ref, o_ref):
    o_ref[...] = jnp.dot(a_ref[...].astype(jnp.bfloat16), b_ref[...].astype(jnp.bfloat16),
                         preferred_element_type=jnp.float32).astype(o_ref.dtype)


def matmul(a, b, *, tm=512, tn=1024, out_dtype=jnp.float32):
    M, K = a.shape
    N = b.shape[1]
    tm = min(tm, M)
    tn = min(tn, N)
    assert M % tm == 0 and N % tn == 0, (M, N, tm, tn)
    return pl.pallas_call(
        _matmul_body,
        grid=(N // tn, M // tm),
        in_specs=[pl.BlockSpec((tm, K), lambda j, i: (i, 0)),
                  pl.BlockSpec((K, tn), lambda j, i: (0, j))],
        out_specs=pl.BlockSpec((tm, tn), lambda j, i: (i, j)),
        out_shape=jax.ShapeDtypeStruct((M, N), out_dtype),
        compiler_params=pltpu.CompilerParams(dimension_semantics=("arbitrary", "arbitrary"),
                                             vmem_limit_bytes=VMEM_LIMIT_BYTES),
        name="matmul",
    )(a, b)


def mm(a, b, **kw):
    lead = a.shape[:-1]
    K = a.shape[-1]
    N = b.shape[1]
    a2 = a.reshape(-1, K).astype(jnp.bfloat16)
    M = a2.shape[0]
    tm = kw.pop("tm", 512)
    tn = kw.pop("tn", 1024)
    Mp = -(-M // 8) * 8 if M < tm else -(-M // tm) * tm
    Np = -(-N // 128) * 128 if N < tn else -(-N // tn) * tn
    if Mp != M:
        a2 = jnp.pad(a2, ((0, Mp - M), (0, 0)))
    bb = b.astype(jnp.bfloat16)
    if Np != N:
        bb = jnp.pad(bb, ((0, 0), (0, Np - N)))
    out = matmul(a2, bb, tm=tm, tn=tn, **kw)
    return out[:M, :N].reshape(*lead, N)


def rmsnorm(x, g):
    xf = x.astype(jnp.float32)
    y = xf * lax.rsqrt(jnp.mean(xf * xf, axis=-1, keepdims=True) + NORM_EPS)
    return (y * g.astype(jnp.float32)).astype(x.dtype)


def modulate(h, shift, scale):
    return h * (1 + scale) + shift


def l2norm(t):
    tf = t.astype(jnp.float32)
    return (tf * lax.rsqrt(jnp.sum(tf * tf, axis=-1, keepdims=True) + NORM_EPS)).astype(t.dtype)


def centred_mean(u, w):
    B, L, C = u.shape
    cs = jnp.concatenate([jnp.zeros((B, 1, C), jnp.float32), jnp.cumsum(u.astype(jnp.float32), axis=1)], axis=1)
    t = jnp.arange(L)
    lo = jnp.clip(t - w // 2, 0, L)
    hi = jnp.clip(t + w // 2, 0, L)
    cnt = (hi - lo).astype(jnp.float32)
    return ((cs[:, hi] - cs[:, lo]) / cnt[None, :, None]).astype(u.dtype)


def pool_mix(u, pool_w, pool_scale):
    B, L, _ = u.shape
    ug = u.reshape(B, L, len(POOL_WINDOWS), POOL_GROUP)
    means = jnp.stack([centred_mean(ug[:, :, i], w) for i, w in enumerate(POOL_WINDOWS)], axis=2)
    y = jnp.einsum('blgc,gce->blge', means - ug, pool_w).reshape(B, L, POOL_WIDTH)
    return y * pool_scale


def short_conv(u, w):
    y = lax.conv_general_dilated(u, w.astype(u.dtype)[:, None, :], window_strides=(1,),
                                 padding=[(GDN_CONV // 2, GDN_CONV // 2)],
                                 dimension_numbers=('NWC', 'WIO', 'NWC'),
                                 feature_group_count=u.shape[-1])
    return jax.nn.silu(y)


def gated_delta_chunked(q, k, v, g, beta, s0):
    out_dtype = v.dtype
    B, L, H, dk = q.shape
    dv = v.shape[-1]
    C = GDN_CHUNK
    N = L // C
    f32 = jnp.float32

    def chunked(t):
        t = t.astype(f32).reshape(B, N, C, H, *t.shape[3:])
        return jnp.moveaxis(t, 3, 1)

    q = chunked(q) * dk ** -0.5
    k = chunked(k)
    v = chunked(v)
    beta = chunked(beta)
    g = jnp.cumsum(chunked(g), axis=-1)
    pos = jnp.arange(C)
    lower = pos[:, None] >= pos[None, :]
    strict = pos[:, None] > pos[None, :]
    decay = jnp.exp(jnp.where(lower, g[..., :, None] - g[..., None, :], -jnp.inf))
    kb = k * beta[..., None]
    a_mat = jnp.where(strict, jnp.einsum('bhnid,bhnjd->bhnij', kb, k) * decay, 0.0)
    rhs = jnp.concatenate([v * beta[..., None], kb * jnp.exp(g)[..., None]], axis=-1)
    sol = lax.linalg.triangular_solve(a_mat + jnp.eye(C, dtype=f32), rhs, left_side=True, lower=True,
                                      unit_diagonal=True)
    u, w = sol[..., :dv], sol[..., dv:]
    attn = jnp.einsum('bhnid,bhnjd->bhnij', q, k) * decay
    q_dec = q * jnp.exp(g)[..., None]
    k_dec = k * jnp.exp(g[..., -1:] - g)[..., None]
    g_end = jnp.exp(g[..., -1])

    def step(s, inp):
        u_n, w_n, q_n, k_n, a_n, ge_n = inp
        v_new = u_n - jnp.einsum('bhcd,bhde->bhce', w_n, s)
        o_n = jnp.einsum('bhcd,bhde->bhce', q_n, s) + jnp.einsum('bhij,bhje->bhie', a_n, v_new)
        s = s * ge_n[..., None, None] + jnp.einsum('bhcd,bhce->bhde', k_n, v_new)
        return s, o_n

    xs = tuple(jnp.moveaxis(t, 2, 0) for t in (u, w, q_dec, k_dec, attn, g_end))
    s_end, o = lax.scan(step, s0, xs)
    o = jnp.transpose(o, (1, 0, 3, 2, 4)).reshape(B, L, H, dv)
    return o.astype(out_dtype), s_end


def bidir_delta(q, k, v, g_f, beta_f, g_b, beta_b, s_f, s_b):
    o_f, s_f = gated_delta_chunked(q, k, v, g_f, beta_f, s_f)
    flip = lambda t: jnp.flip(t, axis=1)
    o_b, s_b = gated_delta_chunked(flip(q), flip(k), flip(v), flip(g_b), flip(beta_b), s_b)
    return o_f + flip(o_b), s_f, s_b


def pool_gdn_mixer(h_ctx, h_lat, w_in, pool_w, pool_scale, conv_w, a_log_f, a_log_b, dt_bias_f, dt_bias_b,
                   gdn_norm, w_out, need_ctx_out):
    f32 = jnp.float32

    def gdn_inputs(p):
        B, L, _ = p.shape
        qkv = short_conv(p[..., :QKV_COLS], conv_w)
        q, k, v = jnp.split(qkv, 3, axis=-1)
        q = l2norm(q.reshape(B, L, GDN_HEADS, GDN_HEAD_DIM))
        k = l2norm(k.reshape(B, L, GDN_HEADS, GDN_HEAD_DIM))
        v = v.reshape(B, L, GDN_HEADS, GDN_HEAD_DIM)
        a_f, a_b, b_f, b_b = jnp.split(p[..., QKV_COLS:STATE_COLS].astype(f32), 4, axis=-1)
        g_f = -jnp.exp(a_log_f.astype(f32)) * jax.nn.softplus(a_f + dt_bias_f.astype(f32))
        g_b = -jnp.exp(a_log_b.astype(f32)) * jax.nn.softplus(a_b + dt_bias_b.astype(f32))
        return q, k, v, g_f, jax.nn.sigmoid(b_f), g_b, jax.nn.sigmoid(b_b)

    def merge_out(p, o):
        B, L, _ = p.shape
        z = p[..., STATE_COLS + POOL_WIDTH:].reshape(B, L, GDN_HEADS, GDN_HEAD_DIM)
        y_gdn = (rmsnorm(o, gdn_norm) * jax.nn.silu(z)).reshape(B, L, GDN_WIDTH)
        y_pool = pool_mix(p[..., STATE_COLS:STATE_COLS + POOL_WIDTH], pool_w, pool_scale)
        return mm(jnp.concatenate([y_pool, y_gdn], axis=-1), w_out)

    B = h_lat.shape[0]
    zero = jnp.zeros((B, GDN_HEADS, GDN_HEAD_DIM, GDN_HEAD_DIM), f32)
    w_in_b = w_in.astype(jnp.bfloat16)
    p_ctx = mm(h_ctx, w_in_b if need_ctx_out else w_in_b[:, :STATE_COLS])
    o_ctx, s_f, s_b = bidir_delta(*gdn_inputs(p_ctx), zero, zero)
    p_lat = mm(h_lat, w_in_b)
    o_lat, _, _ = bidir_delta(*gdn_inputs(p_lat), s_f, s_b)
    y_lat = merge_out(p_lat, o_lat)
    y_ctx = merge_out(p_ctx, o_ctx) if need_ctx_out else None
    return y_ctx, y_lat


def axial_rope_tables(rows):
    pos_row = jnp.repeat(jnp.arange(rows), GRID_W).astype(jnp.float32)
    pos_col = jnp.tile(jnp.arange(GRID_W), rows).astype(jnp.float32)
    quarter = DIFF_HEAD_DIM // 4
    inv_freq = ROPE_BASE ** (-jnp.arange(quarter, dtype=jnp.float32) / quarter)
    ang_r = pos_row[:, None] * inv_freq[None, :]
    ang_c = pos_col[:, None] * inv_freq[None, :]
    ang = jnp.concatenate([ang_r, ang_r, ang_c, ang_c], axis=-1)
    return jnp.cos(ang), jnp.sin(ang)


def axial_rope(x, cos, sin):
    quarter = DIFF_HEAD_DIM // 4
    xs = x.reshape(*x.shape[:-1], 2, 2, quarter)
    rot = jnp.stack([-xs[..., 1, :], xs[..., 0, :]], axis=-2).reshape(x.shape)
    return x * cos[None, :, None, None, :] + rot * sin[None, :, None, None, :]


def diff_attention(q, k, v, lam):
    B, Lq, H, _, d = q.shape
    nb = Lq // Q_BLOCK
    qb = jnp.moveaxis(q.reshape(B, nb, Q_BLOCK, H, 2, d), 1, 0)
    scale = d ** -0.5

    def block(q_blk):
        s = jnp.einsum('bqhmd,bkhmd->bhmqk', q_blk, k).astype(jnp.float32) * scale
        p = jax.nn.softmax(s, axis=-1)
        a = p[:, :, 0] - lam * p[:, :, 1]
        return jnp.einsum('bhqk,bkhe->bqhe', a.astype(v.dtype), v)

    o = lax.map(block, qb)
    return jnp.moveaxis(o, 0, 1).reshape(B, Lq, H, 2 * d)


def diff_attn_mixer(h_ctx, h_lat, w_qkv, lam_q1, lam_k1, lam_q2, lam_k2, subln, w_out, lambda_init, cos, sin,
                    need_ctx_out):
    f32 = jnp.float32
    lam = (jnp.exp(jnp.sum(lam_q1.astype(f32) * lam_k1.astype(f32)))
           - jnp.exp(jnp.sum(lam_q2.astype(f32) * lam_k2.astype(f32))) + lambda_init)
    qk_heads = lambda t: t.reshape(t.shape[0], t.shape[1], DIFF_HEADS, 2, DIFF_HEAD_DIM)
    v_heads = lambda t: t.reshape(t.shape[0], t.shape[1], DIFF_HEADS, 2 * DIFF_HEAD_DIM)

    def finish(o):
        B, L = o.shape[:2]
        o = rmsnorm(o, subln) * (1 - lambda_init)
        return mm(o.reshape(B, L, D_MODEL), w_out)

    w_qkv_b = w_qkv.astype(jnp.bfloat16)
    q_l, k_l, v_l = jnp.split(mm(h_lat, w_qkv_b), 3, axis=-1)
    q_l = axial_rope(qk_heads(q_l), cos, sin)
    k_l = axial_rope(qk_heads(k_l), cos, sin)
    v_l = v_heads(v_l)
    k_c, v_c = jnp.split(mm(h_ctx, w_qkv_b[:, D_MODEL:]), 2, axis=-1)
    k_c, v_c = qk_heads(k_c), v_heads(v_c)
    y_lat = finish(diff_attention(q_l, jnp.concatenate([k_l, k_c], axis=1),
                                  jnp.concatenate([v_l, v_c], axis=1), lam))
    y_ctx = finish(diff_attention(qk_heads(mm(h_ctx, w_qkv_b[:, :D_MODEL])), k_c, v_c, lam)) if need_ctx_out else None
    return y_ctx, y_lat


def expert_choice_ffn(h, router, w1, w3, w2):
    B, L, _ = h.shape
    cap = EC_CAPACITY * L // N_EXPERTS
    aff = jax.nn.softmax((h @ router).astype(jnp.float32), axis=-1)
    gates, idx = lax.top_k(jnp.transpose(aff, (0, 2, 1)), cap)
    bidx = jnp.arange(B)[:, None, None]
    xe = h[bidx, idx]
    hid = jax.nn.silu(jnp.einsum('becd,edf->becf', xe, w1)) * jnp.einsum('becd,edf->becf', xe, w3)
    ye = jnp.einsum('becf,efd->becd', hid, w2) * gates[..., None].astype(h.dtype)
    return jnp.zeros_like(h).at[bidx, idx].add(ye)


def kernel(x, c, ctx, c_ctx, ada_w, ada_b, norm_mix, norm_ffn, w_in, pool_w, pool_scale, conv_w, a_log_f, a_log_b, dt_bias_f, dt_bias_b, gdn_norm, w_out_ab, w_qkv, lam_q1, lam_k1, lam_q2, lam_k2, subln, w_out_c, router, w1, w3, w2, final_norm):
    depth = ada_w.shape[0]
    rows = x.shape[1] // GRID_W
    cos, sin = axial_rope_tables(rows)
    cos, sin = cos.astype(x.dtype), sin.astype(x.dtype)
    cond_lat = jax.nn.silu(c)
    cond_ctx = jax.nn.silu(c_ctx)[None]
    x_lat, x_ctx = x, ctx
    for layer in range(depth):
        last = layer == depth - 1
        j = layer // 2
        m_lat = jnp.split((mm(cond_lat, ada_w[layer], tn=512) + ada_b[layer])[:, None, :], 6, axis=-1)
        m_ctx = jnp.split((mm(cond_ctx, ada_w[layer], tn=512) + ada_b[layer])[:, None, :], 6, axis=-1)
        h_lat = modulate(rmsnorm(x_lat, norm_mix[layer]), m_lat[0], m_lat[1])
        h_ctx = modulate(rmsnorm(x_ctx, norm_mix[layer]), m_ctx[0], m_ctx[1])
        if layer % 2 == 0:
            y_ctx, y_lat = pool_gdn_mixer(h_ctx, h_lat, w_in[j], pool_w[j], pool_scale[j], conv_w[j],
                                          a_log_f[j], a_log_b[j], dt_bias_f[j], dt_bias_b[j], gdn_norm[j],
                                          w_out_ab[j], not last)
        else:
            lambda_init = 0.8 - 0.6 * math.exp(-0.3 * layer)
            y_ctx, y_lat = diff_attn_mixer(h_ctx, h_lat, w_qkv[j], lam_q1[j], lam_k1[j], lam_q2[j], lam_k2[j],
                                           subln[j], w_out_c[j], lambda_init, cos, sin, not last)
        x_lat = x_lat + m_lat[2] * y_lat
        h2_lat = modulate(rmsnorm(x_lat, norm_ffn[layer]), m_lat[3], m_lat[4])
        x_lat = x_lat + m_lat[5] * expert_choice_ffn(h2_lat, router[layer], w1[layer], w3[layer], w2[layer])
        if not last:
            x_ctx = x_ctx + m_ctx[2] * y_ctx
            h2_ctx = modulate(rmsnorm(x_ctx, norm_ffn[layer]), m_ctx[3], m_ctx[4])
            x_ctx = x_ctx + m_ctx[5] * expert_choice_ffn(h2_ctx, router[layer], w1[layer], w3[layer], w2[layer])
    return rmsnorm(x_lat, final_norm)
```

```python
import functools
import math

import jax
import jax.numpy as jnp
from jax import lax
from jax.experimental import pallas as pl
from jax.experimental.pallas import tpu as pltpu

D_MODEL = 4096
GRID_W = 64
NORM_EPS = 1e-6
POOL_WINDOWS = (2, 4, 8, 16)
POOL_GROUP = D_MODEL // 16
POOL_WIDTH = len(POOL_WINDOWS) * POOL_GROUP
GDN_HEAD_DIM = 128
GDN_HEADS = (D_MODEL - POOL_WIDTH) // GDN_HEAD_DIM
GDN_WIDTH = GDN_HEADS * GDN_HEAD_DIM
GDN_CONV = 5
GDN_CHUNK = 64
QKV_COLS = 3 * GDN_WIDTH
STATE_COLS = QKV_COLS + 4 * GDN_HEADS
DIFF_HEAD_DIM = 128
DIFF_HEADS = D_MODEL // (2 * DIFF_HEAD_DIM)
ROPE_BASE = 10000.0
Q_BLOCK = 128
N_EXPERTS = 16
EC_CAPACITY = 2

LANES = 128
MOD_ROWS = 8
PREFIX_BLOCK = 256
VMEM_LIMIT_BYTES = 56 * 1024 * 1024


def _matmul_body(a_ref, b_ref, o_ref):
    o_ref[...] = jnp.dot(a_ref[...].astype(jnp.bfloat16), b_ref[...].astype(jnp.bfloat16),
                         preferred_element_type=jnp.float32).astype(o_ref.dtype)


def matmul(a, b, *, tm=512, tn=1024, out_dtype=jnp.float32):
    M, K = a.shape
    N = b.shape[1]
    tm = min(tm, M)
    tn = min(tn, N)
    assert M % tm == 0 and N % tn == 0, (M, N, tm, tn)
    return pl.pallas_call(
        _matmul_body,
        grid=(N // tn, M // tm),
        in_specs=[pl.BlockSpec((tm, K), lambda j, i: (i, 0)),
                  pl.BlockSpec((K, tn), lambda j, i: (0, j))],
        out_specs=pl.BlockSpec((tm, tn), lambda j, i: (i, j)),
        out_shape=jax.ShapeDtypeStruct((M, N), out_dtype),
        compiler_params=pltpu.CompilerParams(dimension_semantics=("arbitrary", "arbitrary"),
                                             vmem_limit_bytes=VMEM_LIMIT_BYTES),
        name="matmul",
    )(a, b)


def _proj_body(rows_per_sample, ctx_row, a_ref, w_ref, *rest):
    if rows_per_sample is None:
        o_ref, wb_ref = rest
    else:
        x_ref, gate_ref, o_ref, wb_ref = rest
    i = pl.program_id(1)

    @pl.when(i == 0)
    def _():
        wb_ref[...] = w_ref[...].astype(jnp.bfloat16)

    y = jnp.dot(a_ref[...], wb_ref[...], preferred_element_type=jnp.float32)
    if rows_per_sample is None:
        o_ref[...] = y.astype(o_ref.dtype)
    else:
        row = (i * a_ref.shape[0]) // rows_per_sample if ctx_row is None else ctx_row
        o_ref[...] = x_ref[...] + _mod_row(gate_ref, row) * y


def proj(a, w, *, col0=0, ncols=None, tm=512, tn=512, out_dtype=jnp.float32, residual=None):
    M, K = a.shape
    ncols = w.shape[1] - col0 if ncols is None else ncols
    tm = min(tm, M)
    assert M % tm == 0 and ncols % tn == 0 and col0 % tn == 0, (M, ncols, col0, tm, tn)
    j0 = col0 // tn
    in_specs = [pl.BlockSpec((tm, K), lambda j, i: (i, 0)),
                pl.BlockSpec((K, tn), lambda j, i: (0, j0 + j))]
    args = [a, w]
    rows_per_sample = ctx_row = None
    if residual is not None:
        x, mod, k_gate, rows_per_sample, ctx_row = residual
        assert ctx_row is not None or rows_per_sample % tm == 0
        in_specs += [pl.BlockSpec((tm, tn), lambda j, i: (i, j)),
                     pl.BlockSpec((MOD_ROWS, tn), lambda j, i: (0, k_gate * (ncols // tn) + j))]
        args += [x, mod]
    return pl.pallas_call(
        functools.partial(_proj_body, rows_per_sample, ctx_row),
        grid=(ncols // tn, M // tm),
        in_specs=in_specs,
        out_specs=pl.BlockSpec((tm, tn), lambda j, i: (i, j)),
        out_shape=jax.ShapeDtypeStruct((M, ncols), out_dtype),
        scratch_shapes=[pltpu.VMEM((K, tn), jnp.bfloat16)],
        compiler_params=_cparams("arbitrary", "arbitrary"), name="proj",
    )(*args)


def mm(a, b, **kw):
    lead = a.shape[:-1]
    K = a.shape[-1]
    N = b.shape[1]
    a2 = a.reshape(-1, K).astype(jnp.bfloat16)
    M = a2.shape[0]
    tm = kw.pop("tm", 512)
    tn = kw.pop("tn", 1024)
    Mp = -(-M // 8) * 8 if M < tm else -(-M // tm) * tm
    Np = -(-N // 128) * 128 if N < tn else -(-N // tn) * tn
    if Mp != M:
        a2 = jnp.pad(a2, ((0, Mp - M), (0, 0)))
    bb = b.astype(jnp.bfloat16)
    if Np != N:
        bb = jnp.pad(bb, ((0, 0), (0, Np - N)))
    out = matmul(a2, bb, tm=tm, tn=tn, **kw)
    return out[:M, :N].reshape(*lead, N)


def rmsnorm(x, g):
    xf = x.astype(jnp.float32)
    y = xf * lax.rsqrt(jnp.mean(xf * xf, axis=-1, keepdims=True) + NORM_EPS)
    return (y * g.astype(jnp.float32)).astype(x.dtype)


def modulate(h, shift, scale):
    return h * (1 + scale) + shift


def l2norm(t):
    tf = t.astype(jnp.float32)
    return (tf * lax.rsqrt(jnp.sum(tf * tf, axis=-1, keepdims=True) + NORM_EPS)).astype(t.dtype)


def centred_mean(u, w):
    B, L, C = u.shape
    cs = jnp.concatenate([jnp.zeros((B, 1, C), jnp.float32), jnp.cumsum(u.astype(jnp.float32), axis=1)], axis=1)
    t = jnp.arange(L)
    lo = jnp.clip(t - w // 2, 0, L)
    hi = jnp.clip(t + w // 2, 0, L)
    cnt = (hi - lo).astype(jnp.float32)
    return ((cs[:, hi] - cs[:, lo]) / cnt[None, :, None]).astype(u.dtype)


def pool_mix(u, pool_w, pool_scale):
    B, L, _ = u.shape
    ug = u.reshape(B, L, len(POOL_WINDOWS), POOL_GROUP)
    means = jnp.stack([centred_mean(ug[:, :, i], w) for i, w in enumerate(POOL_WINDOWS)], axis=2)
    y = jnp.einsum('blgc,gce->blge', means - ug, pool_w).reshape(B, L, POOL_WIDTH)
    return y * pool_scale


def short_conv(u, w):
    y = lax.conv_general_dilated(u, w.astype(u.dtype)[:, None, :], window_strides=(1,),
                                 padding=[(GDN_CONV // 2, GDN_CONV // 2)],
                                 dimension_numbers=('NWC', 'WIO', 'NWC'),
                                 feature_group_count=u.shape[-1])
    return jax.nn.silu(y)


def gated_delta_chunked(q, k, v, g, beta, s0):
    out_dtype = v.dtype
    B, L, H, dk = q.shape
    dv = v.shape[-1]
    C = GDN_CHUNK
    N = L // C
    f32 = jnp.float32

    def chunked(t):
        t = t.astype(f32).reshape(B, N, C, H, *t.shape[3:])
        return jnp.moveaxis(t, 3, 1)

    q = chunked(q) * dk ** -0.5
    k = chunked(k)
    v = chunked(v)
    beta = chunked(beta)
    g = jnp.cumsum(chunked(g), axis=-1)
    pos = jnp.arange(C)
    lower = pos[:, None] >= pos[None, :]
    strict = pos[:, None] > pos[None, :]
    decay = jnp.exp(jnp.where(lower, g[..., :, None] - g[..., None, :], -jnp.inf))
    kb = k * beta[..., None]
    a_mat = jnp.where(strict, jnp.einsum('bhnid,bhnjd->bhnij', kb, k) * decay, 0.0)
    rhs = jnp.concatenate([v * beta[..., None], kb * jnp.exp(g)[..., None]], axis=-1)
    sol = lax.linalg.triangular_solve(a_mat + jnp.eye(C, dtype=f32), rhs, left_side=True, lower=True,
                                      unit_diagonal=True)
    u, w = sol[..., :dv], sol[..., dv:]
    attn = jnp.einsum('bhnid,bhnjd->bhnij', q, k) * decay
    q_dec = q * jnp.exp(g)[..., None]
    k_dec = k * jnp.exp(g[..., -1:] - g)[..., None]
    g_end = jnp.exp(g[..., -1])

    def step(s, inp):
        u_n, w_n, q_n, k_n, a_n, ge_n = inp
        v_new = u_n - jnp.einsum('bhcd,bhde->bhce', w_n, s)
        o_n = jnp.einsum('bhcd,bhde->bhce', q_n, s) + jnp.einsum('bhij,bhje->bhie', a_n, v_new)
        s = s * ge_n[..., None, None] + jnp.einsum('bhcd,bhce->bhde', k_n, v_new)
        return s, o_n

    xs = tuple(jnp.moveaxis(t, 2, 0) for t in (u, w, q_dec, k_dec, attn, g_end))
    s_end, o = lax.scan(step, s0, xs)
    o = jnp.transpose(o, (1, 0, 3, 2, 4)).reshape(B, L, H, dv)
    return o.astype(out_dtype), s_end


def bidir_delta(q, k, v, g_f, beta_f, g_b, beta_b, s_f, s_b):
    o_f, s_f = gated_delta_chunked(q, k, v, g_f, beta_f, s_f)
    flip = lambda t: jnp.flip(t, axis=1)
    o_b, s_b = gated_delta_chunked(flip(q), flip(k), flip(v), flip(g_b), flip(beta_b), s_b)
    return o_f + flip(o_b), s_f, s_b


def pool_gdn_mixer(h_ctx, h_lat, w_in, pool_w, pool_scale, conv_w, a_log_f, a_log_b, dt_bias_f, dt_bias_b,
                   gdn_norm, w_out, need_ctx_out):
    f32 = jnp.float32

    def gdn_inputs(p):
        B, L, _ = p.shape
        qkv = short_conv(p[..., :QKV_COLS], conv_w)
        q, k, v = jnp.split(qkv, 3, axis=-1)
        q = l2norm(q.reshape(B, L, GDN_HEADS, GDN_HEAD_DIM))
        k = l2norm(k.reshape(B, L, GDN_HEADS, GDN_HEAD_DIM))
        v = v.reshape(B, L, GDN_HEADS, GDN_HEAD_DIM)
        a_f, a_b, b_f, b_b = jnp.split(p[..., QKV_COLS:STATE_COLS].astype(f32), 4, axis=-1)
        g_f = -jnp.exp(a_log_f.astype(f32)) * jax.nn.softplus(a_f + dt_bias_f.astype(f32))
        g_b = -jnp.exp(a_log_b.astype(f32)) * jax.nn.softplus(a_b + dt_bias_b.astype(f32))
        return q, k, v, g_f, jax.nn.sigmoid(b_f), g_b, jax.nn.sigmoid(b_b)

    def merge_out(p, o):
        B, L, _ = p.shape
        z = p[..., STATE_COLS + POOL_WIDTH:].reshape(B, L, GDN_HEADS, GDN_HEAD_DIM)
        y_gdn = (rmsnorm(o, gdn_norm) * jax.nn.silu(z)).reshape(B, L, GDN_WIDTH)
        y_pool = pool_mix(p[..., STATE_COLS:STATE_COLS + POOL_WIDTH], pool_w, pool_scale)
        return jnp.concatenate([y_pool, y_gdn], axis=-1).astype(jnp.bfloat16)

    B = h_lat.shape[0]
    zero = jnp.zeros((B, GDN_HEADS, GDN_HEAD_DIM, GDN_HEAD_DIM), f32)
    w_in_b = w_in.astype(jnp.bfloat16)
    p_ctx = mm(h_ctx, w_in_b if need_ctx_out else w_in_b[:, :STATE_COLS])
    o_ctx, s_f, s_b = bidir_delta(*gdn_inputs(p_ctx), zero, zero)
    p_lat = mm(h_lat, w_in_b)
    o_lat, _, _ = bidir_delta(*gdn_inputs(p_lat), s_f, s_b)
    y_lat = merge_out(p_lat, o_lat)
    y_ctx = merge_out(p_ctx, o_ctx) if need_ctx_out else None
    return y_ctx, y_lat


def axial_rope_tables(rows):
    pos_row = jnp.repeat(jnp.arange(rows), GRID_W).astype(jnp.float32)
    pos_col = jnp.tile(jnp.arange(GRID_W), rows).astype(jnp.float32)
    quarter = DIFF_HEAD_DIM // 4
    inv_freq = ROPE_BASE ** (-jnp.arange(quarter, dtype=jnp.float32) / quarter)
    ang_r = pos_row[:, None] * inv_freq[None, :]
    ang_c = pos_col[:, None] * inv_freq[None, :]
    ang = jnp.concatenate([ang_r, ang_r, ang_c, ang_c], axis=-1)
    return jnp.cos(ang), jnp.sin(ang)


def _rope(x, cos, sin_signed):
    d = x.shape[-1]
    first = (lax.broadcasted_iota(jnp.int32, x.shape, 1) & (d // 2 - 1)) < d // 4
    partner = jnp.where(first, pltpu.roll(x, d - d // 4, 1), pltpu.roll(x, d // 4, 1))
    return x * cos + partner * sin_signed


def _diff_attn_body(lambda_init, tq, q_ref, kl_ref, vl_ref, kc_ref, vc_ref, cos_ref, sin_ref, lam_ref, g_ref,
                    o_ref, k_sc, v_sc):
    d = DIFF_HEAD_DIM
    L = kl_ref.shape[1]
    scale = d ** -0.5
    lv = lam_ref[...]
    lam = (jnp.exp(jnp.sum(lv[0:1] * lv[1:2], axis=1, keepdims=True))
           - jnp.exp(jnp.sum(lv[2:3] * lv[3:4], axis=1, keepdims=True)) + lambda_init)
    cos = cos_ref[...]
    sin = sin_ref[...]
    for m in range(2):
        k_sc[0:L, m * d:(m + 1) * d] = _rope(kl_ref[0, :, m * d:(m + 1) * d], cos, sin).astype(jnp.bfloat16)
    k_sc[L:, :] = kc_ref[0].astype(jnp.bfloat16)
    v_sc[0:L, :] = vl_ref[0].astype(jnp.bfloat16)
    v_sc[L:, :] = vc_ref[0].astype(jnp.bfloat16)

    def q_tile(t, carry):
        r0 = pl.multiple_of(t * tq, tq)
        cs = cos_ref[pl.ds(r0, tq), :]
        sn = sin_ref[pl.ds(r0, tq), :]
        probs = []
        for m in range(2):
            q = _rope(q_ref[0, pl.ds(r0, tq), m * d:(m + 1) * d], cs, sn).astype(jnp.bfloat16)
            s = lax.dot_general(q, k_sc[:, m * d:(m + 1) * d], (((1,), (1,)), ((), ())),
                                preferred_element_type=jnp.float32) * scale
            e = jnp.exp(s - jnp.max(s, axis=-1, keepdims=True))
            probs.append(e / jnp.sum(e, axis=-1, keepdims=True))
        a = (probs[0] - lam * probs[1]).astype(jnp.bfloat16)
        o = jnp.dot(a, v_sc[...], preferred_element_type=jnp.float32)
        o = o * lax.rsqrt(jnp.mean(o * o, axis=-1, keepdims=True) + NORM_EPS) * g_ref[...]
        o_ref[0, pl.ds(r0, tq), :] = (o * (1 - lambda_init)).astype(o_ref.dtype)
        return carry

    lax.fori_loop(0, L // tq, q_tile, 0)


def diff_attention_lat(qkv_lat, kv_ctx, cos, sin_signed, lam_vecs, subln, lambda_init, tq=256):
    B, L, _ = qkv_lat.shape
    Lc = kv_ctx.shape[1]
    H, hd = DIFF_HEADS, 2 * DIFF_HEAD_DIM
    tq = min(tq, L)
    return pl.pallas_call(
        functools.partial(_diff_attn_body, lambda_init, tq),
        grid=(B, H),
        in_specs=[pl.BlockSpec((1, L, hd), lambda b, h: (b, 0, h)),
                  pl.BlockSpec((1, L, hd), lambda b, h: (b, 0, H + h)),
                  pl.BlockSpec((1, L, hd), lambda b, h: (b, 0, 2 * H + h)),
                  pl.BlockSpec((1, Lc, hd), lambda b, h: (b, 0, h)),
                  pl.BlockSpec((1, Lc, hd), lambda b, h: (b, 0, H + h)),
                  pl.BlockSpec((L, DIFF_HEAD_DIM), lambda b, h: (0, 0)),
                  pl.BlockSpec((L, DIFF_HEAD_DIM), lambda b, h: (0, 0)),
                  pl.BlockSpec((4, DIFF_HEAD_DIM), lambda b, h: (0, 0)),
                  pl.BlockSpec((1, hd), lambda b, h: (0, 0))],
        out_specs=pl.BlockSpec((1, L, hd), lambda b, h: (b, 0, h)),
        out_shape=jax.ShapeDtypeStruct((B, L, H * hd), jnp.bfloat16),
        scratch_shapes=[pltpu.VMEM((L + Lc, hd), jnp.bfloat16), pltpu.VMEM((L + Lc, hd), jnp.bfloat16)],
        compiler_params=_cparams("arbitrary", "arbitrary"), name="diff_attention",
    )(qkv_lat, qkv_lat, qkv_lat, kv_ctx, kv_ctx, cos, sin_signed, lam_vecs, subln.reshape(1, hd))


def diff_attn_mixer(h_ctx, h_lat, w_qkv, lam_q1, lam_k1, lam_q2, lam_k2, subln, w_out, lambda_init, cos, sin,
                    need_ctx_out):
    assert not need_ctx_out, "the attention layer is the last layer here: context outputs are never consumed"
    B, L, D = h_lat.shape
    Lc = h_ctx.shape[1]
    qkv = proj(h_lat.reshape(B * L, D), w_qkv, tm=1024).reshape(B, L, 3 * D)
    kv_ctx = proj(h_ctx.reshape(B * Lc, D), w_qkv, col0=D, tm=1024).reshape(B, Lc, 2 * D)
    quarter = DIFF_HEAD_DIM // 4
    first = (jnp.arange(DIFF_HEAD_DIM) % (2 * quarter)) < quarter
    sin_signed = jnp.where(first[None, :], -sin, sin)
    lam_vecs = jnp.stack([lam_q1, lam_k1, lam_q2, lam_k2]).astype(jnp.float32)
    return diff_attention_lat(qkv, kv_ctx, cos, sin_signed, lam_vecs, subln, lambda_init)


def _cparams(*sem):
    return pltpu.CompilerParams(dimension_semantics=sem, vmem_limit_bytes=VMEM_LIMIT_BYTES)


def _mod_row(mod_ref, row):
    return mod_ref[pl.ds(row, 1), :]


def _normmod_body(ctx_row, with_router, x_ref, g_ref, shift_ref, scale_ref, *rest):
    row = pl.program_id(0) if ctx_row is None else ctx_row
    x = x_ref[0]
    y = x * lax.rsqrt(jnp.mean(x * x, axis=-1, keepdims=True) + NORM_EPS) * g_ref[...]
    h = (y * (1.0 + _mod_row(scale_ref, row)) + _mod_row(shift_ref, row)).astype(jnp.bfloat16)
    if not with_router:
        (h_ref,) = rest
        h_ref[0] = h
        return
    r_ref, h_ref, aff_ref = rest
    h_ref[0] = h
    logits = jnp.dot(h, r_ref[...], preferred_element_type=jnp.float32)
    lane = lax.broadcasted_iota(jnp.int32, logits.shape, 1)
    logits = jnp.where(lane < N_EXPERTS, logits, -jnp.inf)
    e = jnp.exp(logits - jnp.max(logits, axis=-1, keepdims=True))
    aff_ref[0] = e / jnp.sum(e, axis=-1, keepdims=True)


def normmod(x, gain, mod, k_shift, k_scale, ctx_row=None, router=None, tr=256):
    B, L, D = x.shape
    tr = min(tr, L)
    in_specs = [pl.BlockSpec((1, tr, D), lambda b, t: (b, t, 0)),
                pl.BlockSpec((1, D), lambda b, t: (0, 0)),
                pl.BlockSpec((MOD_ROWS, D), lambda b, t: (0, k_shift)),
                pl.BlockSpec((MOD_ROWS, D), lambda b, t: (0, k_scale))]
    args = [x, gain.reshape(1, D), mod, mod]
    out_specs = [pl.BlockSpec((1, tr, D), lambda b, t: (b, t, 0))]
    out_shape = [jax.ShapeDtypeStruct((B, L, D), jnp.bfloat16)]
    if router is not None:
        rp = jnp.pad(router.astype(jnp.bfloat16), ((0, 0), (0, LANES - router.shape[1])))
        in_specs.append(pl.BlockSpec((D, LANES), lambda b, t: (0, 0)))
        args.append(rp)
        out_specs.append(pl.BlockSpec((1, tr, LANES), lambda b, t: (b, t, 0)))
        out_shape.append(jax.ShapeDtypeStruct((B, L, LANES), jnp.float32))
    out = pl.pallas_call(
        functools.partial(_normmod_body, ctx_row, router is not None),
        grid=(B, L // tr), in_specs=in_specs, out_specs=out_specs, out_shape=out_shape,
        compiler_params=_cparams("arbitrary", "arbitrary"), name="normmod",
    )(*args)
    return out if router is not None else out[0]


def _final_norm_body(x_ref, g_ref, o_ref):
    x = x_ref[...]
    o_ref[...] = x * lax.rsqrt(jnp.mean(x * x, axis=-1, keepdims=True) + NORM_EPS) * g_ref[...]


def final_rmsnorm(x, gain, tr=512):
    B, L, D = x.shape
    M = B * L
    tr = min(tr, M)
    return pl.pallas_call(
        _final_norm_body,
        grid=(M // tr,),
        in_specs=[pl.BlockSpec((tr, D), lambda i: (i, 0)), pl.BlockSpec((1, D), lambda i: (0, 0))],
        out_specs=pl.BlockSpec((tr, D), lambda i: (i, 0)),
        out_shape=jax.ShapeDtypeStruct((M, D), x.dtype),
        compiler_params=_cparams("arbitrary"), name="final_rmsnorm",
    )(x.reshape(M, D), gain.reshape(1, D)).reshape(B, L, D)


def _count_before(mask):
    R, L = mask.shape
    W = PREFIX_BLOCK
    m = mask.astype(jnp.float32)
    before = (lax.broadcasted_iota(jnp.int32, (W, W), 0) < lax.broadcasted_iota(jnp.int32, (W, W), 1)
              ).astype(jnp.bfloat16)
    carry = jnp.zeros((R, 1), jnp.float32)
    out = []
    for c in range(L // W):
        blk = m[:, c * W:(c + 1) * W]
        out.append(jnp.dot(blk.astype(jnp.bfloat16), before, preferred_element_type=jnp.float32) + carry)
        carry = carry + jnp.sum(blk, axis=1, keepdims=True)
    return out[0] if len(out) == 1 else jnp.concatenate(out, axis=1)


def _select_body(cap, aff_ref, pos_e_ref, gate_e_ref, pos_t_ref):
    L = aff_ref.shape[1]
    a = aff_ref[0].T[:N_EXPERTS]
    bits = pltpu.bitcast(a, jnp.int32)
    thr = jnp.zeros((N_EXPERTS, 1), jnp.int32)
    for bit in range(30, -1, -1):
        cand = thr | (1 << bit)
        cnt = jnp.sum((bits >= cand).astype(jnp.float32), axis=1, keepdims=True)
        thr = jnp.where(cnt >= cap, cand, thr)
    gt = bits > thr
    eq = bits == thr
    need = cap - jnp.sum(gt.astype(jnp.float32), axis=1, keepdims=True)
    sel = gt | (eq & (_count_before(eq) < need))
    pos = jnp.where(sel, _count_before(sel).astype(jnp.int32), -1)
    pos_e_ref[0] = pos
    gate_e_ref[0] = jnp.where(sel, a, 0.0)
    pad = jnp.full((LANES - N_EXPERTS, L), -1, jnp.int32)
    pos_t_ref[0] = jnp.concatenate([pos, pad], axis=0).T


def ec_select(aff, cap):
    B, L, _ = aff.shape
    return pl.pallas_call(
        functools.partial(_select_body, cap),
        grid=(B,),
        in_specs=[pl.BlockSpec((1, L, LANES), lambda b: (b, 0, 0))],
        out_specs=[pl.BlockSpec((1, N_EXPERTS, L), lambda b: (b, 0, 0)),
                   pl.BlockSpec((1, N_EXPERTS, L), lambda b: (b, 0, 0)),
                   pl.BlockSpec((1, L, LANES), lambda b: (b, 0, 0))],
        out_shape=[jax.ShapeDtypeStruct((B, N_EXPERTS, L), jnp.int32),
                   jax.ShapeDtypeStruct((B, N_EXPERTS, L), jnp.float32),
                   jax.ShapeDtypeStruct((B, L, LANES), jnp.int32)],
        compiler_params=_cparams("arbitrary"), name="ec_select",
    )(aff)


def _gather_body(h_ref, pos_e_ref, gate_e_ref, xe_ref, gc_ref):
    e = pl.program_id(1)
    cap = xe_ref.shape[2]
    L = h_ref.shape[1]
    pos_row = pos_e_ref[0, pl.ds(e, 1), :]
    gate_row = gate_e_ref[0, pl.ds(e, 1), :]
    hit = lax.broadcasted_iota(jnp.int32, (cap, L), 0) == pos_row
    xe_ref[0, 0] = jnp.dot(hit.astype(jnp.bfloat16), h_ref[0],
                           preferred_element_type=jnp.float32).astype(jnp.bfloat16)
    gc_ref[0, 0] = jnp.sum(jnp.where(hit, gate_row, 0.0), axis=1, keepdims=True)


def ec_gather(h, pos_e, gate_e, cap):
    B, L, D = h.shape
    return pl.pallas_call(
        _gather_body,
        grid=(B, N_EXPERTS),
        in_specs=[pl.BlockSpec((1, L, D), lambda b, e: (b, 0, 0)),
                  pl.BlockSpec((1, N_EXPERTS, L), lambda b, e: (b, 0, 0)),
                  pl.BlockSpec((1, N_EXPERTS, L), lambda b, e: (b, 0, 0))],
        out_specs=[pl.BlockSpec((1, 1, cap, D), lambda b, e: (b, e, 0, 0)),
                   pl.BlockSpec((1, 1, cap, 1), lambda b, e: (b, e, 0, 0))],
        out_shape=[jax.ShapeDtypeStruct((B, N_EXPERTS, cap, D), jnp.bfloat16),
                   jax.ShapeDtypeStruct((B, N_EXPERTS, cap, 1), jnp.float32)],
        compiler_params=_cparams("arbitrary", "arbitrary"), name="ec_gather",
    )(h, pos_e, gate_e)


def _experts_body(nf, xe_ref, gc_ref, w1_ref, w3_ref, w2_ref, ye_ref, hid_ref):
    s = pl.program_id(1)
    B, _, cap, D = xe_ref.shape

    @pl.when(s < nf)
    def _():
        x = xe_ref[...].reshape(B * cap, D)
        h1 = jnp.dot(x, w1_ref[0].astype(jnp.bfloat16), preferred_element_type=jnp.float32)
        h3 = jnp.dot(x, w3_ref[0].astype(jnp.bfloat16), preferred_element_type=jnp.float32)
        hid_ref[s] = (h1 * jax.nn.sigmoid(h1) * h3).astype(jnp.bfloat16)

    @pl.when(s >= nf)
    def _():
        hid = jnp.concatenate([hid_ref[f] for f in range(nf)], axis=1) if nf > 1 else hid_ref[0]
        y = jnp.dot(hid, w2_ref[0].astype(jnp.bfloat16), preferred_element_type=jnp.float32)
        y = y * gc_ref[...].reshape(B * cap, 1)
        ye_ref[...] = y.astype(jnp.bfloat16).reshape(ye_ref.shape)


def ec_experts(xe, gc, w1, w3, w2, tf=256, tn=1024):
    B, E, cap, D = xe.shape
    F = w1.shape[2]
    tf = min(tf, F)
    tn = min(tn, D)
    nf, nn = F // tf, D // tn
    return pl.pallas_call(
        functools.partial(_experts_body, nf),
        grid=(E, nf + nn),
        in_specs=[pl.BlockSpec((B, 1, cap, D), lambda e, s: (0, e, 0, 0), pipeline_mode=pl.Buffered(1)),
                  pl.BlockSpec((B, 1, cap, 1), lambda e, s: (0, e, 0, 0)),
                  pl.BlockSpec((1, D, tf), lambda e, s: (e, 0, jnp.minimum(s, nf - 1))),
                  pl.BlockSpec((1, D, tf), lambda e, s: (e, 0, jnp.minimum(s, nf - 1))),
                  pl.BlockSpec((1, F, tn), lambda e, s: (e, 0, jnp.maximum(s - nf, 0)))],
        out_specs=pl.BlockSpec((B, 1, cap, tn), lambda e, s: (0, e, 0, jnp.maximum(s - nf, 0))),
        out_shape=jax.ShapeDtypeStruct((B, E, cap, D), jnp.bfloat16),
        scratch_shapes=[pltpu.VMEM((nf, B * cap, tf), jnp.bfloat16)],
        compiler_params=_cparams("arbitrary", "arbitrary"), name="ec_experts",
    )(xe, gc, w1, w3, w2)


def _combine_body(ctx_row, cap, pos_t_ref, ye_ref, x_ref, gate_ref, o_ref):
    row = pl.program_id(0) if ctx_row is None else ctx_row
    K = ye_ref.shape[1]
    assert cap & (cap - 1) == 0 and cap <= 256
    col_expert = lax.broadcasted_iota(jnp.int32, (LANES, K), 1) >> (cap.bit_length() - 1)
    spread = (col_expert == lax.broadcasted_iota(jnp.int32, (LANES, K), 0)).astype(jnp.bfloat16)
    pos = jnp.dot(pos_t_ref[0].astype(jnp.float32).astype(jnp.bfloat16), spread,
                  preferred_element_type=jnp.float32)
    col_slot = (lax.broadcasted_iota(jnp.int32, (1, K), 1) & (cap - 1)).astype(jnp.float32)
    hit = (pos == col_slot).astype(jnp.bfloat16)
    y = jnp.dot(hit, ye_ref[0], preferred_element_type=jnp.float32)
    o_ref[0] = x_ref[0] + _mod_row(gate_ref, row) * y


def ec_combine(x, pos_t, ye, mod, k_gate, cap, ctx_row=None, tt=512, tn=1024):
    B, L, D = x.shape
    tt = min(tt, L)
    tn = min(tn, D)
    K = N_EXPERTS * cap
    return pl.pallas_call(
        functools.partial(_combine_body, ctx_row, cap),
        grid=(B, D // tn, L // tt),
        in_specs=[pl.BlockSpec((1, tt, LANES), lambda b, n, t: (b, t, 0)),
                  pl.BlockSpec((1, K, tn), lambda b, n, t: (b, 0, n)),
                  pl.BlockSpec((1, tt, tn), lambda b, n, t: (b, t, n)),
                  pl.BlockSpec((MOD_ROWS, tn), lambda b, n, t: (0, k_gate * (D // tn) + n))],
        out_specs=pl.BlockSpec((1, tt, tn), lambda b, n, t: (b, t, n)),
        out_shape=jax.ShapeDtypeStruct((B, L, D), jnp.float32),
        compiler_params=_cparams("arbitrary", "arbitrary", "arbitrary"), name="ec_combine",
    )(pos_t, ye.reshape(B, K, D), x, mod)


def moe_block(x, gain, mod, router, w1, w3, w2, ctx_row=None):
    B, L, D = x.shape
    cap = EC_CAPACITY * L // N_EXPERTS
    h, aff = normmod(x, gain, mod, 3, 4, ctx_row=ctx_row, router=router)
    pos_e, gate_e, pos_t = ec_select(aff, cap)
    xe, gc = ec_gather(h, pos_e, gate_e, cap)
    ye = ec_experts(xe, gc, w1, w3, w2)
    return ec_combine(x, pos_t, ye, mod, 5, cap, ctx_row=ctx_row)


def kernel(x, c, ctx, c_ctx, ada_w, ada_b, norm_mix, norm_ffn, w_in, pool_w, pool_scale, conv_w, a_log_f, a_log_b, dt_bias_f, dt_bias_b, gdn_norm, w_out_ab, w_qkv, lam_q1, lam_k1, lam_q2, lam_k2, subln, w_out_c, router, w1, w3, w2, final_norm):
    depth = ada_w.shape[0]
    B, L, D = x.shape
    Lc = ctx.shape[1]
    assert B < MOD_ROWS
    ctx_row = B
    cos, sin = axial_rope_tables(L // GRID_W)
    cond = jnp.zeros((MOD_ROWS, D), jnp.float32).at[:B].set(jax.nn.silu(c)).at[ctx_row].set(jax.nn.silu(c_ctx))
    cond = cond.astype(jnp.bfloat16)
    x_lat, x_ctx = x, ctx
    for layer in range(depth):
        last = layer == depth - 1
        j = layer // 2
        mod = proj(cond, ada_w[layer], tn=512) + ada_b[layer]
        h_lat = normmod(x_lat, norm_mix[layer], mod, 0, 1)
        h_ctx = normmod(x_ctx, norm_mix[layer], mod, 0, 1, ctx_row=ctx_row)
        if layer % 2 == 0:
            y_ctx, y_lat = pool_gdn_mixer(h_ctx, h_lat, w_in[j], pool_w[j], pool_scale[j], conv_w[j],
                                          a_log_f[j], a_log_b[j], dt_bias_f[j], dt_bias_b[j], gdn_norm[j],
                                          w_out_ab[j], not last)
            w_out = w_out_ab[j]
        else:
            lambda_init = 0.8 - 0.6 * math.exp(-0.3 * layer)
            y_ctx = None
            y_lat = diff_attn_mixer(h_ctx, h_lat, w_qkv[j], lam_q1[j], lam_k1[j], lam_q2[j], lam_k2[j],
                                    subln[j], w_out_c[j], lambda_init, cos, sin, not last)
            w_out = w_out_c[j]
        x_lat = proj(y_lat.reshape(B * L, D), w_out, tm=1024,
                     residual=(x_lat.reshape(B * L, D), mod, 2, L, None)).reshape(B, L, D)
        x_lat = moe_block(x_lat, norm_ffn[layer], mod, router[layer], w1[layer], w3[layer], w2[layer])
        if not last:
            x_ctx = proj(y_ctx.reshape(B * Lc, D), w_out, tm=1024,
                         residual=(x_ctx.reshape(B * Lc, D), mod, 2, Lc, ctx_row)).reshape(B, Lc, D)
            x_ctx = moe_block(x_ctx, norm_ffn[layer], mod, router[layer], w1[layer], w3[layer], w2[layer],
                              ctx_row=ctx_row)
    return final_rmsnorm(x_lat, final_norm)
```

```python
import functools
import math

import jax
import jax.numpy as jnp
from jax import lax
from jax.experimental import pallas as pl
from jax.experimental.pallas import tpu as pltpu

D_MODEL = 4096
GRID_W = 64
NORM_EPS = 1e-6
POOL_WINDOWS = (2, 4, 8, 16)
POOL_GROUP = D_MODEL // 16
POOL_WIDTH = len(POOL_WINDOWS) * POOL_GROUP
GDN_HEAD_DIM = 128
GDN_HEADS = (D_MODEL - POOL_WIDTH) // GDN_HEAD_DIM
GDN_WIDTH = GDN_HEADS * GDN_HEAD_DIM
GDN_CONV = 5
GDN_CHUNK = 64
QKV_COLS = 3 * GDN_WIDTH
DIFF_HEAD_DIM = 128
DIFF_HEADS = D_MODEL // (2 * DIFF_HEAD_DIM)
ROPE_BASE = 10000.0
N_EXPERTS = 16
EC_CAPACITY = 2

LANES = 128
MOD_ROWS = 8
PREFIX_BLOCK = 256
VMEM_LIMIT_BYTES = 56 * 1024 * 1024

PROJ_TN = 512
GDN_PAIR = 2 * GDN_CHUNK
CONV_PAD = 8
ROW_BLOCK = 128
STATE_W = 4 * GDN_HEADS
REST_TILE0 = QKV_COLS // LANES
IN_COLS_PADDED = -(-(QKV_COLS + STATE_W + POOL_WIDTH + GDN_WIDTH) // PROJ_TN) * PROJ_TN
Z_TILE0 = REST_TILE0 + (STATE_W + POOL_WIDTH) // LANES
Z_SHIFT = LANES - (STATE_W + POOL_WIDTH) % LANES
POOL_PAD = 16
POOL_IN_W = 1152


def _cparams(*sem):
    return pltpu.CompilerParams(dimension_semantics=sem, vmem_limit_bytes=VMEM_LIMIT_BYTES)


def _mod_row(mod_ref, row):
    return mod_ref[pl.ds(row, 1), :]


def _proj_body(rows_per_sample, ctx_row, a_ref, w_ref, *rest):
    if rows_per_sample is None:
        o_ref, wb_ref = rest
    else:
        x_ref, gate_ref, o_ref, wb_ref = rest
    i = pl.program_id(1)

    @pl.when(i == 0)
    def _():
        wb_ref[...] = w_ref[...].astype(jnp.bfloat16)

    y = jnp.dot(a_ref[...], wb_ref[...], preferred_element_type=jnp.float32)
    if rows_per_sample is None:
        o_ref[...] = y.astype(o_ref.dtype)
    else:
        row = (i * a_ref.shape[0]) // rows_per_sample if ctx_row is None else ctx_row
        o_ref[...] = x_ref[...] + _mod_row(gate_ref, row) * y


def proj(a, w, *, col0=0, ncols=None, tm=512, tn=PROJ_TN, out_dtype=jnp.float32, residual=None):
    M, K = a.shape
    ncols = w.shape[1] - col0 if ncols is None else ncols
    tm = min(tm, M)
    assert M % tm == 0 and ncols % tn == 0 and col0 % tn == 0, (M, ncols, col0, tm, tn)
    j0 = col0 // tn
    in_specs = [pl.BlockSpec((tm, K), lambda j, i: (i, 0)),
                pl.BlockSpec((K, tn), lambda j, i: (0, j0 + j))]
    args = [a, w]
    rows_per_sample = ctx_row = None
    if residual is not None:
        x, mod, k_gate, rows_per_sample, ctx_row = residual
        assert ctx_row is not None or rows_per_sample % tm == 0
        in_specs += [pl.BlockSpec((tm, tn), lambda j, i: (i, j)),
                     pl.BlockSpec((MOD_ROWS, tn), lambda j, i: (0, k_gate * (ncols // tn) + j))]
        args += [x, mod]
    return pl.pallas_call(
        functools.partial(_proj_body, rows_per_sample, ctx_row),
        grid=(ncols // tn, M // tm),
        in_specs=in_specs,
        out_specs=pl.BlockSpec((tm, tn), lambda j, i: (i, j)),
        out_shape=jax.ShapeDtypeStruct((M, ncols), out_dtype),
        scratch_shapes=[pltpu.VMEM((K, tn), jnp.bfloat16)],
        compiler_params=_cparams("arbitrary", "arbitrary"), name="proj",
    )(*args)


def _normmod_body(ctx_row, with_router, x_ref, g_ref, shift_ref, scale_ref, *rest):
    row = pl.program_id(0) if ctx_row is None else ctx_row
    x = x_ref[0]
    y = x * lax.rsqrt(jnp.mean(x * x, axis=-1, keepdims=True) + NORM_EPS) * g_ref[...]
    h = (y * (1.0 + _mod_row(scale_ref, row)) + _mod_row(shift_ref, row)).astype(jnp.bfloat16)
    if not with_router:
        (h_ref,) = rest
        h_ref[0] = h
        return
    r_ref, h_ref, aff_ref = rest
    h_ref[0] = h
    logits = jnp.dot(h, r_ref[...], preferred_element_type=jnp.float32)
    lane = lax.broadcasted_iota(jnp.int32, logits.shape, 1)
    logits = jnp.where(lane < N_EXPERTS, logits, -jnp.inf)
    e = jnp.exp(logits - jnp.max(logits, axis=-1, keepdims=True))
    aff_ref[0] = e / jnp.sum(e, axis=-1, keepdims=True)


def normmod(x, gain, mod, k_shift, k_scale, ctx_row=None, router=None, tr=256):
    B, L, D = x.shape
    tr = min(tr, L)
    in_specs = [pl.BlockSpec((1, tr, D), lambda b, t: (b, t, 0)),
                pl.BlockSpec((1, D), lambda b, t: (0, 0)),
                pl.BlockSpec((MOD_ROWS, D), lambda b, t: (0, k_shift)),
                pl.BlockSpec((MOD_ROWS, D), lambda b, t: (0, k_scale))]
    args = [x, gain.reshape(1, D), mod, mod]
    out_specs = [pl.BlockSpec((1, tr, D), lambda b, t: (b, t, 0))]
    out_shape = [jax.ShapeDtypeStruct((B, L, D), jnp.bfloat16)]
    if router is not None:
        rp = jnp.pad(router.astype(jnp.bfloat16), ((0, 0), (0, LANES - router.shape[1])))
        in_specs.append(pl.BlockSpec((D, LANES), lambda b, t: (0, 0)))
        args.append(rp)
        out_specs.append(pl.BlockSpec((1, tr, LANES), lambda b, t: (b, t, 0)))
        out_shape.append(jax.ShapeDtypeStruct((B, L, LANES), jnp.float32))
    out = pl.pallas_call(
        functools.partial(_normmod_body, ctx_row, router is not None),
        grid=(B, L // tr), in_specs=in_specs, out_specs=out_specs, out_shape=out_shape,
        compiler_params=_cparams("arbitrary", "arbitrary"), name="normmod",
    )(*args)
    return out if router is not None else out[0]


def _final_norm_body(x_ref, g_ref, o_ref):
    x = x_ref[...]
    o_ref[...] = x * lax.rsqrt(jnp.mean(x * x, axis=-1, keepdims=True) + NORM_EPS) * g_ref[...]


def final_rmsnorm(x, gain, tr=512):
    B, L, D = x.shape
    M = B * L
    tr = min(tr, M)
    return pl.pallas_call(
        _final_norm_body,
        grid=(M // tr,),
        in_specs=[pl.BlockSpec((tr, D), lambda i: (i, 0)), pl.BlockSpec((1, D), lambda i: (0, 0))],
        out_specs=pl.BlockSpec((tr, D), lambda i: (i, 0)),
        out_shape=jax.ShapeDtypeStruct((M, D), x.dtype),
        compiler_params=_cparams("arbitrary"), name="final_rmsnorm",
    )(x.reshape(M, D), gain.reshape(1, D)).reshape(B, L, D)


def _gates_body(st_ref, alog_ref, dtb_ref, rows_ref, cols_ref):
    T = st_ref.shape[1]
    H = GDN_HEADS
    x = st_ref[0].T
    t = x + dtb_ref[...]
    softplus = jnp.maximum(t, 0.0) + jnp.log1p(jnp.exp(-jnp.abs(t)))
    r_full = lax.broadcasted_iota(jnp.int32, x.shape, 0)
    val = jnp.where(r_full < 2 * H, -jnp.exp(alog_ref[...]) * softplus, jax.nn.sigmoid(x))
    pos = lax.broadcasted_iota(jnp.int32, (LANES, LANES), 1) & (GDN_CHUNK - 1)
    r = lax.broadcasted_iota(jnp.int32, (LANES, LANES), 0)
    tiles = []
    for c in range(T // LANES):
        v = val[:, c * LANES:(c + 1) * LANES]
        pre, suf, s = v, v, 1
        while s < GDN_CHUNK:
            pre = pre + jnp.where(pos >= s, pltpu.roll(pre, s, 1), 0.0)
            suf = suf + jnp.where(pos < GDN_CHUNK - s, pltpu.roll(suf, LANES - s, 1), 0.0)
            s *= 2
        tiles.append(jnp.where(r < H, pre, jnp.where(r < 2 * H, suf, v)))
    out = jnp.concatenate(tiles, axis=1) if len(tiles) > 1 else tiles[0]
    rows_ref[0] = out
    cols_ref[0] = out.T


def gdn_gates(p, a_log_f, a_log_b, dt_bias_f, dt_bias_b):
    B, T, _ = p.shape
    pad = jnp.zeros((LANES - 2 * GDN_HEADS,), jnp.float32)
    alog = jnp.concatenate([a_log_f, a_log_b, pad]).astype(jnp.float32).reshape(LANES, 1)
    dtb = jnp.concatenate([dt_bias_f, dt_bias_b, pad]).astype(jnp.float32).reshape(LANES, 1)
    return pl.pallas_call(
        _gates_body,
        grid=(B,),
        in_specs=[pl.BlockSpec((1, T, LANES), lambda b: (b, 0, REST_TILE0)),
                  pl.BlockSpec((LANES, 1), lambda b: (0, 0)),
                  pl.BlockSpec((LANES, 1), lambda b: (0, 0))],
        out_specs=[pl.BlockSpec((1, LANES, T), lambda b: (b, 0, 0)),
                   pl.BlockSpec((1, T, LANES), lambda b: (b, 0, 0))],
        out_shape=[jax.ShapeDtypeStruct((B, LANES, T), jnp.float32),
                   jax.ShapeDtypeStruct((B, T, LANES), jnp.float32)],
        compiler_params=_cparams("arbitrary"), name="gdn_gates",
    )(p, alog, dtb)


def _conv_silu(x_ref, w_ref, pad_sc, dst_sc, T, unit_norm):
    half = GDN_CONV // 2
    zeros = jnp.zeros((CONV_PAD, LANES), jnp.float32)
    pad_sc[0:CONV_PAD, :] = zeros
    pad_sc[CONV_PAD:CONV_PAD + T, :] = x_ref[0]
    pad_sc[CONV_PAD + T:2 * CONV_PAD + T, :] = zeros
    w = w_ref[...]
    for r0 in range(0, T, ROW_BLOCK):
        y = jnp.zeros((ROW_BLOCK, LANES), jnp.float32)
        for j in range(GDN_CONV):
            s = CONV_PAD + r0 + j - half
            y = y + w[j:j + 1, :] * pad_sc[s:s + ROW_BLOCK, :]
        y = y * jax.nn.sigmoid(y)
        if unit_norm:
            y = y * lax.rsqrt(jnp.sum(y * y, axis=-1, keepdims=True) + NORM_EPS)
        dst_sc[r0:r0 + ROW_BLOCK, :] = y


def _nt(a, b):
    return lax.dot_general(a, b, (((1,), (1,)), ((), ())), preferred_element_type=jnp.float32)


def _gdn_segment(T, h, refs, sc, s_init):
    pq_ref, pk_ref, pv_ref, cols_ref, rows_ref, za_ref, zb_ref, y_ref = refs
    (cwq_ref, cwk_ref, cwv_ref, gn_ref, pad_sc, q_sc, k_sc, v_sc, u_sc, w_sc, qd_sc, kdt_sc, attn_sc, gend_sc,
     o_sc, grow_sc) = sc
    C, P, H = GDN_CHUNK, GDN_PAIR, GDN_HEADS
    n_pairs = T // P
    bf16 = jnp.bfloat16
    scale = GDN_HEAD_DIM ** -0.5
    for d in range(2):
        grow_sc[d, 0:1, 0:T] = rows_ref[0, pl.ds(d * H + h, 1), :]
    _conv_silu(pq_ref, cwq_ref, pad_sc, q_sc, T, True)
    _conv_silu(pk_ref, cwk_ref, pad_sc, k_sc, T, True)
    _conv_silu(pv_ref, cwv_ref, pad_sc, v_sc, T, False)

    lane = lax.broadcasted_iota(jnp.int32, (P, LANES), 1)
    ii = lax.broadcasted_iota(jnp.int32, (C, C), 0)
    jj = lax.broadcasted_iota(jnp.int32, (C, C), 1)

    def prep(m, carry):
        r0 = pl.multiple_of(m * P, P)
        q2 = q_sc[pl.ds(r0, P), :]
        k2 = k_sc[pl.ds(r0, P), :]
        v2 = v_sc[pl.ds(r0, P), :]
        cols = cols_ref[0, pl.ds(r0, P), :]
        qs = q2 * scale
        for d in range(2):
            gcol = jnp.sum(jnp.where(lane == d * H + h, cols, 0.0), axis=1, keepdims=True)
            bcol = jnp.sum(jnp.where(lane == (2 + d) * H + h, cols, 0.0), axis=1, keepdims=True)
            grow = grow_sc[d, 0:1, pl.ds(r0, P)]
            keep = (ii >= jj) if d == 0 else (ii <= jj)
            strict = (ii > jj) if d == 0 else (ii < jj)
            eg = jnp.exp(gcol)
            kb = k2 * bcol
            rhs_u = v2 * bcol
            rhs_w = kb * eg
            qd_sc[d, pl.ds(r0, P), :] = (qs * eg).astype(bf16)
            kd = []
            for c in range(2):
                sl = slice(c * C, (c + 1) * C)
                rc = pl.multiple_of(r0 + c * C, C)
                gc = gcol[sl]
                decay = jnp.exp(jnp.where(keep, gc - grow[:, sl], -jnp.inf))
                kh = k2[sl].astype(bf16)
                neg_a = jnp.where(strict, -(_nt(kb[sl].astype(bf16), kh) * decay), 0.0)
                e, p = neg_a, neg_a
                for _ in range(C.bit_length() - 2):
                    pb = p.astype(bf16)
                    p = jnp.dot(pb, pb, preferred_element_type=jnp.float32)
                    e = e + p + jnp.dot(e.astype(bf16), p.astype(bf16), preferred_element_type=jnp.float32)
                eb = e.astype(bf16)
                u_sc[d, pl.ds(rc, C), :] = rhs_u[sl] + jnp.dot(eb, rhs_u[sl].astype(bf16),
                                                               preferred_element_type=jnp.float32)
                w_sc[d, pl.ds(rc, C), :] = (rhs_w[sl] + jnp.dot(eb, rhs_w[sl].astype(bf16),
                                                                preferred_element_type=jnp.float32)).astype(bf16)
                attn_sc[d, pl.ds(rc, C), :] = (_nt(qs[sl].astype(bf16), kh) * decay).astype(bf16)
                g_last = gc[C - 1:C] if d == 0 else gc[0:1]
                kd.append(k2[sl] * jnp.exp(g_last - gc))
                gend_sc[d, pl.ds(pl.multiple_of((2 * m + c) * 8, 8), 8), :] = jnp.broadcast_to(
                    jnp.exp(g_last), (8, LANES))
            kdt_sc[d, :, pl.ds(r0, P)] = jnp.concatenate(kd, axis=0).T.astype(bf16)
        return carry

    lax.fori_loop(0, n_pairs, prep, 0)

    def chunk_step(d, s, rc, kdt, n):
        sb = s.astype(bf16)
        v_new = u_sc[d, pl.ds(rc, C), :] - jnp.dot(w_sc[d, pl.ds(rc, C), :], sb, preferred_element_type=jnp.float32)
        vb = v_new.astype(bf16)
        o_sc[d, pl.ds(rc, C), :] = (jnp.dot(qd_sc[d, pl.ds(rc, C), :], sb, preferred_element_type=jnp.float32)
                                    + jnp.dot(attn_sc[d, pl.ds(rc, C), :], vb, preferred_element_type=jnp.float32))
        g_end = gend_sc[d, pl.ds(pl.multiple_of(n * 8, 8), 8), :][0:1]
        return s * g_end + jnp.dot(kdt, vb, preferred_element_type=jnp.float32)

    def scan(m, carry):
        s_f, s_b = carry
        rf = pl.multiple_of(m * P, P)
        mb = n_pairs - 1 - m
        rb = pl.multiple_of(mb * P, P)
        kdt_f = kdt_sc[0, :, pl.ds(rf, P)]
        kdt_b = kdt_sc[1, :, pl.ds(rb, P)]
        for c in range(2):
            s_f = chunk_step(0, s_f, pl.multiple_of(rf + c * C, C), kdt_f[:, c * C:(c + 1) * C], 2 * m + c)
            cb = 1 - c
            s_b = chunk_step(1, s_b, pl.multiple_of(rb + cb * C, C), kdt_b[:, cb * C:(cb + 1) * C], 2 * mb + cb)
        return s_f, s_b

    s_f, s_b = lax.fori_loop(0, n_pairs, scan, s_init)

    lane_b = lax.broadcasted_iota(jnp.int32, (ROW_BLOCK, LANES), 1)
    for r0 in range(0, T, ROW_BLOCK):
        o = o_sc[0, r0:r0 + ROW_BLOCK, :] + o_sc[1, r0:r0 + ROW_BLOCK, :]
        z = pltpu.roll(jnp.where(lane_b >= LANES - Z_SHIFT, za_ref[0, r0:r0 + ROW_BLOCK, :],
                                 zb_ref[0, r0:r0 + ROW_BLOCK, :]), Z_SHIFT, 1)
        y = o * lax.rsqrt(jnp.mean(o * o, axis=-1, keepdims=True) + NORM_EPS) * gn_ref[...]
        y_ref[0, r0:r0 + ROW_BLOCK, :] = (y * (z * jax.nn.sigmoid(z))).astype(y_ref.dtype)
    return s_f, s_b


def _gdn_body(Tc, Tl, *refs):
    h = pl.program_id(1)
    ctx_refs = refs[0:7]
    lat_refs = refs[7:14]
    shared = refs[14:18]
    y_ctx_ref, y_lat_ref = refs[18:20]
    sc = shared + refs[20:]
    zero = jnp.zeros((GDN_HEAD_DIM, GDN_HEAD_DIM), jnp.float32)
    states = _gdn_segment(Tc, h, ctx_refs + (y_ctx_ref,), sc, (zero, zero))
    _gdn_segment(Tl, h, lat_refs + (y_lat_ref,), sc, states)


def gdn_mix(p_ctx, gates_ctx, p_lat, gates_lat, conv_w, gdn_norm):
    B, Tc, _ = p_ctx.shape
    Tl = p_lat.shape[1]
    H, dh = GDN_HEADS, GDN_HEAD_DIM

    def seg_specs(T):
        return [pl.BlockSpec((1, T, dh), lambda b, h: (b, 0, h)),
                pl.BlockSpec((1, T, dh), lambda b, h: (b, 0, H + h)),
                pl.BlockSpec((1, T, dh), lambda b, h: (b, 0, 2 * H + h)),
                pl.BlockSpec((1, T, LANES), lambda b, h: (b, 0, 0)),
                pl.BlockSpec((1, LANES, T), lambda b, h: (b, 0, 0)),
                pl.BlockSpec((1, T, LANES), lambda b, h: (b, 0, Z_TILE0 + h)),
                pl.BlockSpec((1, T, LANES), lambda b, h: (b, 0, Z_TILE0 + 1 + h))]

    shared_specs = [pl.BlockSpec((GDN_CONV, dh), lambda b, h: (0, h)),
                    pl.BlockSpec((GDN_CONV, dh), lambda b, h: (0, H + h)),
                    pl.BlockSpec((GDN_CONV, dh), lambda b, h: (0, 2 * H + h)),
                    pl.BlockSpec((1, dh), lambda b, h: (0, 0))]
    rows_c, cols_c = gates_ctx
    rows_l, cols_l = gates_lat
    f32, bf16 = jnp.float32, jnp.bfloat16
    scratch = [pltpu.VMEM((Tl + 2 * CONV_PAD, LANES), f32),
               pltpu.VMEM((Tl, dh), f32), pltpu.VMEM((Tl, dh), f32), pltpu.VMEM((Tl, dh), f32),
               pltpu.VMEM((2, Tl, dh), f32), pltpu.VMEM((2, Tl, dh), bf16), pltpu.VMEM((2, Tl, dh), bf16),
               pltpu.VMEM((2, dh, Tl), bf16), pltpu.VMEM((2, Tl, GDN_CHUNK), bf16),
               pltpu.VMEM((2, Tl // GDN_CHUNK * 8, LANES), f32), pltpu.VMEM((2, Tl, dh), f32),
               pltpu.VMEM((2, 8, Tl), f32)]
    return pl.pallas_call(
        functools.partial(_gdn_body, Tc, Tl),
        grid=(B, H),
        in_specs=seg_specs(Tc) + seg_specs(Tl) + shared_specs,
        out_specs=[pl.BlockSpec((1, Tc, dh), lambda b, h: (b, 0, h)),
                   pl.BlockSpec((1, Tl, dh), lambda b, h: (b, 0, h))],
        out_shape=[jax.ShapeDtypeStruct((B, Tc, GDN_WIDTH), bf16), jax.ShapeDtypeStruct((B, Tl, GDN_WIDTH), bf16)],
        scratch_shapes=scratch,
        compiler_params=_cparams("arbitrary", "arbitrary"), name="gdn_mix",
    )(p_ctx, p_ctx, p_ctx, cols_c, rows_c, p_ctx, p_ctx,
      p_lat, p_lat, p_lat, cols_l, rows_l, p_lat, p_lat,
      conv_w, conv_w, conv_w, gdn_norm.reshape(1, dh))


def _pool_body(x_ref, w_ref, sc_ref, y_ref, pad_sc):
    T = x_ref.shape[1]
    G = POOL_GROUP
    zeros = jnp.zeros((POOL_PAD, G), jnp.float32)
    t = lax.broadcasted_iota(jnp.int32, (ROW_BLOCK, 1), 0)
    for i, win in enumerate(POOL_WINDOWS):
        half = win // 2
        lo = (STATE_W + i * G) // LANES * LANES
        off = STATE_W + i * G - lo
        pad_sc[0:POOL_PAD, :] = zeros
        pad_sc[POOL_PAD:POOL_PAD + T, :] = x_ref[0, :, lo:lo + G + LANES][:, off:off + G]
        pad_sc[POOL_PAD + T:2 * POOL_PAD + T, :] = zeros
        wb = w_ref[i].astype(jnp.bfloat16)
        for r0 in range(0, T, ROW_BLOCK):
            tot = jnp.zeros((ROW_BLOCK, G), jnp.float32)
            for s in range(-half, half):
                tot = tot + pad_sc[POOL_PAD + r0 + s:POOL_PAD + r0 + s + ROW_BLOCK, :]
            cnt = (jnp.minimum(t + r0 + half, T) - jnp.maximum(t + r0 - half, 0)).astype(jnp.float32)
            centred = tot / cnt - pad_sc[POOL_PAD + r0:POOL_PAD + r0 + ROW_BLOCK, :]
            y = jnp.dot(centred.astype(jnp.bfloat16), wb, preferred_element_type=jnp.float32)
            y_ref[0, r0:r0 + ROW_BLOCK, i * G:(i + 1) * G] = (y * sc_ref[:, i * G:(i + 1) * G]).astype(y_ref.dtype)


def pool_mix(p, pool_w, pool_scale):
    B, T, _ = p.shape
    n = len(POOL_WINDOWS)
    assert QKV_COLS % POOL_IN_W == 0 and STATE_W + POOL_WIDTH <= POOL_IN_W
    return pl.pallas_call(
        _pool_body,
        grid=(B,),
        in_specs=[pl.BlockSpec((1, T, POOL_IN_W), lambda b: (b, 0, QKV_COLS // POOL_IN_W)),
                  pl.BlockSpec((n, POOL_GROUP, POOL_GROUP), lambda b: (0, 0, 0)),
                  pl.BlockSpec((1, POOL_WIDTH), lambda b: (0, 0))],
        out_specs=pl.BlockSpec((1, T, POOL_WIDTH), lambda b: (b, 0, 0)),
        out_shape=jax.ShapeDtypeStruct((B, T, POOL_WIDTH), jnp.bfloat16),
        scratch_shapes=[pltpu.VMEM((T + 2 * POOL_PAD, POOL_GROUP), jnp.float32)],
        compiler_params=_cparams("arbitrary"), name="pool_mix",
    )(p, pool_w, pool_scale.reshape(1, POOL_WIDTH))


def pool_gdn_mixer(h_ctx, h_lat, w_in, pool_w, pool_scale, conv_w, a_log_f, a_log_b, dt_bias_f, dt_bias_b, gdn_norm):
    B, Tl, D = h_lat.shape
    Tc = h_ctx.shape[1]
    p_lat = proj(h_lat.reshape(B * Tl, D), w_in, ncols=IN_COLS_PADDED, tm=1024).reshape(B, Tl, IN_COLS_PADDED)
    p_ctx = proj(h_ctx.reshape(B * Tc, D), w_in, ncols=IN_COLS_PADDED, tm=1024).reshape(B, Tc, IN_COLS_PADDED)
    gates_ctx = gdn_gates(p_ctx, a_log_f, a_log_b, dt_bias_f, dt_bias_b)
    gates_lat = gdn_gates(p_lat, a_log_f, a_log_b, dt_bias_f, dt_bias_b)
    g_ctx, g_lat = gdn_mix(p_ctx, gates_ctx, p_lat, gates_lat, conv_w, gdn_norm)
    y_ctx = jnp.concatenate([pool_mix(p_ctx, pool_w, pool_scale), g_ctx], axis=-1)
    y_lat = jnp.concatenate([pool_mix(p_lat, pool_w, pool_scale), g_lat], axis=-1)
    return y_ctx, y_lat


def axial_rope_tables(rows):
    pos_row = jnp.repeat(jnp.arange(rows), GRID_W).astype(jnp.float32)
    pos_col = jnp.tile(jnp.arange(GRID_W), rows).astype(jnp.float32)
    quarter = DIFF_HEAD_DIM // 4
    inv_freq = ROPE_BASE ** (-jnp.arange(quarter, dtype=jnp.float32) / quarter)
    ang_r = pos_row[:, None] * inv_freq[None, :]
    ang_c = pos_col[:, None] * inv_freq[None, :]
    ang = jnp.concatenate([ang_r, ang_r, ang_c, ang_c], axis=-1)
    return jnp.cos(ang), jnp.sin(ang)


def _rope(x, cos, sin_signed):
    d = x.shape[-1]
    first = (lax.broadcasted_iota(jnp.int32, x.shape, 1) & (d // 2 - 1)) < d // 4
    partner = jnp.where(first, pltpu.roll(x, d - d // 4, 1), pltpu.roll(x, d // 4, 1))
    return x * cos + partner * sin_signed


def _diff_attn_body(lambda_init, tq, q_ref, kl_ref, vl_ref, kc_ref, vc_ref, cos_ref, sin_ref, lam_ref, g_ref,
                    o_ref, k_sc, v_sc):
    d = DIFF_HEAD_DIM
    L = kl_ref.shape[1]
    scale = d ** -0.5
    lv = lam_ref[...]
    lam = (jnp.exp(jnp.sum(lv[0:1] * lv[1:2], axis=1, keepdims=True))
           - jnp.exp(jnp.sum(lv[2:3] * lv[3:4], axis=1, keepdims=True)) + lambda_init)
    cos = cos_ref[...]
    sin = sin_ref[...]
    for m in range(2):
        k_sc[0:L, m * d:(m + 1) * d] = _rope(kl_ref[0, :, m * d:(m + 1) * d], cos, sin).astype(jnp.bfloat16)
    k_sc[L:, :] = kc_ref[0].astype(jnp.bfloat16)
    v_sc[0:L, :] = vl_ref[0].astype(jnp.bfloat16)
    v_sc[L:, :] = vc_ref[0].astype(jnp.bfloat16)

    def q_tile(t, carry):
        r0 = pl.multiple_of(t * tq, tq)
        cs = cos_ref[pl.ds(r0, tq), :]
        sn = sin_ref[pl.ds(r0, tq), :]
        probs = []
        for m in range(2):
            q = _rope(q_ref[0, pl.ds(r0, tq), m * d:(m + 1) * d], cs, sn).astype(jnp.bfloat16)
            s = lax.dot_general(q, k_sc[:, m * d:(m + 1) * d], (((1,), (1,)), ((), ())),
                                preferred_element_type=jnp.float32) * scale
            e = jnp.exp(s - jnp.max(s, axis=-1, keepdims=True))
            probs.append(e / jnp.sum(e, axis=-1, keepdims=True))
        a = (probs[0] - lam * probs[1]).astype(jnp.bfloat16)
        o = jnp.dot(a, v_sc[...], preferred_element_type=jnp.float32)
        o = o * lax.rsqrt(jnp.mean(o * o, axis=-1, keepdims=True) + NORM_EPS) * g_ref[...]
        o_ref[0, pl.ds(r0, tq), :] = (o * (1 - lambda_init)).astype(o_ref.dtype)
        return carry

    lax.fori_loop(0, L // tq, q_tile, 0)


def diff_attention_lat(qkv_lat, kv_ctx, cos, sin_signed, lam_vecs, subln, lambda_init, tq=256):
    B, L, _ = qkv_lat.shape
    Lc = kv_ctx.shape[1]
    H, hd = DIFF_HEADS, 2 * DIFF_HEAD_DIM
    tq = min(tq, L)
    return pl.pallas_call(
        functools.partial(_diff_attn_body, lambda_init, tq),
        grid=(B, H),
        in_specs=[pl.BlockSpec((1, L, hd), lambda b, h: (b, 0, h)),
                  pl.BlockSpec((1, L, hd), lambda b, h: (b, 0, H + h)),
                  pl.BlockSpec((1, L, hd), lambda b, h: (b, 0, 2 * H + h)),
                  pl.BlockSpec((1, Lc, hd), lambda b, h: (b, 0, h)),
                  pl.BlockSpec((1, Lc, hd), lambda b, h: (b, 0, H + h)),
                  pl.BlockSpec((L, DIFF_HEAD_DIM), lambda b, h: (0, 0)),
                  pl.BlockSpec((L, DIFF_HEAD_DIM), lambda b, h: (0, 0)),
                  pl.BlockSpec((4, DIFF_HEAD_DIM), lambda b, h: (0, 0)),
                  pl.BlockSpec((1, hd), lambda b, h: (0, 0))],
        out_specs=pl.BlockSpec((1, L, hd), lambda b, h: (b, 0, h)),
        out_shape=jax.ShapeDtypeStruct((B, L, H * hd), jnp.bfloat16),
        scratch_shapes=[pltpu.VMEM((L + Lc, hd), jnp.bfloat16), pltpu.VMEM((L + Lc, hd), jnp.bfloat16)],
        compiler_params=_cparams("arbitrary", "arbitrary"), name="diff_attention",
    )(qkv_lat, qkv_lat, qkv_lat, kv_ctx, kv_ctx, cos, sin_signed, lam_vecs, subln.reshape(1, hd))


def diff_attn_mixer(h_ctx, h_lat, w_qkv, lam_q1, lam_k1, lam_q2, lam_k2, subln, lambda_init, cos, sin):
    B, L, D = h_lat.shape
    Lc = h_ctx.shape[1]
    qkv = proj(h_lat.reshape(B * L, D), w_qkv, tm=1024).reshape(B, L, 3 * D)
    kv_ctx = proj(h_ctx.reshape(B * Lc, D), w_qkv, col0=D, tm=1024).reshape(B, Lc, 2 * D)
    quarter = DIFF_HEAD_DIM // 4
    first = (jnp.arange(DIFF_HEAD_DIM) % (2 * quarter)) < quarter
    sin_signed = jnp.where(first[None, :], -sin, sin)
    lam_vecs = jnp.stack([lam_q1, lam_k1, lam_q2, lam_k2]).astype(jnp.float32)
    return diff_attention_lat(qkv, kv_ctx, cos, sin_signed, lam_vecs, subln, lambda_init)


def _count_before(mask):
    R, L = mask.shape
    W = PREFIX_BLOCK
    m = mask.astype(jnp.float32)
    before = (lax.broadcasted_iota(jnp.int32, (W, W), 0) < lax.broadcasted_iota(jnp.int32, (W, W), 1)
              ).astype(jnp.bfloat16)
    carry = jnp.zeros((R, 1), jnp.float32)
    out = []
    for c in range(L // W):
        blk = m[:, c * W:(c + 1) * W]
        out.append(jnp.dot(blk.astype(jnp.bfloat16), before, preferred_element_type=jnp.float32) + carry)
        carry = carry + jnp.sum(blk, axis=1, keepdims=True)
    return out[0] if len(out) == 1 else jnp.concatenate(out, axis=1)


def _select_body(cap, aff_ref, pos_e_ref, gate_e_ref, pos_t_ref):
    L = aff_ref.shape[1]
    a = aff_ref[0].T[:N_EXPERTS]
    bits = pltpu.bitcast(a, jnp.int32)
    thr = jnp.zeros((N_EXPERTS, 1), jnp.int32)
    for bit in range(30, -1, -1):
        cand = thr | (1 << bit)
        cnt = jnp.sum((bits >= cand).astype(jnp.float32), axis=1, keepdims=True)
        thr = jnp.where(cnt >= cap, cand, thr)
    gt = bits > thr
    eq = bits == thr
    need = cap - jnp.sum(gt.astype(jnp.float32), axis=1, keepdims=True)
    sel = gt | (eq & (_count_before(eq) < need))
    pos = jnp.where(sel, _count_before(sel).astype(jnp.int32), -1)
    pos_e_ref[0] = pos
    gate_e_ref[0] = jnp.where(sel, a, 0.0)
    pad = jnp.full((LANES - N_EXPERTS, L), -1, jnp.int32)
    pos_t_ref[0] = jnp.concatenate([pos, pad], axis=0).T


def ec_select(aff, cap):
    B, L, _ = aff.shape
    return pl.pallas_call(
        functools.partial(_select_body, cap),
        grid=(B,),
        in_specs=[pl.BlockSpec((1, L, LANES), lambda b: (b, 0, 0))],
        out_specs=[pl.BlockSpec((1, N_EXPERTS, L), lambda b: (b, 0, 0)),
                   pl.BlockSpec((1, N_EXPERTS, L), lambda b: (b, 0, 0)),
                   pl.BlockSpec((1, L, LANES), lambda b: (b, 0, 0))],
        out_shape=[jax.ShapeDtypeStruct((B, N_EXPERTS, L), jnp.int32),
                   jax.ShapeDtypeStruct((B, N_EXPERTS, L), jnp.float32),
                   jax.ShapeDtypeStruct((B, L, LANES), jnp.int32)],
        compiler_params=_cparams("arbitrary"), name="ec_select",
    )(aff)


def _gather_body(h_ref, pos_e_ref, gate_e_ref, xe_ref, gc_ref):
    e = pl.program_id(1)
    cap = xe_ref.shape[2]
    L = h_ref.shape[1]
    pos_row = pos_e_ref[0, pl.ds(e, 1), :]
    gate_row = gate_e_ref[0, pl.ds(e, 1), :]
    hit = lax.broadcasted_iota(jnp.int32, (cap, L), 0) == pos_row
    xe_ref[0, 0] = jnp.dot(hit.astype(jnp.bfloat16), h_ref[0],
                           preferred_element_type=jnp.float32).astype(jnp.bfloat16)
    gc_ref[0, 0] = jnp.sum(jnp.where(hit, gate_row, 0.0), axis=1, keepdims=True)


def ec_gather(h, pos_e, gate_e, cap):
    B, L, D = h.shape
    return pl.pallas_call(
        _gather_body,
        grid=(B, N_EXPERTS),
        in_specs=[pl.BlockSpec((1, L, D), lambda b, e: (b, 0, 0)),
                  pl.BlockSpec((1, N_EXPERTS, L), lambda b, e: (b, 0, 0)),
                  pl.BlockSpec((1, N_EXPERTS, L), lambda b, e: (b, 0, 0))],
        out_specs=[pl.BlockSpec((1, 1, cap, D), lambda b, e: (b, e, 0, 0)),
                   pl.BlockSpec((1, 1, cap, 1), lambda b, e: (b, e, 0, 0))],
        out_shape=[jax.ShapeDtypeStruct((B, N_EXPERTS, cap, D), jnp.bfloat16),
                   jax.ShapeDtypeStruct((B, N_EXPERTS, cap, 1), jnp.float32)],
        compiler_params=_cparams("arbitrary", "arbitrary"), name="ec_gather",
    )(h, pos_e, gate_e)


def _experts_body(nf, xe_ref, gc_ref, w1_ref, w3_ref, w2_ref, ye_ref, hid_ref):
    s = pl.program_id(1)
    B, _, cap, D = xe_ref.shape

    @pl.when(s < nf)
    def _():
        x = xe_ref[...].reshape(B * cap, D)
        h1 = jnp.dot(x, w1_ref[0].astype(jnp.bfloat16), preferred_element_type=jnp.float32)
        h3 = jnp.dot(x, w3_ref[0].astype(jnp.bfloat16), preferred_element_type=jnp.float32)
        hid_ref[s] = (h1 * jax.nn.sigmoid(h1) * h3).astype(jnp.bfloat16)

    @pl.when(s >= nf)
    def _():
        hid = jnp.concatenate([hid_ref[f] for f in range(nf)], axis=1) if nf > 1 else hid_ref[0]
        y = jnp.dot(hid, w2_ref[0].astype(jnp.bfloat16), preferred_element_type=jnp.float32)
        y = y * gc_ref[...].reshape(B * cap, 1)
        ye_ref[...] = y.astype(jnp.bfloat16).reshape(ye_ref.shape)


def ec_experts(xe, gc, w1, w3, w2, tf=256, tn=1024):
    B, E, cap, D = xe.shape
    F = w1.shape[2]
    tf = min(tf, F)
    tn = min(tn, D)
    nf, nn = F // tf, D // tn
    return pl.pallas_call(
        functools.partial(_experts_body, nf),
        grid=(E, nf + nn),
        in_specs=[pl.BlockSpec((B, 1, cap, D), lambda e, s: (0, e, 0, 0), pipeline_mode=pl.Buffered(1)),
                  pl.BlockSpec((B, 1, cap, 1), lambda e, s: (0, e, 0, 0)),
                  pl.BlockSpec((1, D, tf), lambda e, s: (e, 0, jnp.minimum(s, nf - 1))),
                  pl.BlockSpec((1, D, tf), lambda e, s: (e, 0, jnp.minimum(s, nf - 1))),
                  pl.BlockSpec((1, F, tn), lambda e, s: (e, 0, jnp.maximum(s - nf, 0)))],
        out_specs=pl.BlockSpec((B, 1, cap, tn), lambda e, s: (0, e, 0, jnp.maximum(s - nf, 0))),
        out_shape=jax.ShapeDtypeStruct((B, E, cap, D), jnp.bfloat16),
        scratch_shapes=[pltpu.VMEM((nf, B * cap, tf), jnp.bfloat16)],
        compiler_params=_cparams("arbitrary", "arbitrary"), name="ec_experts",
    )(xe, gc, w1, w3, w2)


def _combine_body(ctx_row, cap, pos_t_ref, ye_ref, x_ref, gate_ref, o_ref):
    row = pl.program_id(0) if ctx_row is None else ctx_row
    K = ye_ref.shape[1]
    assert cap & (cap - 1) == 0 and cap <= 256
    col_expert = lax.broadcasted_iota(jnp.int32, (LANES, K), 1) >> (cap.bit_length() - 1)
    spread = (col_expert == lax.broadcasted_iota(jnp.int32, (LANES, K), 0)).astype(jnp.bfloat16)
    pos = jnp.dot(pos_t_ref[0].astype(jnp.float32).astype(jnp.bfloat16), spread,
                  preferred_element_type=jnp.float32)
    col_slot = (lax.broadcasted_iota(jnp.int32, (1, K), 1) & (cap - 1)).astype(jnp.float32)
    hit = (pos == col_slot).astype(jnp.bfloat16)
    y = jnp.dot(hit, ye_ref[0], preferred_element_type=jnp.float32)
    o_ref[0] = x_ref[0] + _mod_row(gate_ref, row) * y


def ec_combine(x, pos_t, ye, mod, k_gate, cap, ctx_row=None, tt=512, tn=1024):
    B, L, D = x.shape
    tt = min(tt, L)
    tn = min(tn, D)
    K = N_EXPERTS * cap
    return pl.pallas_call(
        functools.partial(_combine_body, ctx_row, cap),
        grid=(B, D // tn, L // tt),
        in_specs=[pl.BlockSpec((1, tt, LANES), lambda b, n, t: (b, t, 0)),
                  pl.BlockSpec((1, K, tn), lambda b, n, t: (b, 0, n)),
                  pl.BlockSpec((1, tt, tn), lambda b, n, t: (b, t, n)),
                  pl.BlockSpec((MOD_ROWS, tn), lambda b, n, t: (0, k_gate * (D // tn) + n))],
        out_specs=pl.BlockSpec((1, tt, tn), lambda b, n, t: (b, t, n)),
        out_shape=jax.ShapeDtypeStruct((B, L, D), jnp.float32),
        compiler_params=_cparams("arbitrary", "arbitrary", "arbitrary"), name="ec_combine",
    )(pos_t, ye.reshape(B, K, D), x, mod)


def moe_block(x, gain, mod, router, w1, w3, w2, ctx_row=None):
    B, L, D = x.shape
    cap = EC_CAPACITY * L // N_EXPERTS
    h, aff = normmod(x, gain, mod, 3, 4, ctx_row=ctx_row, router=router)
    pos_e, gate_e, pos_t = ec_select(aff, cap)
    xe, gc = ec_gather(h, pos_e, gate_e, cap)
    ye = ec_experts(xe, gc, w1, w3, w2)
    return ec_combine(x, pos_t, ye, mod, 5, cap, ctx_row=ctx_row)


def kernel(x, c, ctx, c_ctx, ada_w, ada_b, norm_mix, norm_ffn, w_in, pool_w, pool_scale, conv_w, a_log_f, a_log_b, dt_bias_f, dt_bias_b, gdn_norm, w_out_ab, w_qkv, lam_q1, lam_k1, lam_q2, lam_k2, subln, w_out_c, router, w1, w3, w2, final_norm):
    depth = ada_w.shape[0]
    B, L, D = x.shape
    Lc = ctx.shape[1]
    assert B < MOD_ROWS
    ctx_row = B
    cos, sin = axial_rope_tables(L // GRID_W)
    cond = jnp.zeros((MOD_ROWS, D), jnp.float32).at[:B].set(jax.nn.silu(c)).at[ctx_row].set(jax.nn.silu(c_ctx))
    cond = cond.astype(jnp.bfloat16)
    x_lat, x_ctx = x, ctx
    for layer in range(depth):
        last = layer == depth - 1
        j = layer // 2
        mod = proj(cond, ada_w[layer]) + ada_b[layer]
        h_lat = normmod(x_lat, norm_mix[layer], mod, 0, 1)
        h_ctx = normmod(x_ctx, norm_mix[layer], mod, 0, 1, ctx_row=ctx_row)
        if layer % 2 == 0:
            y_ctx, y_lat = pool_gdn_mixer(h_ctx, h_lat, w_in[j], pool_w[j], pool_scale[j], conv_w[j],
                                          a_log_f[j], a_log_b[j], dt_bias_f[j], dt_bias_b[j], gdn_norm[j])
            w_out = w_out_ab[j]
        else:
            assert last, "context outputs of an attention layer are only skipped when no later layer reads them"
            lambda_init = 0.8 - 0.6 * math.exp(-0.3 * layer)
            y_ctx = None
            y_lat = diff_attn_mixer(h_ctx, h_lat, w_qkv[j], lam_q1[j], lam_k1[j], lam_q2[j], lam_k2[j],
                                    subln[j], lambda_init, cos, sin)
            w_out = w_out_c[j]
        x_lat = proj(y_lat.reshape(B * L, D), w_out, tm=1024,
                     residual=(x_lat.reshape(B * L, D), mod, 2, L, None)).reshape(B, L, D)
        x_lat = moe_block(x_lat, norm_ffn[layer], mod, router[layer], w1[layer], w3[layer], w2[layer])
        if not last:
            x_ctx = proj(y_ctx.reshape(B * Lc, D), w_out, tm=1024,
                         residual=(x_ctx.reshape(B * Lc, D), mod, 2, Lc, ctx_row)).reshape(B, Lc, D)
            x_ctx = moe_block(x_ctx, norm_ffn[layer], mod, router[layer], w1[layer], w3[layer], w2[layer],
                              ctx_row=ctx_row)
    return final_rmsnorm(x_lat, final_norm)
```

```python
import functools
import math

import jax
import jax.numpy as jnp
from jax import lax
from jax.experimental import pallas as pl
from jax.experimental.pallas import tpu as pltpu

D_MODEL = 4096
GRID_W = 64
NORM_EPS = 1e-6
POOL_WINDOWS = (2, 4, 8, 16)
POOL_GROUP = D_MODEL // 16
POOL_WIDTH = len(POOL_WINDOWS) * POOL_GROUP
GDN_HEAD_DIM = 128
GDN_HEADS = (D_MODEL - POOL_WIDTH) // GDN_HEAD_DIM
GDN_WIDTH = GDN_HEADS * GDN_HEAD_DIM
GDN_CONV = 5
GDN_CHUNK = 64
QKV_COLS = 3 * GDN_WIDTH
DIFF_HEAD_DIM = 128
DIFF_HEADS = D_MODEL // (2 * DIFF_HEAD_DIM)
ROPE_BASE = 10000.0
N_EXPERTS = 16
EC_CAPACITY = 2

LANES = 128
MOD_ROWS = 8
PREFIX_BLOCK = 256
VMEM_LIMIT_BYTES = 56 * 1024 * 1024

PROJ_TN = 512
GDN_PAIR = 2 * GDN_CHUNK
CONV_PAD = 8
ROW_BLOCK = 128
STATE_W = 4 * GDN_HEADS
REST_TILE0 = QKV_COLS // LANES
IN_COLS_PADDED = -(-(QKV_COLS + STATE_W + POOL_WIDTH + GDN_WIDTH) // PROJ_TN) * PROJ_TN
Z_TILE0 = REST_TILE0 + (STATE_W + POOL_WIDTH) // LANES
Z_SHIFT = LANES - (STATE_W + POOL_WIDTH) % LANES
POOL_PAD = 16
POOL_IN_W = 1152


def _cparams(*sem):
    return pltpu.CompilerParams(dimension_semantics=sem, vmem_limit_bytes=VMEM_LIMIT_BYTES)


def _mod_row(mod_ref, row):
    return mod_ref[pl.ds(row, 1), :]


def _proj_body(rows_per_sample, ctx_row, a_ref, w_ref, *rest):
    if rows_per_sample is None:
        o_ref, wb_ref = rest
    else:
        x_ref, gate_ref, o_ref, wb_ref = rest
    i = pl.program_id(1)

    @pl.when(i == 0)
    def _():
        wb_ref[...] = w_ref[...].astype(jnp.bfloat16)

    y = jnp.dot(a_ref[...], wb_ref[...], preferred_element_type=jnp.float32)
    if rows_per_sample is None:
        o_ref[...] = y.astype(o_ref.dtype)
    else:
        row = (i * a_ref.shape[0]) // rows_per_sample if ctx_row is None else ctx_row
        o_ref[...] = x_ref[...] + _mod_row(gate_ref, row) * y


def proj(a, w, layer, *, col0=0, ncols=None, tm=512, tn=PROJ_TN, out_dtype=jnp.float32, residual=None):
    M, K = a.shape
    ncols = w.shape[2] - col0 if ncols is None else ncols
    tm = min(tm, M)
    assert M % tm == 0 and ncols % tn == 0 and col0 % tn == 0, (M, ncols, col0, tm, tn)
    j0 = col0 // tn
    in_specs = [pl.BlockSpec((tm, K), lambda j, i: (i, 0)),
                pl.BlockSpec((None, K, tn), lambda j, i: (layer, 0, j0 + j))]
    args = [a, w]
    rows_per_sample = ctx_row = None
    if residual is not None:
        x, mod, k_gate, rows_per_sample, ctx_row = residual
        assert ctx_row is not None or rows_per_sample % tm == 0
        in_specs += [pl.BlockSpec((tm, tn), lambda j, i: (i, j)),
                     pl.BlockSpec((MOD_ROWS, tn), lambda j, i: (0, k_gate * (ncols // tn) + j))]
        args += [x, mod]
    return pl.pallas_call(
        functools.partial(_proj_body, rows_per_sample, ctx_row),
        grid=(ncols // tn, M // tm),
        in_specs=in_specs,
        out_specs=pl.BlockSpec((tm, tn), lambda j, i: (i, j)),
        out_shape=jax.ShapeDtypeStruct((M, ncols), out_dtype),
        scratch_shapes=[pltpu.VMEM((K, tn), jnp.bfloat16)],
        compiler_params=_cparams("arbitrary", "arbitrary"), name="proj",
    )(*args)


def _normmod_body(ctx_row, with_router, x_ref, g_ref, shift_ref, scale_ref, *rest):
    row = pl.program_id(0) if ctx_row is None else ctx_row
    x = x_ref[0]
    y = x * lax.rsqrt(jnp.mean(x * x, axis=-1, keepdims=True) + NORM_EPS) * g_ref[...]
    h = (y * (1.0 + _mod_row(scale_ref, row)) + _mod_row(shift_ref, row)).astype(jnp.bfloat16)
    if not with_router:
        (h_ref,) = rest
        h_ref[0] = h
        return
    r_ref, h_ref, aff_ref = rest
    h_ref[0] = h
    logits = jnp.dot(h, r_ref[...], preferred_element_type=jnp.float32)
    lane = lax.broadcasted_iota(jnp.int32, logits.shape, 1)
    logits = jnp.where(lane < N_EXPERTS, logits, -jnp.inf)
    e = jnp.exp(logits - jnp.max(logits, axis=-1, keepdims=True))
    aff_ref[0] = e / jnp.sum(e, axis=-1, keepdims=True)


def normmod(x, gain, mod, k_shift, k_scale, ctx_row=None, router=None, tr=256):
    B, L, D = x.shape
    tr = min(tr, L)
    in_specs = [pl.BlockSpec((1, tr, D), lambda b, t: (b, t, 0)),
                pl.BlockSpec((1, D), lambda b, t: (0, 0)),
                pl.BlockSpec((MOD_ROWS, D), lambda b, t: (0, k_shift)),
                pl.BlockSpec((MOD_ROWS, D), lambda b, t: (0, k_scale))]
    args = [x, gain.reshape(1, D), mod, mod]
    out_specs = [pl.BlockSpec((1, tr, D), lambda b, t: (b, t, 0))]
    out_shape = [jax.ShapeDtypeStruct((B, L, D), jnp.bfloat16)]
    if router is not None:
        rp = jnp.pad(router.astype(jnp.bfloat16), ((0, 0), (0, LANES - router.shape[1])))
        in_specs.append(pl.BlockSpec((D, LANES), lambda b, t: (0, 0)))
        args.append(rp)
        out_specs.append(pl.BlockSpec((1, tr, LANES), lambda b, t: (b, t, 0)))
        out_shape.append(jax.ShapeDtypeStruct((B, L, LANES), jnp.float32))
    out = pl.pallas_call(
        functools.partial(_normmod_body, ctx_row, router is not None),
        grid=(B, L // tr), in_specs=in_specs, out_specs=out_specs, out_shape=out_shape,
        compiler_params=_cparams("arbitrary", "arbitrary"), name="normmod",
    )(*args)
    return out if router is not None else out[0]


def _final_norm_body(x_ref, g_ref, o_ref):
    x = x_ref[...]
    o_ref[...] = x * lax.rsqrt(jnp.mean(x * x, axis=-1, keepdims=True) + NORM_EPS) * g_ref[...]


def final_rmsnorm(x, gain, tr=512):
    B, L, D = x.shape
    M = B * L
    tr = min(tr, M)
    return pl.pallas_call(
        _final_norm_body,
        grid=(M // tr,),
        in_specs=[pl.BlockSpec((tr, D), lambda i: (i, 0)), pl.BlockSpec((1, D), lambda i: (0, 0))],
        out_specs=pl.BlockSpec((tr, D), lambda i: (i, 0)),
        out_shape=jax.ShapeDtypeStruct((M, D), x.dtype),
        compiler_params=_cparams("arbitrary"), name="final_rmsnorm",
    )(x.reshape(M, D), gain.reshape(1, D)).reshape(B, L, D)


def _gates_body(st_ref, alog_ref, dtb_ref, rows_ref, cols_ref):
    T = st_ref.shape[1]
    H = GDN_HEADS
    x = st_ref[0].T
    t = x + dtb_ref[...]
    softplus = jnp.maximum(t, 0.0) + jnp.log1p(jnp.exp(-jnp.abs(t)))
    r_full = lax.broadcasted_iota(jnp.int32, x.shape, 0)
    val = jnp.where(r_full < 2 * H, -jnp.exp(alog_ref[...]) * softplus, jax.nn.sigmoid(x))
    pos = lax.broadcasted_iota(jnp.int32, (LANES, LANES), 1) & (GDN_CHUNK - 1)
    r = lax.broadcasted_iota(jnp.int32, (LANES, LANES), 0)
    tiles = []
    for c in range(T // LANES):
        v = val[:, c * LANES:(c + 1) * LANES]
        pre, suf, s = v, v, 1
        while s < GDN_CHUNK:
            pre = pre + jnp.where(pos >= s, pltpu.roll(pre, s, 1), 0.0)
            suf = suf + jnp.where(pos < GDN_CHUNK - s, pltpu.roll(suf, LANES - s, 1), 0.0)
            s *= 2
        tiles.append(jnp.where(r < H, pre, jnp.where(r < 2 * H, suf, v)))
    out = jnp.concatenate(tiles, axis=1) if len(tiles) > 1 else tiles[0]
    rows_ref[0] = out
    cols_ref[0] = out.T


def gdn_gates(p, a_log_f, a_log_b, dt_bias_f, dt_bias_b):
    B, T, _ = p.shape
    pad = jnp.zeros((LANES - 2 * GDN_HEADS,), jnp.float32)
    alog = jnp.concatenate([a_log_f, a_log_b, pad]).astype(jnp.float32).reshape(LANES, 1)
    dtb = jnp.concatenate([dt_bias_f, dt_bias_b, pad]).astype(jnp.float32).reshape(LANES, 1)
    return pl.pallas_call(
        _gates_body,
        grid=(B,),
        in_specs=[pl.BlockSpec((1, T, LANES), lambda b: (b, 0, REST_TILE0)),
                  pl.BlockSpec((LANES, 1), lambda b: (0, 0)),
                  pl.BlockSpec((LANES, 1), lambda b: (0, 0))],
        out_specs=[pl.BlockSpec((1, LANES, T), lambda b: (b, 0, 0)),
                   pl.BlockSpec((1, T, LANES), lambda b: (b, 0, 0))],
        out_shape=[jax.ShapeDtypeStruct((B, LANES, T), jnp.float32),
                   jax.ShapeDtypeStruct((B, T, LANES), jnp.float32)],
        compiler_params=_cparams("arbitrary"), name="gdn_gates",
    )(p, alog, dtb)


def _conv_silu(x_ref, w_ref, pad_sc, dst_sc, T, unit_norm):
    half = GDN_CONV // 2
    zeros = jnp.zeros((CONV_PAD, LANES), jnp.float32)
    pad_sc[0:CONV_PAD, :] = zeros
    pad_sc[CONV_PAD:CONV_PAD + T, :] = x_ref[0]
    pad_sc[CONV_PAD + T:2 * CONV_PAD + T, :] = zeros
    w = w_ref[...]
    for r0 in range(0, T, ROW_BLOCK):
        y = jnp.zeros((ROW_BLOCK, LANES), jnp.float32)
        for j in range(GDN_CONV):
            s = CONV_PAD + r0 + j - half
            y = y + w[j:j + 1, :] * pad_sc[s:s + ROW_BLOCK, :]
        y = y * jax.nn.sigmoid(y)
        if unit_norm:
            y = y * lax.rsqrt(jnp.sum(y * y, axis=-1, keepdims=True) + NORM_EPS)
        dst_sc[r0:r0 + ROW_BLOCK, :] = y


def _nt(a, b):
    return lax.dot_general(a, b, (((1,), (1,)), ((), ())), preferred_element_type=jnp.float32)


def _gdn_segment(T, h, refs, sc, s_init):
    pq_ref, pk_ref, pv_ref, cols_ref, rows_ref, za_ref, zb_ref, y_ref = refs
    (cwq_ref, cwk_ref, cwv_ref, gn_ref, pad_sc, q_sc, k_sc, v_sc, u_sc, w_sc, qd_sc, kdt_sc, attn_sc, gend_sc,
     o_sc, grow_sc) = sc
    C, P, H = GDN_CHUNK, GDN_PAIR, GDN_HEADS
    n_pairs = T // P
    bf16 = jnp.bfloat16
    scale = GDN_HEAD_DIM ** -0.5
    for d in range(2):
        grow_sc[d, 0:1, 0:T] = rows_ref[0, pl.ds(d * H + h, 1), :]
    _conv_silu(pq_ref, cwq_ref, pad_sc, q_sc, T, True)
    _conv_silu(pk_ref, cwk_ref, pad_sc, k_sc, T, True)
    _conv_silu(pv_ref, cwv_ref, pad_sc, v_sc, T, False)

    lane = lax.broadcasted_iota(jnp.int32, (P, LANES), 1)
    ii = lax.broadcasted_iota(jnp.int32, (C, C), 0)
    jj = lax.broadcasted_iota(jnp.int32, (C, C), 1)

    def prep(m, carry):
        r0 = pl.multiple_of(m * P, P)
        q2 = q_sc[pl.ds(r0, P), :]
        k2 = k_sc[pl.ds(r0, P), :]
        v2 = v_sc[pl.ds(r0, P), :]
        cols = cols_ref[0, pl.ds(r0, P), :]
        qs = q2 * scale
        halves = [slice(c * C, (c + 1) * C) for c in range(2)]
        kh = [k2[sl].astype(bf16) for sl in halves]
        qk = [_nt(qs[sl].astype(bf16), kh[c]) for c, sl in enumerate(halves)]
        neg_a, rhs, dest = [], [], []
        for d in range(2):
            gcol = jnp.sum(jnp.where(lane == d * H + h, cols, 0.0), axis=1, keepdims=True)
            bcol = jnp.sum(jnp.where(lane == (2 + d) * H + h, cols, 0.0), axis=1, keepdims=True)
            grow = grow_sc[d, 0:1, pl.ds(r0, P)]
            keep = (ii >= jj) if d == 0 else (ii <= jj)
            strict = (ii > jj) if d == 0 else (ii < jj)
            eg = jnp.exp(gcol)
            kb = k2 * bcol
            rhs_uw = jnp.concatenate([v2 * bcol, kb * eg], axis=1)
            qd_sc[d, pl.ds(r0, P), :] = (qs * eg).astype(bf16)
            kk = [_nt(kb[sl].astype(bf16), kh[c]) for c, sl in enumerate(halves)]
            kd = []
            for c, sl in enumerate(halves):
                rc = pl.multiple_of(r0 + c * C, C)
                gc = gcol[sl]
                decay = jnp.exp(jnp.where(keep, gc - grow[:, sl], -jnp.inf))
                neg_a.append(jnp.where(strict, -(kk[c] * decay), 0.0))
                rhs.append(rhs_uw[sl])
                dest.append((d, rc))
                attn_sc[d, pl.ds(rc, C), :] = (qk[c] * decay).astype(bf16)
                g_last = gc[C - 1:C] if d == 0 else gc[0:1]
                kd.append(k2[sl] * jnp.exp(g_last - gc))
                gend_sc[d, pl.ds(pl.multiple_of((2 * m + c) * 8, 8), 8), :] = jnp.broadcast_to(
                    jnp.exp(g_last), (8, LANES))
            kdt_sc[d, :, pl.ds(r0, P)] = jnp.concatenate(kd, axis=0).T.astype(bf16)
        e, p = list(neg_a), list(neg_a)
        for _ in range(C.bit_length() - 2):
            pb = [x.astype(bf16) for x in p]
            p = [jnp.dot(x, x, preferred_element_type=jnp.float32) for x in pb]
            ep = [jnp.dot(a.astype(bf16), b.astype(bf16), preferred_element_type=jnp.float32)
                  for a, b in zip(e, p)]
            e = [a + b + ab for a, b, ab in zip(e, p, ep)]
        sol = [r + jnp.dot(a.astype(bf16), r.astype(bf16), preferred_element_type=jnp.float32)
               for a, r in zip(e, rhs)]
        for (d, rc), s in zip(dest, sol):
            u_sc[d, pl.ds(rc, C), :] = s[:, :GDN_HEAD_DIM]
            w_sc[d, pl.ds(rc, C), :] = s[:, GDN_HEAD_DIM:].astype(bf16)
        return carry

    lax.fori_loop(0, n_pairs, prep, 0)

    def chunk_step(d, s, rc, kdt, n):
        sb = s.astype(bf16)
        v_new = u_sc[d, pl.ds(rc, C), :] - jnp.dot(w_sc[d, pl.ds(rc, C), :], sb, preferred_element_type=jnp.float32)
        vb = v_new.astype(bf16)
        o_sc[d, pl.ds(rc, C), :] = (jnp.dot(qd_sc[d, pl.ds(rc, C), :], sb, preferred_element_type=jnp.float32)
                                    + jnp.dot(attn_sc[d, pl.ds(rc, C), :], vb, preferred_element_type=jnp.float32))
        g_end = gend_sc[d, pl.ds(pl.multiple_of(n * 8, 8), 8), :][0:1]
        return s * g_end + jnp.dot(kdt, vb, preferred_element_type=jnp.float32)

    def scan(m, carry):
        s_f, s_b = carry
        rf = pl.multiple_of(m * P, P)
        mb = n_pairs - 1 - m
        rb = pl.multiple_of(mb * P, P)
        kdt_f = kdt_sc[0, :, pl.ds(rf, P)]
        kdt_b = kdt_sc[1, :, pl.ds(rb, P)]
        for c in range(2):
            s_f = chunk_step(0, s_f, pl.multiple_of(rf + c * C, C), kdt_f[:, c * C:(c + 1) * C], 2 * m + c)
            cb = 1 - c
            s_b = chunk_step(1, s_b, pl.multiple_of(rb + cb * C, C), kdt_b[:, cb * C:(cb + 1) * C], 2 * mb + cb)
        return s_f, s_b

    s_f, s_b = lax.fori_loop(0, n_pairs, scan, s_init)

    lane_b = lax.broadcasted_iota(jnp.int32, (ROW_BLOCK, LANES), 1)
    for r0 in range(0, T, ROW_BLOCK):
        o = o_sc[0, r0:r0 + ROW_BLOCK, :] + o_sc[1, r0:r0 + ROW_BLOCK, :]
        z = pltpu.roll(jnp.where(lane_b >= LANES - Z_SHIFT, za_ref[0, r0:r0 + ROW_BLOCK, :],
                                 zb_ref[0, r0:r0 + ROW_BLOCK, :]), Z_SHIFT, 1)
        y = o * lax.rsqrt(jnp.mean(o * o, axis=-1, keepdims=True) + NORM_EPS) * gn_ref[...]
        y_ref[0, r0:r0 + ROW_BLOCK, :] = (y * (z * jax.nn.sigmoid(z))).astype(y_ref.dtype)
    return s_f, s_b


def _gdn_body(Tc, Tl, *refs):
    h = pl.program_id(1)
    ctx_refs = refs[0:7]
    lat_refs = refs[7:14]
    shared = refs[14:18]
    y_ctx_ref, y_lat_ref = refs[18:20]
    sc = shared + refs[20:]
    zero = jnp.zeros((GDN_HEAD_DIM, GDN_HEAD_DIM), jnp.float32)
    states = _gdn_segment(Tc, h, ctx_refs + (y_ctx_ref,), sc, (zero, zero))
    _gdn_segment(Tl, h, lat_refs + (y_lat_ref,), sc, states)


def gdn_mix(p_ctx, gates_ctx, p_lat, gates_lat, conv_w, gdn_norm):
    B, Tc, _ = p_ctx.shape
    Tl = p_lat.shape[1]
    H, dh = GDN_HEADS, GDN_HEAD_DIM

    def seg_specs(T):
        return [pl.BlockSpec((1, T, dh), lambda b, h: (b, 0, h)),
                pl.BlockSpec((1, T, dh), lambda b, h: (b, 0, H + h)),
                pl.BlockSpec((1, T, dh), lambda b, h: (b, 0, 2 * H + h)),
                pl.BlockSpec((1, T, LANES), lambda b, h: (b, 0, 0)),
                pl.BlockSpec((1, LANES, T), lambda b, h: (b, 0, 0)),
                pl.BlockSpec((1, T, LANES), lambda b, h: (b, 0, Z_TILE0 + h)),
                pl.BlockSpec((1, T, LANES), lambda b, h: (b, 0, Z_TILE0 + 1 + h))]

    shared_specs = [pl.BlockSpec((GDN_CONV, dh), lambda b, h: (0, h)),
                    pl.BlockSpec((GDN_CONV, dh), lambda b, h: (0, H + h)),
                    pl.BlockSpec((GDN_CONV, dh), lambda b, h: (0, 2 * H + h)),
                    pl.BlockSpec((1, dh), lambda b, h: (0, 0))]
    rows_c, cols_c = gates_ctx
    rows_l, cols_l = gates_lat
    f32, bf16 = jnp.float32, jnp.bfloat16
    scratch = [pltpu.VMEM((Tl + 2 * CONV_PAD, LANES), f32),
               pltpu.VMEM((Tl, dh), f32), pltpu.VMEM((Tl, dh), f32), pltpu.VMEM((Tl, dh), f32),
               pltpu.VMEM((2, Tl, dh), f32), pltpu.VMEM((2, Tl, dh), bf16), pltpu.VMEM((2, Tl, dh), bf16),
               pltpu.VMEM((2, dh, Tl), bf16), pltpu.VMEM((2, Tl, GDN_CHUNK), bf16),
               pltpu.VMEM((2, Tl // GDN_CHUNK * 8, LANES), f32), pltpu.VMEM((2, Tl, dh), f32),
               pltpu.VMEM((2, 8, Tl), f32)]
    return pl.pallas_call(
        functools.partial(_gdn_body, Tc, Tl),
        grid=(B, H),
        in_specs=seg_specs(Tc) + seg_specs(Tl) + shared_specs,
        out_specs=[pl.BlockSpec((1, Tc, dh), lambda b, h: (b, 0, h)),
                   pl.BlockSpec((1, Tl, dh), lambda b, h: (b, 0, h))],
        out_shape=[jax.ShapeDtypeStruct((B, Tc, GDN_WIDTH), bf16), jax.ShapeDtypeStruct((B, Tl, GDN_WIDTH), bf16)],
        scratch_shapes=scratch,
        compiler_params=_cparams("arbitrary", "arbitrary"), name="gdn_mix",
    )(p_ctx, p_ctx, p_ctx, cols_c, rows_c, p_ctx, p_ctx,
      p_lat, p_lat, p_lat, cols_l, rows_l, p_lat, p_lat,
      conv_w, conv_w, conv_w, gdn_norm.reshape(1, dh))


def _pool_body(x_ref, w_ref, sc_ref, y_ref, pad_sc):
    T = x_ref.shape[1]
    G = POOL_GROUP
    zeros = jnp.zeros((POOL_PAD, G), jnp.float32)
    t = lax.broadcasted_iota(jnp.int32, (ROW_BLOCK, 1), 0)
    for i, win in enumerate(POOL_WINDOWS):
        half = win // 2
        lo = (STATE_W + i * G) // LANES * LANES
        off = STATE_W + i * G - lo
        pad_sc[0:POOL_PAD, :] = zeros
        pad_sc[POOL_PAD:POOL_PAD + T, :] = x_ref[0, :, lo:lo + G + LANES][:, off:off + G]
        pad_sc[POOL_PAD + T:2 * POOL_PAD + T, :] = zeros
        wb = w_ref[i].astype(jnp.bfloat16)
        for r0 in range(0, T, ROW_BLOCK):
            tot = jnp.zeros((ROW_BLOCK, G), jnp.float32)
            for s in range(-half, half):
                tot = tot + pad_sc[POOL_PAD + r0 + s:POOL_PAD + r0 + s + ROW_BLOCK, :]
            cnt = (jnp.minimum(t + r0 + half, T) - jnp.maximum(t + r0 - half, 0)).astype(jnp.float32)
            centred = tot / cnt - pad_sc[POOL_PAD + r0:POOL_PAD + r0 + ROW_BLOCK, :]
            y = jnp.dot(centred.astype(jnp.bfloat16), wb, preferred_element_type=jnp.float32)
            y_ref[0, r0:r0 + ROW_BLOCK, i * G:(i + 1) * G] = (y * sc_ref[:, i * G:(i + 1) * G]).astype(y_ref.dtype)


def pool_mix(p, pool_w, pool_scale):
    B, T, _ = p.shape
    n = len(POOL_WINDOWS)
    assert QKV_COLS % POOL_IN_W == 0 and STATE_W + POOL_WIDTH <= POOL_IN_W
    return pl.pallas_call(
        _pool_body,
        grid=(B,),
        in_specs=[pl.BlockSpec((1, T, POOL_IN_W), lambda b: (b, 0, QKV_COLS // POOL_IN_W)),
                  pl.BlockSpec((n, POOL_GROUP, POOL_GROUP), lambda b: (0, 0, 0)),
                  pl.BlockSpec((1, POOL_WIDTH), lambda b: (0, 0))],
        out_specs=pl.BlockSpec((1, T, POOL_WIDTH), lambda b: (b, 0, 0)),
        out_shape=jax.ShapeDtypeStruct((B, T, POOL_WIDTH), jnp.bfloat16),
        scratch_shapes=[pltpu.VMEM((T + 2 * POOL_PAD, POOL_GROUP), jnp.float32)],
        compiler_params=_cparams("arbitrary"), name="pool_mix",
    )(p, pool_w, pool_scale.reshape(1, POOL_WIDTH))


def pool_gdn_mixer(h_ctx, h_lat, w_in, j, pool_w, pool_scale, conv_w, a_log_f, a_log_b, dt_bias_f, dt_bias_b,
                   gdn_norm):
    B, Tl, D = h_lat.shape
    Tc = h_ctx.shape[1]
    p_lat = proj(h_lat.reshape(B * Tl, D), w_in, j, ncols=IN_COLS_PADDED, tm=1024).reshape(B, Tl, IN_COLS_PADDED)
    p_ctx = proj(h_ctx.reshape(B * Tc, D), w_in, j, ncols=IN_COLS_PADDED, tm=1024).reshape(B, Tc, IN_COLS_PADDED)
    gates_ctx = gdn_gates(p_ctx, a_log_f, a_log_b, dt_bias_f, dt_bias_b)
    gates_lat = gdn_gates(p_lat, a_log_f, a_log_b, dt_bias_f, dt_bias_b)
    g_ctx, g_lat = gdn_mix(p_ctx, gates_ctx, p_lat, gates_lat, conv_w, gdn_norm)
    y_ctx = jnp.concatenate([pool_mix(p_ctx, pool_w, pool_scale), g_ctx], axis=-1)
    y_lat = jnp.concatenate([pool_mix(p_lat, pool_w, pool_scale), g_lat], axis=-1)
    return y_ctx, y_lat


def axial_rope_tables(rows):
    pos_row = jnp.repeat(jnp.arange(rows), GRID_W).astype(jnp.float32)
    pos_col = jnp.tile(jnp.arange(GRID_W), rows).astype(jnp.float32)
    quarter = DIFF_HEAD_DIM // 4
    inv_freq = ROPE_BASE ** (-jnp.arange(quarter, dtype=jnp.float32) / quarter)
    ang_r = pos_row[:, None] * inv_freq[None, :]
    ang_c = pos_col[:, None] * inv_freq[None, :]
    ang = jnp.concatenate([ang_r, ang_r, ang_c, ang_c], axis=-1)
    return jnp.cos(ang), jnp.sin(ang)


def _rope(x, cos, sin_signed):
    d = x.shape[-1]
    first = (lax.broadcasted_iota(jnp.int32, x.shape, 1) & (d // 2 - 1)) < d // 4
    partner = jnp.where(first, pltpu.roll(x, d - d // 4, 1), pltpu.roll(x, d // 4, 1))
    return x * cos + partner * sin_signed


def _diff_attn_body(lambda_init, tq, q_ref, kl_ref, vl_ref, kc_ref, vc_ref, cos_ref, sin_ref, lam_ref, g_ref,
                    o_ref, k_sc, v_sc):
    d = DIFF_HEAD_DIM
    L = kl_ref.shape[1]
    scale = d ** -0.5
    lv = lam_ref[...]
    lam = (jnp.exp(jnp.sum(lv[0:1] * lv[1:2], axis=1, keepdims=True))
           - jnp.exp(jnp.sum(lv[2:3] * lv[3:4], axis=1, keepdims=True)) + lambda_init)
    cos = cos_ref[...]
    sin = sin_ref[...]
    for m in range(2):
        k_sc[0:L, m * d:(m + 1) * d] = _rope(kl_ref[0, :, m * d:(m + 1) * d], cos, sin).astype(jnp.bfloat16)
    k_sc[L:, :] = kc_ref[0].astype(jnp.bfloat16)
    v_sc[0:L, :] = vl_ref[0].astype(jnp.bfloat16)
    v_sc[L:, :] = vc_ref[0].astype(jnp.bfloat16)

    def scores(r0, m):
        q = _rope(q_ref[0, pl.ds(r0, tq), m * d:(m + 1) * d], cos_ref[pl.ds(r0, tq), :],
                  sin_ref[pl.ds(r0, tq), :]).astype(jnp.bfloat16)
        return lax.dot_general(q, k_sc[:, m * d:(m + 1) * d], (((1,), (1,)), ((), ())),
                               preferred_element_type=jnp.float32)

    def finish(r0, s_pair):
        probs = []
        for s in s_pair:
            s = s * scale
            e = jnp.exp(s - jnp.max(s, axis=-1, keepdims=True))
            probs.append(e * (1.0 / jnp.sum(e, axis=-1, keepdims=True)))
        a = (probs[0] - lam * probs[1]).astype(jnp.bfloat16)
        o = jnp.dot(a, v_sc[...], preferred_element_type=jnp.float32)
        o = o * lax.rsqrt(jnp.mean(o * o, axis=-1, keepdims=True) + NORM_EPS) * g_ref[...]
        o_ref[0, pl.ds(r0, tq), :] = (o * (1 - lambda_init)).astype(o_ref.dtype)

    def q_tile_pair(t, carry):
        ra = pl.multiple_of(2 * t * tq, tq)
        rb = pl.multiple_of(ra + tq, tq)
        sa = [scores(ra, m) for m in range(2)]
        sb = [scores(rb, m) for m in range(2)]
        finish(ra, sa)
        finish(rb, sb)
        return carry

    lax.fori_loop(0, L // (2 * tq), q_tile_pair, 0)


def diff_attention_lat(qkv_lat, kv_ctx, cos, sin_signed, lam_vecs, subln, lambda_init, tq=128):
    B, L, _ = qkv_lat.shape
    Lc = kv_ctx.shape[1]
    H, hd = DIFF_HEADS, 2 * DIFF_HEAD_DIM
    tq = min(tq, L // 2)
    assert L % (2 * tq) == 0
    return pl.pallas_call(
        functools.partial(_diff_attn_body, lambda_init, tq),
        grid=(B, H),
        in_specs=[pl.BlockSpec((1, L, hd), lambda b, h: (b, 0, h)),
                  pl.BlockSpec((1, L, hd), lambda b, h: (b, 0, H + h)),
                  pl.BlockSpec((1, L, hd), lambda b, h: (b, 0, 2 * H + h)),
                  pl.BlockSpec((1, Lc, hd), lambda b, h: (b, 0, h)),
                  pl.BlockSpec((1, Lc, hd), lambda b, h: (b, 0, H + h)),
                  pl.BlockSpec((L, DIFF_HEAD_DIM), lambda b, h: (0, 0)),
                  pl.BlockSpec((L, DIFF_HEAD_DIM), lambda b, h: (0, 0)),
                  pl.BlockSpec((4, DIFF_HEAD_DIM), lambda b, h: (0, 0)),
                  pl.BlockSpec((1, hd), lambda b, h: (0, 0))],
        out_specs=pl.BlockSpec((1, L, hd), lambda b, h: (b, 0, h)),
        out_shape=jax.ShapeDtypeStruct((B, L, H * hd), jnp.bfloat16),
        scratch_shapes=[pltpu.VMEM((L + Lc, hd), jnp.bfloat16), pltpu.VMEM((L + Lc, hd), jnp.bfloat16)],
        compiler_params=_cparams("arbitrary", "arbitrary"), name="diff_attention",
    )(qkv_lat, qkv_lat, qkv_lat, kv_ctx, kv_ctx, cos, sin_signed, lam_vecs, subln.reshape(1, hd))


def diff_attn_mixer(h_ctx, h_lat, w_qkv, j, lam_q1, lam_k1, lam_q2, lam_k2, subln, lambda_init, cos, sin):
    B, L, D = h_lat.shape
    Lc = h_ctx.shape[1]
    qkv = proj(h_lat.reshape(B * L, D), w_qkv, j, tm=1024).reshape(B, L, 3 * D)
    kv_ctx = proj(h_ctx.reshape(B * Lc, D), w_qkv, j, col0=D, tm=1024).reshape(B, Lc, 2 * D)
    quarter = DIFF_HEAD_DIM // 4
    first = (jnp.arange(DIFF_HEAD_DIM) % (2 * quarter)) < quarter
    sin_signed = jnp.where(first[None, :], -sin, sin)
    lam_vecs = jnp.stack([lam_q1, lam_k1, lam_q2, lam_k2]).astype(jnp.float32)
    return diff_attention_lat(qkv, kv_ctx, cos, sin_signed, lam_vecs, subln, lambda_init)


def _count_before(mask):
    R, L = mask.shape
    W = PREFIX_BLOCK
    m = mask.astype(jnp.float32)
    before = (lax.broadcasted_iota(jnp.int32, (W, W), 0) < lax.broadcasted_iota(jnp.int32, (W, W), 1)
              ).astype(jnp.bfloat16)
    carry = jnp.zeros((R, 1), jnp.float32)
    out = []
    for c in range(L // W):
        blk = m[:, c * W:(c + 1) * W]
        out.append(jnp.dot(blk.astype(jnp.bfloat16), before, preferred_element_type=jnp.float32) + carry)
        carry = carry + jnp.sum(blk, axis=1, keepdims=True)
    return out[0] if len(out) == 1 else jnp.concatenate(out, axis=1)


def _select_body(cap, aff_ref, pos_e_ref, gate_e_ref, pos_t_ref):
    L = aff_ref.shape[1]
    a = aff_ref[0].T[:N_EXPERTS]
    bits = pltpu.bitcast(a, jnp.int32)
    thr = jnp.zeros((N_EXPERTS, 1), jnp.int32)
    for bit in range(30, -1, -1):
        cand = thr | (1 << bit)
        cnt = jnp.sum((bits >= cand).astype(jnp.float32), axis=1, keepdims=True)
        thr = jnp.where(cnt >= cap, cand, thr)
    gt = bits > thr
    eq = bits == thr
    need = cap - jnp.sum(gt.astype(jnp.float32), axis=1, keepdims=True)
    sel = gt | (eq & (_count_before(eq) < need))
    pos = jnp.where(sel, _count_before(sel).astype(jnp.int32), -1)
    pos_e_ref[0] = pos
    gate_e_ref[0] = jnp.where(sel, a, 0.0)
    pad = jnp.full((LANES - N_EXPERTS, L), -1, jnp.int32)
    pos_t_ref[0] = jnp.concatenate([pos, pad], axis=0).T


def ec_select(aff, cap):
    B, L, _ = aff.shape
    return pl.pallas_call(
        functools.partial(_select_body, cap),
        grid=(B,),
        in_specs=[pl.BlockSpec((1, L, LANES), lambda b: (b, 0, 0))],
        out_specs=[pl.BlockSpec((1, N_EXPERTS, L), lambda b: (b, 0, 0)),
                   pl.BlockSpec((1, N_EXPERTS, L), lambda b: (b, 0, 0)),
                   pl.BlockSpec((1, L, LANES), lambda b: (b, 0, 0))],
        out_shape=[jax.ShapeDtypeStruct((B, N_EXPERTS, L), jnp.int32),
                   jax.ShapeDtypeStruct((B, N_EXPERTS, L), jnp.float32),
                   jax.ShapeDtypeStruct((B, L, LANES), jnp.int32)],
        compiler_params=_cparams("arbitrary"), name="ec_select",
    )(aff)


def _gather_body(h_ref, pos_e_ref, gate_e_ref, xe_ref, gc_ref):
    e = pl.program_id(1)
    cap = xe_ref.shape[2]
    L = h_ref.shape[1]
    pos_row = pos_e_ref[0, pl.ds(e, 1), :]
    gate_row = gate_e_ref[0, pl.ds(e, 1), :]
    hit = lax.broadcasted_iota(jnp.int32, (cap, L), 0) == pos_row
    xe_ref[0, 0] = jnp.dot(hit.astype(jnp.bfloat16), h_ref[0],
                           preferred_element_type=jnp.float32).astype(jnp.bfloat16)
    gc_ref[0, 0] = jnp.sum(jnp.where(hit, gate_row, 0.0), axis=1, keepdims=True)


def ec_gather(h, pos_e, gate_e, cap):
    B, L, D = h.shape
    return pl.pallas_call(
        _gather_body,
        grid=(B, N_EXPERTS),
        in_specs=[pl.BlockSpec((1, L, D), lambda b, e: (b, 0, 0)),
                  pl.BlockSpec((1, N_EXPERTS, L), lambda b, e: (b, 0, 0)),
                  pl.BlockSpec((1, N_EXPERTS, L), lambda b, e: (b, 0, 0))],
        out_specs=[pl.BlockSpec((1, 1, cap, D), lambda b, e: (b, e, 0, 0)),
                   pl.BlockSpec((1, 1, cap, 1), lambda b, e: (b, e, 0, 0))],
        out_shape=[jax.ShapeDtypeStruct((B, N_EXPERTS, cap, D), jnp.bfloat16),
                   jax.ShapeDtypeStruct((B, N_EXPERTS, cap, 1), jnp.float32)],
        compiler_params=_cparams("arbitrary", "arbitrary"), name="ec_gather",
    )(h, pos_e, gate_e)


def _experts_body(nf, xe_ref, gc_ref, w1_ref, w3_ref, w2_ref, ye_ref, hid_ref):
    s = pl.program_id(1)
    B, _, cap, D = xe_ref.shape

    @pl.when(s < nf)
    def _():
        x = xe_ref[...].reshape(B * cap, D)
        h1 = jnp.dot(x, w1_ref[0].astype(jnp.bfloat16), preferred_element_type=jnp.float32)
        h3 = jnp.dot(x, w3_ref[0].astype(jnp.bfloat16), preferred_element_type=jnp.float32)
        hid_ref[s] = (h1 * jax.nn.sigmoid(h1) * h3).astype(jnp.bfloat16)

    @pl.when(s >= nf)
    def _():
        hid = jnp.concatenate([hid_ref[f] for f in range(nf)], axis=1) if nf > 1 else hid_ref[0]
        y = jnp.dot(hid, w2_ref[0].astype(jnp.bfloat16), preferred_element_type=jnp.float32)
        y = y * gc_ref[...].reshape(B * cap, 1)
        ye_ref[...] = y.astype(jnp.bfloat16).reshape(ye_ref.shape)


def ec_experts(xe, gc, w1, w3, w2, layer, tf=256, tn=1024):
    B, E, cap, D = xe.shape
    F = w1.shape[3]
    tf = min(tf, F)
    tn = min(tn, D)
    nf, nn = F // tf, D // tn
    return pl.pallas_call(
        functools.partial(_experts_body, nf),
        grid=(E, nf + nn),
        in_specs=[pl.BlockSpec((B, 1, cap, D), lambda e, s: (0, e, 0, 0), pipeline_mode=pl.Buffered(1)),
                  pl.BlockSpec((B, 1, cap, 1), lambda e, s: (0, e, 0, 0)),
                  pl.BlockSpec((None, 1, D, tf), lambda e, s: (layer, e, 0, jnp.minimum(s, nf - 1))),
                  pl.BlockSpec((None, 1, D, tf), lambda e, s: (layer, e, 0, jnp.minimum(s, nf - 1))),
                  pl.BlockSpec((None, 1, F, tn), lambda e, s: (layer, e, 0, jnp.maximum(s - nf, 0)))],
        out_specs=pl.BlockSpec((B, 1, cap, tn), lambda e, s: (0, e, 0, jnp.maximum(s - nf, 0))),
        out_shape=jax.ShapeDtypeStruct((B, E, cap, D), jnp.bfloat16),
        scratch_shapes=[pltpu.VMEM((nf, B * cap, tf), jnp.bfloat16)],
        compiler_params=_cparams("arbitrary", "arbitrary"), name="ec_experts",
    )(xe, gc, w1, w3, w2)


def _combine_body(ctx_row, cap, pos_t_ref, ye_ref, x_ref, gate_ref, o_ref):
    row = pl.program_id(0) if ctx_row is None else ctx_row
    K = ye_ref.shape[1]
    assert cap & (cap - 1) == 0 and cap <= 256
    col_expert = lax.broadcasted_iota(jnp.int32, (LANES, K), 1) >> (cap.bit_length() - 1)
    spread = (col_expert == lax.broadcasted_iota(jnp.int32, (LANES, K), 0)).astype(jnp.bfloat16)
    pos = jnp.dot(pos_t_ref[0].astype(jnp.float32).astype(jnp.bfloat16), spread,
                  preferred_element_type=jnp.float32)
    col_slot = (lax.broadcasted_iota(jnp.int32, (1, K), 1) & (cap - 1)).astype(jnp.float32)
    hit = (pos == col_slot).astype(jnp.bfloat16)
    y = jnp.dot(hit, ye_ref[0], preferred_element_type=jnp.float32)
    o_ref[0] = x_ref[0] + _mod_row(gate_ref, row) * y


def ec_combine(x, pos_t, ye, mod, k_gate, cap, ctx_row=None, tt=512, tn=1024):
    B, L, D = x.shape
    tt = min(tt, L)
    tn = min(tn, D)
    K = N_EXPERTS * cap
    return pl.pallas_call(
        functools.partial(_combine_body, ctx_row, cap),
        grid=(B, D // tn, L // tt),
        in_specs=[pl.BlockSpec((1, tt, LANES), lambda b, n, t: (b, t, 0)),
                  pl.BlockSpec((1, K, tn), lambda b, n, t: (b, 0, n)),
                  pl.BlockSpec((1, tt, tn), lambda b, n, t: (b, t, n)),
                  pl.BlockSpec((MOD_ROWS, tn), lambda b, n, t: (0, k_gate * (D // tn) + n))],
        out_specs=pl.BlockSpec((1, tt, tn), lambda b, n, t: (b, t, n)),
        out_shape=jax.ShapeDtypeStruct((B, L, D), jnp.float32),
        compiler_params=_cparams("arbitrary", "arbitrary", "arbitrary"), name="ec_combine",
    )(pos_t, ye.reshape(B, K, D), x, mod)


def moe_block(x, gain, mod, router, w1, w3, w2, layer, ctx_row=None):
    B, L, D = x.shape
    cap = EC_CAPACITY * L // N_EXPERTS
    h, aff = normmod(x, gain, mod, 3, 4, ctx_row=ctx_row, router=router)
    pos_e, gate_e, pos_t = ec_select(aff, cap)
    xe, gc = ec_gather(h, pos_e, gate_e, cap)
    ye = ec_experts(xe, gc, w1, w3, w2, layer)
    return ec_combine(x, pos_t, ye, mod, 5, cap, ctx_row=ctx_row)


def kernel(x, c, ctx, c_ctx, ada_w, ada_b, norm_mix, norm_ffn, w_in, pool_w, pool_scale, conv_w, a_log_f, a_log_b, dt_bias_f, dt_bias_b, gdn_norm, w_out_ab, w_qkv, lam_q1, lam_k1, lam_q2, lam_k2, subln, w_out_c, router, w1, w3, w2, final_norm):
    depth = ada_w.shape[0]
    B, L, D = x.shape
    Lc = ctx.shape[1]
    assert B < MOD_ROWS
    ctx_row = B
    cos, sin = axial_rope_tables(L // GRID_W)
    cond = jnp.zeros((MOD_ROWS, D), jnp.float32).at[:B].set(jax.nn.silu(c)).at[ctx_row].set(jax.nn.silu(c_ctx))
    cond = cond.astype(jnp.bfloat16)
    x_lat, x_ctx = x, ctx
    for layer in range(depth):
        last = layer == depth - 1
        j = layer // 2
        mod = proj(cond, ada_w, layer) + ada_b[layer]
        h_lat = normmod(x_lat, norm_mix[layer], mod, 0, 1)
        h_ctx = normmod(x_ctx, norm_mix[layer], mod, 0, 1, ctx_row=ctx_row)
        if layer % 2 == 0:
            y_ctx, y_lat = pool_gdn_mixer(h_ctx, h_lat, w_in, j, pool_w[j], pool_scale[j], conv_w[j],
                                          a_log_f[j], a_log_b[j], dt_bias_f[j], dt_bias_b[j], gdn_norm[j])
            w_out = w_out_ab
        else:
            assert last, "context outputs of an attention layer are only skipped when no later layer reads them"
            lambda_init = 0.8 - 0.6 * math.exp(-0.3 * layer)
            y_ctx = None
            y_lat = diff_attn_mixer(h_ctx, h_lat, w_qkv, j, lam_q1[j], lam_k1[j], lam_q2[j], lam_k2[j],
                                    subln[j], lambda_init, cos, sin)
            w_out = w_out_c
        x_lat = proj(y_lat.reshape(B * L, D), w_out, j, tm=1024,
                     residual=(x_lat.reshape(B * L, D), mod, 2, L, None)).reshape(B, L, D)
        x_lat = moe_block(x_lat, norm_ffn[layer], mod, router[layer], w1, w3, w2, layer)
        if not last:
            x_ctx = proj(y_ctx.reshape(B * Lc, D), w_out, j, tm=1024,
                         residual=(x_ctx.reshape(B * Lc, D), mod, 2, Lc, ctx_row)).reshape(B, Lc, D)
            x_ctx = moe_block(x_ctx, norm_ffn[layer], mod, router[layer], w1, w3, w2, layer, ctx_row=ctx_row)
    return final_rmsnorm(x_lat, final_norm)
```

```python
import functools
import math

import jax
import jax.numpy as jnp
from jax import lax
from jax.experimental import pallas as pl
from jax.experimental.pallas import tpu as pltpu

D_MODEL = 4096
GRID_W = 64
NORM_EPS = 1e-6
POOL_WINDOWS = (2, 4, 8, 16)
POOL_GROUP = D_MODEL // 16
POOL_WIDTH = len(POOL_WINDOWS) * POOL_GROUP
GDN_HEAD_DIM = 128
GDN_HEADS = (D_MODEL - POOL_WIDTH) // GDN_HEAD_DIM
GDN_WIDTH = GDN_HEADS * GDN_HEAD_DIM
GDN_CONV = 5
GDN_CHUNK = 64
QKV_COLS = 3 * GDN_WIDTH
DIFF_HEAD_DIM = 128
DIFF_HEADS = D_MODEL // (2 * DIFF_HEAD_DIM)
ROPE_BASE = 10000.0
N_EXPERTS = 16
EC_CAPACITY = 2

LANES = 128
MOD_ROWS = 8
PREFIX_BLOCK = 256
VMEM_LIMIT_BYTES = 56 * 1024 * 1024

PROJ_TN = 512
GDN_PAIR = 2 * GDN_CHUNK
GDN_PAIRS_PER_STEP = 2
CONV_PAD = 8
ROW_BLOCK = 128
STATE_W = 4 * GDN_HEADS
REST_TILE0 = QKV_COLS // LANES
IN_COLS_PADDED = -(-(QKV_COLS + STATE_W + POOL_WIDTH + GDN_WIDTH) // PROJ_TN) * PROJ_TN
Z_TILE0 = REST_TILE0 + (STATE_W + POOL_WIDTH) // LANES
Z_SHIFT = LANES - (STATE_W + POOL_WIDTH) % LANES
POOL_PAD = 16
POOL_IN_W = 1152


def _cparams(*sem):
    return pltpu.CompilerParams(dimension_semantics=sem, vmem_limit_bytes=VMEM_LIMIT_BYTES)


def _mod_row(mod_ref, row):
    return mod_ref[pl.ds(row, 1), :]


def _proj_body(rows_per_sample, ctx_row, a_ref, w_ref, *rest):
    if rows_per_sample is None:
        o_ref, wb_ref = rest
    else:
        x_ref, gate_ref, o_ref, wb_ref = rest
    i = pl.program_id(1)

    @pl.when(i == 0)
    def _():
        wb_ref[...] = w_ref[...].astype(jnp.bfloat16)

    y = jnp.dot(a_ref[...], wb_ref[...], preferred_element_type=jnp.float32)
    if rows_per_sample is None:
        o_ref[...] = y.astype(o_ref.dtype)
    else:
        row = (i * a_ref.shape[0]) // rows_per_sample if ctx_row is None else ctx_row
        o_ref[...] = x_ref[...] + _mod_row(gate_ref, row) * y


def proj(a, w, layer, *, col0=0, ncols=None, tm=512, tn=PROJ_TN, out_dtype=jnp.float32, residual=None):
    M, K = a.shape
    ncols = w.shape[2] - col0 if ncols is None else ncols
    tm = min(tm, M)
    assert M % tm == 0 and ncols % tn == 0 and col0 % tn == 0, (M, ncols, col0, tm, tn)
    j0 = col0 // tn
    in_specs = [pl.BlockSpec((tm, K), lambda j, i: (i, 0)),
                pl.BlockSpec((None, K, tn), lambda j, i: (layer, 0, j0 + j))]
    args = [a, w]
    rows_per_sample = ctx_row = None
    if residual is not None:
        x, mod, k_gate, rows_per_sample, ctx_row = residual
        assert ctx_row is not None or rows_per_sample % tm == 0
        in_specs += [pl.BlockSpec((tm, tn), lambda j, i: (i, j)),
                     pl.BlockSpec((MOD_ROWS, tn), lambda j, i: (0, k_gate * (ncols // tn) + j))]
        args += [x, mod]
    return pl.pallas_call(
        functools.partial(_proj_body, rows_per_sample, ctx_row),
        grid=(ncols // tn, M // tm),
        in_specs=in_specs,
        out_specs=pl.BlockSpec((tm, tn), lambda j, i: (i, j)),
        out_shape=jax.ShapeDtypeStruct((M, ncols), out_dtype),
        scratch_shapes=[pltpu.VMEM((K, tn), jnp.bfloat16)],
        compiler_params=_cparams("arbitrary", "arbitrary"), name="proj",
    )(*args)


def _normmod_body(ctx_row, with_router, x_ref, g_ref, shift_ref, scale_ref, *rest):
    row = pl.program_id(0) if ctx_row is None else ctx_row
    x = x_ref[0]
    y = x * lax.rsqrt(jnp.mean(x * x, axis=-1, keepdims=True) + NORM_EPS) * g_ref[...]
    h = (y * (1.0 + _mod_row(scale_ref, row)) + _mod_row(shift_ref, row)).astype(jnp.bfloat16)
    if not with_router:
        (h_ref,) = rest
        h_ref[0] = h
        return
    r_ref, h_ref, aff_ref = rest
    h_ref[0] = h
    logits = jnp.dot(h, r_ref[...], preferred_element_type=jnp.float32)
    lane = lax.broadcasted_iota(jnp.int32, logits.shape, 1)
    logits = jnp.where(lane < N_EXPERTS, logits, -jnp.inf)
    e = jnp.exp(logits - jnp.max(logits, axis=-1, keepdims=True))
    aff_ref[0] = e / jnp.sum(e, axis=-1, keepdims=True)


def normmod(x, gain, mod, k_shift, k_scale, ctx_row=None, router=None, tr=256):
    B, L, D = x.shape
    tr = min(tr, L)
    in_specs = [pl.BlockSpec((1, tr, D), lambda b, t: (b, t, 0)),
                pl.BlockSpec((1, D), lambda b, t: (0, 0)),
                pl.BlockSpec((MOD_ROWS, D), lambda b, t: (0, k_shift)),
                pl.BlockSpec((MOD_ROWS, D), lambda b, t: (0, k_scale))]
    args = [x, gain.reshape(1, D), mod, mod]
    out_specs = [pl.BlockSpec((1, tr, D), lambda b, t: (b, t, 0))]
    out_shape = [jax.ShapeDtypeStruct((B, L, D), jnp.bfloat16)]
    if router is not None:
        rp = jnp.pad(router.astype(jnp.bfloat16), ((0, 0), (0, LANES - router.shape[1])))
        in_specs.append(pl.BlockSpec((D, LANES), lambda b, t: (0, 0)))
        args.append(rp)
        out_specs.append(pl.BlockSpec((1, tr, LANES), lambda b, t: (b, t, 0)))
        out_shape.append(jax.ShapeDtypeStruct((B, L, LANES), jnp.float32))
    out = pl.pallas_call(
        functools.partial(_normmod_body, ctx_row, router is not None),
        grid=(B, L // tr), in_specs=in_specs, out_specs=out_specs, out_shape=out_shape,
        compiler_params=_cparams("arbitrary", "arbitrary"), name="normmod",
    )(*args)
    return out if router is not None else out[0]


def _final_norm_body(x_ref, g_ref, o_ref):
    x = x_ref[...]
    o_ref[...] = x * lax.rsqrt(jnp.mean(x * x, axis=-1, keepdims=True) + NORM_EPS) * g_ref[...]


def final_rmsnorm(x, gain, tr=512):
    B, L, D = x.shape
    M = B * L
    tr = min(tr, M)
    return pl.pallas_call(
        _final_norm_body,
        grid=(M // tr,),
        in_specs=[pl.BlockSpec((tr, D), lambda i: (i, 0)), pl.BlockSpec((1, D), lambda i: (0, 0))],
        out_specs=pl.BlockSpec((tr, D), lambda i: (i, 0)),
        out_shape=jax.ShapeDtypeStruct((M, D), x.dtype),
        compiler_params=_cparams("arbitrary"), name="final_rmsnorm",
    )(x.reshape(M, D), gain.reshape(1, D)).reshape(B, L, D)


def _gates_body(st_ref, alog_ref, dtb_ref, rows_ref, cols_ref):
    T = st_ref.shape[1]
    H = GDN_HEADS
    x = st_ref[0].T
    t = x + dtb_ref[...]
    softplus = jnp.maximum(t, 0.0) + jnp.log1p(jnp.exp(-jnp.abs(t)))
    r_full = lax.broadcasted_iota(jnp.int32, x.shape, 0)
    val = jnp.where(r_full < 2 * H, -jnp.exp(alog_ref[...]) * softplus, jax.nn.sigmoid(x))
    pos = lax.broadcasted_iota(jnp.int32, (LANES, LANES), 1) & (GDN_CHUNK - 1)
    r = lax.broadcasted_iota(jnp.int32, (LANES, LANES), 0)
    tiles = []
    for c in range(T // LANES):
        v = val[:, c * LANES:(c + 1) * LANES]
        pre, suf, s = v, v, 1
        while s < GDN_CHUNK:
            pre = pre + jnp.where(pos >= s, pltpu.roll(pre, s, 1), 0.0)
            suf = suf + jnp.where(pos < GDN_CHUNK - s, pltpu.roll(suf, LANES - s, 1), 0.0)
            s *= 2
        tiles.append(jnp.where(r < H, pre, jnp.where(r < 2 * H, suf, v)))
    out = jnp.concatenate(tiles, axis=1) if len(tiles) > 1 else tiles[0]
    rows_ref[0] = out
    cols_ref[0] = out.T


def gdn_gates(p, a_log_f, a_log_b, dt_bias_f, dt_bias_b):
    B, T, _ = p.shape
    pad = jnp.zeros((LANES - 2 * GDN_HEADS,), jnp.float32)
    alog = jnp.concatenate([a_log_f, a_log_b, pad]).astype(jnp.float32).reshape(LANES, 1)
    dtb = jnp.concatenate([dt_bias_f, dt_bias_b, pad]).astype(jnp.float32).reshape(LANES, 1)
    return pl.pallas_call(
        _gates_body,
        grid=(B,),
        in_specs=[pl.BlockSpec((1, T, LANES), lambda b: (b, 0, REST_TILE0)),
                  pl.BlockSpec((LANES, 1), lambda b: (0, 0)),
                  pl.BlockSpec((LANES, 1), lambda b: (0, 0))],
        out_specs=[pl.BlockSpec((1, LANES, T), lambda b: (b, 0, 0)),
                   pl.BlockSpec((1, T, LANES), lambda b: (b, 0, 0))],
        out_shape=[jax.ShapeDtypeStruct((B, LANES, T), jnp.float32),
                   jax.ShapeDtypeStruct((B, T, LANES), jnp.float32)],
        compiler_params=_cparams("arbitrary"), name="gdn_gates",
    )(p, alog, dtb)


def _conv_silu(x_ref, w_ref, pad_sc, dst_sc, T, unit_norm):
    half = GDN_CONV // 2
    zeros = jnp.zeros((CONV_PAD, LANES), jnp.float32)
    pad_sc[0:CONV_PAD, :] = zeros
    pad_sc[CONV_PAD:CONV_PAD + T, :] = x_ref[0]
    pad_sc[CONV_PAD + T:2 * CONV_PAD + T, :] = zeros
    w = w_ref[...]
    for r0 in range(0, T, ROW_BLOCK):
        y = jnp.zeros((ROW_BLOCK, LANES), jnp.float32)
        for j in range(GDN_CONV):
            s = CONV_PAD + r0 + j - half
            y = y + w[j:j + 1, :] * pad_sc[s:s + ROW_BLOCK, :]
        y = y * jax.nn.sigmoid(y)
        if unit_norm:
            y = y * lax.rsqrt(jnp.sum(y * y, axis=-1, keepdims=True) + NORM_EPS)
        dst_sc[r0:r0 + ROW_BLOCK, :] = y


def _nt(a, b):
    return lax.dot_general(a, b, (((1,), (1,)), ((), ())), preferred_element_type=jnp.float32)


def _gdn_segment(T, h, refs, sc, s_init):
    pq_ref, pk_ref, pv_ref, cols_ref, rows_ref, za_ref, zb_ref, y_ref = refs
    (cwq_ref, cwk_ref, cwv_ref, gn_ref, pad_sc, q_sc, k_sc, v_sc, u_sc, w_sc, qd_sc, kdt_sc, attn_sc, gend_sc,
     o_sc, grow_sc) = sc
    C, P, H = GDN_CHUNK, GDN_PAIR, GDN_HEADS
    n_pairs = T // P
    bf16 = jnp.bfloat16
    scale = GDN_HEAD_DIM ** -0.5
    for d in range(2):
        grow_sc[d, 0:1, 0:T] = rows_ref[0, pl.ds(d * H + h, 1), :]
    _conv_silu(pq_ref, cwq_ref, pad_sc, q_sc, T, True)
    _conv_silu(pk_ref, cwk_ref, pad_sc, k_sc, T, True)
    _conv_silu(pv_ref, cwv_ref, pad_sc, v_sc, T, False)

    lane = lax.broadcasted_iota(jnp.int32, (P, LANES), 1)
    ii = lax.broadcasted_iota(jnp.int32, (C, C), 0)
    jj = lax.broadcasted_iota(jnp.int32, (C, C), 1)

    halves = [slice(c * C, (c + 1) * C) for c in range(2)]

    G = GDN_PAIRS_PER_STEP
    n_steps = n_pairs // G

    def units(i):
        fwd = [(0, G * i + g) for g in range(G)]
        return fwd + [(1, n_pairs - 1 - m) for _, m in fwd]

    def mm(a, b):
        return jnp.dot(a, b, preferred_element_type=jnp.float32)


    def prep_stages(i):
        neg_a, rhs, dest, held = [], [], [], []
        for d, m in units(i):
            r0 = pl.multiple_of(m * P, P)
            q2 = q_sc[pl.ds(r0, P), :]
            k2 = k_sc[pl.ds(r0, P), :]
            v2 = v_sc[pl.ds(r0, P), :]
            cols = cols_ref[0, pl.ds(r0, P), :]
            qs = q2 * scale
            kh = [k2[sl].astype(bf16) for sl in halves]
            qk = [_nt(qs[sl].astype(bf16), kh[c]) for c, sl in enumerate(halves)]
            gcol = jnp.sum(jnp.where(lane == d * H + h, cols, 0.0), axis=1, keepdims=True)
            bcol = jnp.sum(jnp.where(lane == (2 + d) * H + h, cols, 0.0), axis=1, keepdims=True)
            eg = jnp.exp(gcol)
            kb = k2 * bcol
            kk = [_nt(kb[sl].astype(bf16), kh[c]) for c, sl in enumerate(halves)]
            held.append((d, m, r0, k2, v2, qs, qk, kk, gcol, bcol, eg, kb))
        yield
        for d, m, r0, k2, v2, qs, qk, kk, gcol, bcol, eg, kb in held:
            grow = grow_sc[d, 0:1, pl.ds(r0, P)]
            keep = (ii >= jj) if d == 0 else (ii <= jj)
            strict = (ii > jj) if d == 0 else (ii < jj)
            rhs_uw = jnp.concatenate([v2 * bcol, kb * eg], axis=1)
            qd_sc[d, pl.ds(r0, P), :] = (qs * eg).astype(bf16)
            kd = []
            for c, sl in enumerate(halves):
                rc = pl.multiple_of(r0 + c * C, C)
                gc = gcol[sl]
                decay = jnp.exp(jnp.where(keep, gc - grow[:, sl], -jnp.inf))
                neg_a.append(jnp.where(strict, -(kk[c] * decay), 0.0))
                rhs.append(rhs_uw[sl])
                dest.append((d, rc))
                attn_sc[d, pl.ds(rc, C), :] = (qk[c] * decay).astype(bf16)
                g_last = gc[C - 1:C] if d == 0 else gc[0:1]
                kd.append(k2[sl] * jnp.exp(g_last - gc))
                gend_sc[d, pl.ds(pl.multiple_of((2 * m + c) * 8, 8), 8), :] = jnp.broadcast_to(
                    jnp.exp(g_last), (8, LANES))
            kdt_sc[d, :, pl.ds(r0, P)] = jnp.concatenate(kd, axis=0).T.astype(bf16)
        levels = C.bit_length() - 2
        e = list(neg_a)
        nb = [x.astype(bf16) for x in neg_a]
        p = [mm(x, x) for x in nb]
        yield
        for k in range(levels):
            pb = [x.astype(bf16) for x in p]
            ep = [mm(a.astype(bf16), b) for a, b in zip(e, pb)]
            p_next = [mm(b, b) for b in pb] if k + 1 < levels else None
            yield
            e = [a + b + ab for a, b, ab in zip(e, p, ep)]
            p = p_next
        corr = [mm(a.astype(bf16), r.astype(bf16)) for a, r in zip(e, rhs)]
        yield
        for (d, rc), r, cr in zip(dest, rhs, corr):
            s = r + cr
            u_sc[d, pl.ds(rc, C), :] = s[:, :GDN_HEAD_DIM]
            w_sc[d, pl.ds(rc, C), :] = s[:, GDN_HEAD_DIM:].astype(bf16)

    def scan_operands(i):
        ops = ([], [])
        for d, m in units(i):
            r0 = pl.multiple_of(m * P, P)
            kdt = kdt_sc[d, :, pl.ds(r0, P)]
            for c in ((0, 1) if d == 0 else (1, 0)):
                rc = pl.multiple_of(r0 + c * C, C)
                g_end = gend_sc[d, pl.ds(pl.multiple_of((2 * m + c) * 8, 8), 8), :][0:1]
                ops[d].append((d, rc, u_sc[d, pl.ds(rc, C), :], w_sc[d, pl.ds(rc, C), :], qd_sc[d, pl.ds(rc, C), :],
                               attn_sc[d, pl.ds(rc, C), :], kdt[:, c * C:(c + 1) * C], g_end))
        return ops

    def scan_stages(ops, carry, result):
        s = list(carry)
        for k in range(2 * G):
            cur = [ops[d][k] for d in range(2)]
            sb = [s[d].astype(bf16) for d in range(2)]
            ws = [mm(cur[d][3], sb[d]) for d in range(2)]
            qs_ = [mm(cur[d][4], sb[d]) for d in range(2)]
            yield
            vb = [(cur[d][2] - ws[d]).astype(bf16) for d in range(2)]
            av = [mm(cur[d][5], vb[d]) for d in range(2)]
            kv = [mm(cur[d][6], vb[d]) for d in range(2)]
            yield
            for d in range(2):
                o_sc[d, pl.ds(cur[d][1], C), :] = qs_[d] + av[d]
                s[d] = s[d] * cur[d][7] + kv[d]
        result.append(tuple(s))

    def run_interleaved(*gens):
        gens = list(gens)
        while gens:
            for g in list(gens):
                try:
                    next(g)
                except StopIteration:
                    gens.remove(g)

    def step(i, carry):
        result = []
        run_interleaved(prep_stages(i + 1), scan_stages(scan_operands(i), carry, result))
        return result[0]

    run_interleaved(prep_stages(0))
    carry = lax.fori_loop(0, n_steps - 1, step, s_init)
    result = []
    run_interleaved(scan_stages(scan_operands(n_steps - 1), carry, result))
    s_f, s_b = result[0]

    lane_b = lax.broadcasted_iota(jnp.int32, (ROW_BLOCK, LANES), 1)
    for r0 in range(0, T, ROW_BLOCK):
        o = o_sc[0, r0:r0 + ROW_BLOCK, :] + o_sc[1, r0:r0 + ROW_BLOCK, :]
        z = pltpu.roll(jnp.where(lane_b >= LANES - Z_SHIFT, za_ref[0, r0:r0 + ROW_BLOCK, :],
                                 zb_ref[0, r0:r0 + ROW_BLOCK, :]), Z_SHIFT, 1)
        y = o * lax.rsqrt(jnp.mean(o * o, axis=-1, keepdims=True) + NORM_EPS) * gn_ref[...]
        y_ref[0, r0:r0 + ROW_BLOCK, :] = (y * (z * jax.nn.sigmoid(z))).astype(y_ref.dtype)
    return s_f, s_b


def _gdn_body(Tc, Tl, *refs):
    h = pl.program_id(1)
    ctx_refs = refs[0:7]
    lat_refs = refs[7:14]
    shared = refs[14:18]
    y_ctx_ref, y_lat_ref = refs[18:20]
    sc = shared + refs[20:]
    zero = jnp.zeros((GDN_HEAD_DIM, GDN_HEAD_DIM), jnp.float32)
    states = _gdn_segment(Tc, h, ctx_refs + (y_ctx_ref,), sc, (zero, zero))
    _gdn_segment(Tl, h, lat_refs + (y_lat_ref,), sc, states)


def gdn_mix(p_ctx, gates_ctx, p_lat, gates_lat, conv_w, gdn_norm):
    B, Tc, _ = p_ctx.shape
    Tl = p_lat.shape[1]
    H, dh = GDN_HEADS, GDN_HEAD_DIM

    def seg_specs(T):
        return [pl.BlockSpec((1, T, dh), lambda b, h: (b, 0, h)),
                pl.BlockSpec((1, T, dh), lambda b, h: (b, 0, H + h)),
                pl.BlockSpec((1, T, dh), lambda b, h: (b, 0, 2 * H + h)),
                pl.BlockSpec((1, T, LANES), lambda b, h: (b, 0, 0)),
                pl.BlockSpec((1, LANES, T), lambda b, h: (b, 0, 0)),
                pl.BlockSpec((1, T, LANES), lambda b, h: (b, 0, Z_TILE0 + h)),
                pl.BlockSpec((1, T, LANES), lambda b, h: (b, 0, Z_TILE0 + 1 + h))]

    shared_specs = [pl.BlockSpec((GDN_CONV, dh), lambda b, h: (0, h)),
                    pl.BlockSpec((GDN_CONV, dh), lambda b, h: (0, H + h)),
                    pl.BlockSpec((GDN_CONV, dh), lambda b, h: (0, 2 * H + h)),
                    pl.BlockSpec((1, dh), lambda b, h: (0, 0))]
    rows_c, cols_c = gates_ctx
    rows_l, cols_l = gates_lat
    f32, bf16 = jnp.float32, jnp.bfloat16
    scratch = [pltpu.VMEM((Tl + 2 * CONV_PAD, LANES), f32),
               pltpu.VMEM((Tl, dh), f32), pltpu.VMEM((Tl, dh), f32), pltpu.VMEM((Tl, dh), f32),
               pltpu.VMEM((2, Tl, dh), f32), pltpu.VMEM((2, Tl, dh), bf16), pltpu.VMEM((2, Tl, dh), bf16),
               pltpu.VMEM((2, dh, Tl), bf16), pltpu.VMEM((2, Tl, GDN_CHUNK), bf16),
               pltpu.VMEM((2, Tl // GDN_CHUNK * 8, LANES), f32), pltpu.VMEM((2, Tl, dh), f32),
               pltpu.VMEM((2, 8, Tl), f32)]
    return pl.pallas_call(
        functools.partial(_gdn_body, Tc, Tl),
        grid=(B, H),
        in_specs=seg_specs(Tc) + seg_specs(Tl) + shared_specs,
        out_specs=[pl.BlockSpec((1, Tc, dh), lambda b, h: (b, 0, h)),
                   pl.BlockSpec((1, Tl, dh), lambda b, h: (b, 0, h))],
        out_shape=[jax.ShapeDtypeStruct((B, Tc, GDN_WIDTH), bf16), jax.ShapeDtypeStruct((B, Tl, GDN_WIDTH), bf16)],
        scratch_shapes=scratch,
        compiler_params=_cparams("arbitrary", "arbitrary"), name="gdn_mix",
    )(p_ctx, p_ctx, p_ctx, cols_c, rows_c, p_ctx, p_ctx,
      p_lat, p_lat, p_lat, cols_l, rows_l, p_lat, p_lat,
      conv_w, conv_w, conv_w, gdn_norm.reshape(1, dh))


def _pool_body(x_ref, w_ref, sc_ref, y_ref, pad_sc):
    T = x_ref.shape[1]
    G = POOL_GROUP
    zeros = jnp.zeros((POOL_PAD, G), jnp.float32)
    t = lax.broadcasted_iota(jnp.int32, (ROW_BLOCK, 1), 0)
    for i, win in enumerate(POOL_WINDOWS):
        half = win // 2
        lo = (STATE_W + i * G) // LANES * LANES
        off = STATE_W + i * G - lo
        pad_sc[0:POOL_PAD, :] = zeros
        pad_sc[POOL_PAD:POOL_PAD + T, :] = x_ref[0, :, lo:lo + G + LANES][:, off:off + G]
        pad_sc[POOL_PAD + T:2 * POOL_PAD + T, :] = zeros
        wb = w_ref[i].astype(jnp.bfloat16)
        for r0 in range(0, T, ROW_BLOCK):
            tot = jnp.zeros((ROW_BLOCK, G), jnp.float32)
            for s in range(-half, half):
                tot = tot + pad_sc[POOL_PAD + r0 + s:POOL_PAD + r0 + s + ROW_BLOCK, :]
            cnt = (jnp.minimum(t + r0 + half, T) - jnp.maximum(t + r0 - half, 0)).astype(jnp.float32)
            centred = tot / cnt - pad_sc[POOL_PAD + r0:POOL_PAD + r0 + ROW_BLOCK, :]
            y = jnp.dot(centred.astype(jnp.bfloat16), wb, preferred_element_type=jnp.float32)
            y_ref[0, r0:r0 + ROW_BLOCK, i * G:(i + 1) * G] = (y * sc_ref[:, i * G:(i + 1) * G]).astype(y_ref.dtype)


def pool_mix(p, pool_w, pool_scale):
    B, T, _ = p.shape
    n = len(POOL_WINDOWS)
    assert QKV_COLS % POOL_IN_W == 0 and STATE_W + POOL_WIDTH <= POOL_IN_W
    return pl.pallas_call(
        _pool_body,
        grid=(B,),
        in_specs=[pl.BlockSpec((1, T, POOL_IN_W), lambda b: (b, 0, QKV_COLS // POOL_IN_W)),
                  pl.BlockSpec((n, POOL_GROUP, POOL_GROUP), lambda b: (0, 0, 0)),
                  pl.BlockSpec((1, POOL_WIDTH), lambda b: (0, 0))],
        out_specs=pl.BlockSpec((1, T, POOL_WIDTH), lambda b: (b, 0, 0)),
        out_shape=jax.ShapeDtypeStruct((B, T, POOL_WIDTH), jnp.bfloat16),
        scratch_shapes=[pltpu.VMEM((T + 2 * POOL_PAD, POOL_GROUP), jnp.float32)],
        compiler_params=_cparams("arbitrary"), name="pool_mix",
    )(p, pool_w, pool_scale.reshape(1, POOL_WIDTH))


def pool_gdn_mixer(h_ctx, h_lat, w_in, j, pool_w, pool_scale, conv_w, a_log_f, a_log_b, dt_bias_f, dt_bias_b,
                   gdn_norm):
    B, Tl, D = h_lat.shape
    Tc = h_ctx.shape[1]
    p_lat = proj(h_lat.reshape(B * Tl, D), w_in, j, ncols=IN_COLS_PADDED, tm=1024).reshape(B, Tl, IN_COLS_PADDED)
    p_ctx = proj(h_ctx.reshape(B * Tc, D), w_in, j, ncols=IN_COLS_PADDED, tm=1024).reshape(B, Tc, IN_COLS_PADDED)
    gates_ctx = gdn_gates(p_ctx, a_log_f, a_log_b, dt_bias_f, dt_bias_b)
    gates_lat = gdn_gates(p_lat, a_log_f, a_log_b, dt_bias_f, dt_bias_b)
    g_ctx, g_lat = gdn_mix(p_ctx, gates_ctx, p_lat, gates_lat, conv_w, gdn_norm)
    y_ctx = jnp.concatenate([pool_mix(p_ctx, pool_w, pool_scale), g_ctx], axis=-1)
    y_lat = jnp.concatenate([pool_mix(p_lat, pool_w, pool_scale), g_lat], axis=-1)
    return y_ctx, y_lat


def axial_rope_tables(rows):
    pos_row = jnp.repeat(jnp.arange(rows), GRID_W).astype(jnp.float32)
    pos_col = jnp.tile(jnp.arange(GRID_W), rows).astype(jnp.float32)
    quarter = DIFF_HEAD_DIM // 4
    inv_freq = ROPE_BASE ** (-jnp.arange(quarter, dtype=jnp.float32) / quarter)
    ang_r = pos_row[:, None] * inv_freq[None, :]
    ang_c = pos_col[:, None] * inv_freq[None, :]
    ang = jnp.concatenate([ang_r, ang_r, ang_c, ang_c], axis=-1)
    return jnp.cos(ang), jnp.sin(ang)


def _rope(x, cos, sin_signed):
    d = x.shape[-1]
    first = (lax.broadcasted_iota(jnp.int32, x.shape, 1) & (d // 2 - 1)) < d // 4
    partner = jnp.where(first, pltpu.roll(x, d - d // 4, 1), pltpu.roll(x, d // 4, 1))
    return x * cos + partner * sin_signed


def _diff_attn_body(lambda_init, tq, q_ref, kl_ref, vl_ref, kc_ref, vc_ref, cos_ref, sin_ref, lam_ref, g_ref,
                    o_ref, k_sc, v_sc):
    d = DIFF_HEAD_DIM
    L = kl_ref.shape[1]
    scale = d ** -0.5
    lv = lam_ref[...]
    lam = (jnp.exp(jnp.sum(lv[0:1] * lv[1:2], axis=1, keepdims=True))
           - jnp.exp(jnp.sum(lv[2:3] * lv[3:4], axis=1, keepdims=True)) + lambda_init)
    cos = cos_ref[...]
    sin = sin_ref[...]
    for m in range(2):
        k_sc[0:L, m * d:(m + 1) * d] = _rope(kl_ref[0, :, m * d:(m + 1) * d], cos, sin).astype(jnp.bfloat16)
    k_sc[L:, :] = kc_ref[0].astype(jnp.bfloat16)
    v_sc[0:L, :] = vl_ref[0].astype(jnp.bfloat16)
    v_sc[L:, :] = vc_ref[0].astype(jnp.bfloat16)

    def scores(r0, m):
        q = _rope(q_ref[0, pl.ds(r0, tq), m * d:(m + 1) * d], cos_ref[pl.ds(r0, tq), :],
                  sin_ref[pl.ds(r0, tq), :]).astype(jnp.bfloat16)
        return lax.dot_general(q, k_sc[:, m * d:(m + 1) * d], (((1,), (1,)), ((), ())),
                               preferred_element_type=jnp.float32)

    def finish(r0, s_pair):
        probs = []
        for s in s_pair:
            s = s * scale
            e = jnp.exp(s - jnp.max(s, axis=-1, keepdims=True))
            probs.append(e * (1.0 / jnp.sum(e, axis=-1, keepdims=True)))
        a = (probs[0] - lam * probs[1]).astype(jnp.bfloat16)
        o = jnp.dot(a, v_sc[...], preferred_element_type=jnp.float32)
        o = o * lax.rsqrt(jnp.mean(o * o, axis=-1, keepdims=True) + NORM_EPS) * g_ref[...]
        o_ref[0, pl.ds(r0, tq), :] = (o * (1 - lambda_init)).astype(o_ref.dtype)

    def q_tile_pair(t, carry):
        ra = pl.multiple_of(2 * t * tq, tq)
        rb = pl.multiple_of(ra + tq, tq)
        sa = [scores(ra, m) for m in range(2)]
        sb = [scores(rb, m) for m in range(2)]
        finish(ra, sa)
        finish(rb, sb)
        return carry

    lax.fori_loop(0, L // (2 * tq), q_tile_pair, 0)


def diff_attention_lat(qkv_lat, kv_ctx, cos, sin_signed, lam_vecs, subln, lambda_init, tq=128):
    B, L, _ = qkv_lat.shape
    Lc = kv_ctx.shape[1]
    H, hd = DIFF_HEADS, 2 * DIFF_HEAD_DIM
    tq = min(tq, L // 2)
    assert L % (2 * tq) == 0
    return pl.pallas_call(
        functools.partial(_diff_attn_body, lambda_init, tq),
        grid=(B, H),
        in_specs=[pl.BlockSpec((1, L, hd), lambda b, h: (b, 0, h)),
                  pl.BlockSpec((1, L, hd), lambda b, h: (b, 0, H + h)),
                  pl.BlockSpec((1, L, hd), lambda b, h: (b, 0, 2 * H + h)),
                  pl.BlockSpec((1, Lc, hd), lambda b, h: (b, 0, h)),
                  pl.BlockSpec((1, Lc, hd), lambda b, h: (b, 0, H + h)),
                  pl.BlockSpec((L, DIFF_HEAD_DIM), lambda b, h: (0, 0)),
                  pl.BlockSpec((L, DIFF_HEAD_DIM), lambda b, h: (0, 0)),
                  pl.BlockSpec((4, DIFF_HEAD_DIM), lambda b, h: (0, 0)),
                  pl.BlockSpec((1, hd), lambda b, h: (0, 0))],
        out_specs=pl.BlockSpec((1, L, hd), lambda b, h: (b, 0, h)),
        out_shape=jax.ShapeDtypeStruct((B, L, H * hd), jnp.bfloat16),
        scratch_shapes=[pltpu.VMEM((L + Lc, hd), jnp.bfloat16), pltpu.VMEM((L + Lc, hd), jnp.bfloat16)],
        compiler_params=_cparams("arbitrary", "arbitrary"), name="diff_attention",
    )(qkv_lat, qkv_lat, qkv_lat, kv_ctx, kv_ctx, cos, sin_signed, lam_vecs, subln.reshape(1, hd))


def diff_attn_mixer(h_ctx, h_lat, w_qkv, j, lam_q1, lam_k1, lam_q2, lam_k2, subln, lambda_init, cos, sin):
    B, L, D = h_lat.shape
    Lc = h_ctx.shape[1]
    qkv = proj(h_lat.reshape(B * L, D), w_qkv, j, tm=1024).reshape(B, L, 3 * D)
    kv_ctx = proj(h_ctx.reshape(B * Lc, D), w_qkv, j, col0=D, tm=1024).reshape(B, Lc, 2 * D)
    quarter = DIFF_HEAD_DIM // 4
    first = (jnp.arange(DIFF_HEAD_DIM) % (2 * quarter)) < quarter
    sin_signed = jnp.where(first[None, :], -sin, sin)
    lam_vecs = jnp.stack([lam_q1, lam_k1, lam_q2, lam_k2]).astype(jnp.float32)
    return diff_attention_lat(qkv, kv_ctx, cos, sin_signed, lam_vecs, subln, lambda_init)


def _count_before(mask):
    R, L = mask.shape
    W = PREFIX_BLOCK
    m = mask.astype(jnp.float32)
    before = (lax.broadcasted_iota(jnp.int32, (W, W), 0) < lax.broadcasted_iota(jnp.int32, (W, W), 1)
              ).astype(jnp.bfloat16)
    carry = jnp.zeros((R, 1), jnp.float32)
    out = []
    for c in range(L // W):
        blk = m[:, c * W:(c + 1) * W]
        out.append(jnp.dot(blk.astype(jnp.bfloat16), before, preferred_element_type=jnp.float32) + carry)
        carry = carry + jnp.sum(blk, axis=1, keepdims=True)
    return out[0] if len(out) == 1 else jnp.concatenate(out, axis=1)


def _select_body(cap, aff_ref, pos_e_ref, gate_e_ref, pos_t_ref):
    L = aff_ref.shape[1]
    a = aff_ref[0].T[:N_EXPERTS]
    bits = pltpu.bitcast(a, jnp.int32)
    thr = jnp.zeros((N_EXPERTS, 1), jnp.int32)
    for bit in range(30, -1, -1):
        cand = thr | (1 << bit)
        cnt = jnp.sum((bits >= cand).astype(jnp.float32), axis=1, keepdims=True)
        thr = jnp.where(cnt >= cap, cand, thr)
    gt = bits > thr
    eq = bits == thr
    need = cap - jnp.sum(gt.astype(jnp.float32), axis=1, keepdims=True)
    sel = gt | (eq & (_count_before(eq) < need))
    pos = jnp.where(sel, _count_before(sel).astype(jnp.int32), -1)
    pos_e_ref[0] = pos
    gate_e_ref[0] = jnp.where(sel, a, 0.0)
    pad = jnp.full((LANES - N_EXPERTS, L), -1, jnp.int32)
    pos_t_ref[0] = jnp.concatenate([pos, pad], axis=0).T


def ec_select(aff, cap):
    B, L, _ = aff.shape
    return pl.pallas_call(
        functools.partial(_select_body, cap),
        grid=(B,),
        in_specs=[pl.BlockSpec((1, L, LANES), lambda b: (b, 0, 0))],
        out_specs=[pl.BlockSpec((1, N_EXPERTS, L), lambda b: (b, 0, 0)),
                   pl.BlockSpec((1, N_EXPERTS, L), lambda b: (b, 0, 0)),
                   pl.BlockSpec((1, L, LANES), lambda b: (b, 0, 0))],
        out_shape=[jax.ShapeDtypeStruct((B, N_EXPERTS, L), jnp.int32),
                   jax.ShapeDtypeStruct((B, N_EXPERTS, L), jnp.float32),
                   jax.ShapeDtypeStruct((B, L, LANES), jnp.int32)],
        compiler_params=_cparams("arbitrary"), name="ec_select",
    )(aff)


def _gather_body(h_ref, pos_e_ref, gate_e_ref, xe_ref, gc_ref):
    e = pl.program_id(1)
    cap = xe_ref.shape[2]
    L = h_ref.shape[1]
    pos_row = pos_e_ref[0, pl.ds(e, 1), :]
    gate_row = gate_e_ref[0, pl.ds(e, 1), :]
    hit = lax.broadcasted_iota(jnp.int32, (cap, L), 0) == pos_row
    xe_ref[0, 0] = jnp.dot(hit.astype(jnp.bfloat16), h_ref[0],
                           preferred_element_type=jnp.float32).astype(jnp.bfloat16)
    gc_ref[0, 0] = jnp.sum(jnp.where(hit, gate_row, 0.0), axis=1, keepdims=True)


def ec_gather(h, pos_e, gate_e, cap):
    B, L, D = h.shape
    return pl.pallas_call(
        _gather_body,
        grid=(B, N_EXPERTS),
        in_specs=[pl.BlockSpec((1, L, D), lambda b, e: (b, 0, 0)),
                  pl.BlockSpec((1, N_EXPERTS, L), lambda b, e: (b, 0, 0)),
                  pl.BlockSpec((1, N_EXPERTS, L), lambda b, e: (b, 0, 0))],
        out_specs=[pl.BlockSpec((1, 1, cap, D), lambda b, e: (b, e, 0, 0)),
                   pl.BlockSpec((1, 1, cap, 1), lambda b, e: (b, e, 0, 0))],
        out_shape=[jax.ShapeDtypeStruct((B, N_EXPERTS, cap, D), jnp.bfloat16),
                   jax.ShapeDtypeStruct((B, N_EXPERTS, cap, 1), jnp.float32)],
        compiler_params=_cparams("arbitrary", "arbitrary"), name="ec_gather",
    )(h, pos_e, gate_e)


def _experts_body(nf, xe_ref, gc_ref, w1_ref, w3_ref, w2_ref, ye_ref, hid_ref):
    s = pl.program_id(1)
    B, _, cap, D = xe_ref.shape

    @pl.when(s < nf)
    def _():
        x = xe_ref[...].reshape(B * cap, D)
        h1 = jnp.dot(x, w1_ref[0].astype(jnp.bfloat16), preferred_element_type=jnp.float32)
        h3 = jnp.dot(x, w3_ref[0].astype(jnp.bfloat16), preferred_element_type=jnp.float32)
        hid_ref[s] = (h1 * jax.nn.sigmoid(h1) * h3).astype(jnp.bfloat16)

    @pl.when(s >= nf)
    def _():
        hid = jnp.concatenate([hid_ref[f] for f in range(nf)], axis=1) if nf > 1 else hid_ref[0]
        y = jnp.dot(hid, w2_ref[0].astype(jnp.bfloat16), preferred_element_type=jnp.float32)
        y = y * gc_ref[...].reshape(B * cap, 1)
        ye_ref[...] = y.astype(jnp.bfloat16).reshape(ye_ref.shape)


def ec_experts(xe, gc, w1, w3, w2, layer, tf=256, tn=1024):
    B, E, cap, D = xe.shape
    F = w1.shape[3]
    tf = min(tf, F)
    tn = min(tn, D)
    nf, nn = F // tf, D // tn
    return pl.pallas_call(
        functools.partial(_experts_body, nf),
        grid=(E, nf + nn),
        in_specs=[pl.BlockSpec((B, 1, cap, D), lambda e, s: (0, e, 0, 0), pipeline_mode=pl.Buffered(1)),
                  pl.BlockSpec((B, 1, cap, 1), lambda e, s: (0, e, 0, 0)),
                  pl.BlockSpec((None, 1, D, tf), lambda e, s: (layer, e, 0, jnp.minimum(s, nf - 1))),
                  pl.BlockSpec((None, 1, D, tf), lambda e, s: (layer, e, 0, jnp.minimum(s, nf - 1))),
                  pl.BlockSpec((None, 1, F, tn), lambda e, s: (layer, e, 0, jnp.maximum(s - nf, 0)))],
        out_specs=pl.BlockSpec((B, 1, cap, tn), lambda e, s: (0, e, 0, jnp.maximum(s - nf, 0))),
        out_shape=jax.ShapeDtypeStruct((B, E, cap, D), jnp.bfloat16),
        scratch_shapes=[pltpu.VMEM((nf, B * cap, tf), jnp.bfloat16)],
        compiler_params=_cparams("arbitrary", "arbitrary"), name="ec_experts",
    )(xe, gc, w1, w3, w2)


def _combine_body(ctx_row, cap, pos_t_ref, ye_ref, x_ref, gate_ref, o_ref):
    row = pl.program_id(0) if ctx_row is None else ctx_row
    K = ye_ref.shape[1]
    assert cap & (cap - 1) == 0 and cap <= 256
    col_expert = lax.broadcasted_iota(jnp.int32, (LANES, K), 1) >> (cap.bit_length() - 1)
    spread = (col_expert == lax.broadcasted_iota(jnp.int32, (LANES, K), 0)).astype(jnp.bfloat16)
    pos = jnp.dot(pos_t_ref[0].astype(jnp.float32).astype(jnp.bfloat16), spread,
                  preferred_element_type=jnp.float32)
    col_slot = (lax.broadcasted_iota(jnp.int32, (1, K), 1) & (cap - 1)).astype(jnp.float32)
    hit = (pos == col_slot).astype(jnp.bfloat16)
    y = jnp.dot(hit, ye_ref[0], preferred_element_type=jnp.float32)
    o_ref[0] = x_ref[0] + _mod_row(gate_ref, row) * y


def ec_combine(x, pos_t, ye, mod, k_gate, cap, ctx_row=None, tt=512, tn=1024):
    B, L, D = x.shape
    tt = min(tt, L)
    tn = min(tn, D)
    K = N_EXPERTS * cap
    return pl.pallas_call(
        functools.partial(_combine_body, ctx_row, cap),
        grid=(B, D // tn, L // tt),
        in_specs=[pl.BlockSpec((1, tt, LANES), lambda b, n, t: (b, t, 0)),
                  pl.BlockSpec((1, K, tn), lambda b, n, t: (b, 0, n)),
                  pl.BlockSpec((1, tt, tn), lambda b, n, t: (b, t, n)),
                  pl.BlockSpec((MOD_ROWS, tn), lambda b, n, t: (0, k_gate * (D // tn) + n))],
        out_specs=pl.BlockSpec((1, tt, tn), lambda b, n, t: (b, t, n)),
        out_shape=jax.ShapeDtypeStruct((B, L, D), jnp.float32),
        compiler_params=_cparams("arbitrary", "arbitrary", "arbitrary"), name="ec_combine",
    )(pos_t, ye.reshape(B, K, D), x, mod)


def moe_block(x, gain, mod, router, w1, w3, w2, layer, ctx_row=None):
    B, L, D = x.shape
    cap = EC_CAPACITY * L // N_EXPERTS
    h, aff = normmod(x, gain, mod, 3, 4, ctx_row=ctx_row, router=router)
    pos_e, gate_e, pos_t = ec_select(aff, cap)
    xe, gc = ec_gather(h, pos_e, gate_e, cap)
    ye = ec_experts(xe, gc, w1, w3, w2, layer)
    return ec_combine(x, pos_t, ye, mod, 5, cap, ctx_row=ctx_row)


def kernel(x, c, ctx, c_ctx, ada_w, ada_b, norm_mix, norm_ffn, w_in, pool_w, pool_scale, conv_w, a_log_f, a_log_b, dt_bias_f, dt_bias_b, gdn_norm, w_out_ab, w_qkv, lam_q1, lam_k1, lam_q2, lam_k2, subln, w_out_c, router, w1, w3, w2, final_norm):
    depth = ada_w.shape[0]
    B, L, D = x.shape
    Lc = ctx.shape[1]
    assert B < MOD_ROWS
    ctx_row = B
    cos, sin = axial_rope_tables(L // GRID_W)
    cond = jnp.zeros((MOD_ROWS, D), jnp.float32).at[:B].set(jax.nn.silu(c)).at[ctx_row].set(jax.nn.silu(c_ctx))
    cond = cond.astype(jnp.bfloat16)
    x_lat, x_ctx = x, ctx
    for layer in range(depth):
        last = layer == depth - 1
        j = layer // 2
        mod = proj(cond, ada_w, layer) + ada_b[layer]
        h_lat = normmod(x_lat, norm_mix[layer], mod, 0, 1)
        h_ctx = normmod(x_ctx, norm_mix[layer], mod, 0, 1, ctx_row=ctx_row)
        if layer % 2 == 0:
            y_ctx, y_lat = pool_gdn_mixer(h_ctx, h_lat, w_in, j, pool_w[j], pool_scale[j], conv_w[j],
                                          a_log_f[j], a_log_b[j], dt_bias_f[j], dt_bias_b[j], gdn_norm[j])
            w_out = w_out_ab
        else:
            assert last, "context outputs of an attention layer are only skipped when no later layer reads them"
            lambda_init = 0.8 - 0.6 * math.exp(-0.3 * layer)
            y_ctx = None
            y_lat = diff_attn_mixer(h_ctx, h_lat, w_qkv, j, lam_q1[j], lam_k1[j], lam_q2[j], lam_k2[j],
                                    subln[j], lambda_init, cos, sin)
            w_out = w_out_c
        x_lat = proj(y_lat.reshape(B * L, D), w_out, j, tm=1024,
                     residual=(x_lat.reshape(B * L, D), mod, 2, L, None)).reshape(B, L, D)
        x_lat = moe_block(x_lat, norm_ffn[layer], mod, router[layer], w1, w3, w2, layer)
        if not last:
            x_ctx = proj(y_ctx.reshape(B * Lc, D), w_out, j, tm=1024,
                         residual=(x_ctx.reshape(B * Lc, D), mod, 2, Lc, ctx_row)).reshape(B, Lc, D)
            x_ctx = moe_block(x_ctx, norm_ffn[layer], mod, router[layer], w1, w3, w2, layer, ctx_row=ctx_row)
    return final_rmsnorm(x_lat, final_norm)
```

```python
import functools
import math

import jax
import jax.numpy as jnp
from jax import lax
from jax.experimental import pallas as pl
from jax.experimental.pallas import tpu as pltpu

D_MODEL = 4096
GRID_W = 64
NORM_EPS = 1e-6
POOL_WINDOWS = (2, 4, 8, 16)
POOL_GROUP = D_MODEL // 16
POOL_WIDTH = len(POOL_WINDOWS) * POOL_GROUP
GDN_HEAD_DIM = 128
GDN_HEADS = (D_MODEL - POOL_WIDTH) // GDN_HEAD_DIM
GDN_WIDTH = GDN_HEADS * GDN_HEAD_DIM
GDN_CONV = 5
GDN_CHUNK = 64
QKV_COLS = 3 * GDN_WIDTH
DIFF_HEAD_DIM = 128
DIFF_HEADS = D_MODEL // (2 * DIFF_HEAD_DIM)
ROPE_BASE = 10000.0
N_EXPERTS = 16
EC_CAPACITY = 2

LANES = 128
MOD_ROWS = 8
PREFIX_BLOCK = 256
VMEM_LIMIT_BYTES = 56 * 1024 * 1024

PROJ_TN = 512
GDN_PAIR = 2 * GDN_CHUNK
GDN_PAIRS_PER_STEP = 2
CONV_PAD = 8
ROW_BLOCK = 128
STATE_W = 4 * GDN_HEADS
REST_TILE0 = QKV_COLS // LANES
IN_COLS_PADDED = -(-(QKV_COLS + STATE_W + POOL_WIDTH + GDN_WIDTH) // PROJ_TN) * PROJ_TN
Z_TILE0 = REST_TILE0 + (STATE_W + POOL_WIDTH) // LANES
Z_SHIFT = LANES - (STATE_W + POOL_WIDTH) % LANES
POOL_PAD = 16
POOL_IN_W = 1152


def _cparams(*sem):
    return pltpu.CompilerParams(dimension_semantics=sem, vmem_limit_bytes=VMEM_LIMIT_BYTES)


def _mod_row(mod_ref, row):
    return mod_ref[pl.ds(row, 1), :]


def _proj_body(n_parts, rows_per_sample, ctx_row, *refs):
    a_refs, w_ref = refs[:n_parts], refs[n_parts]
    if rows_per_sample is None:
        o_ref, wb_ref = refs[n_parts + 1:]
    else:
        x_ref, gate_ref, o_ref, wb_ref = refs[n_parts + 1:]
    i = pl.program_id(1)

    @pl.when(i == 0)
    def _():
        wb_ref[...] = w_ref[...].astype(jnp.bfloat16)

    y, k0 = None, 0
    for a_ref in a_refs:
        k1 = k0 + a_ref.shape[1]
        part = jnp.dot(a_ref[...], wb_ref[k0:k1, :], preferred_element_type=jnp.float32)
        y = part if y is None else y + part
        k0 = k1
    if rows_per_sample is None:
        o_ref[...] = y.astype(o_ref.dtype)
    else:
        row = (i * o_ref.shape[0]) // rows_per_sample if ctx_row is None else ctx_row
        o_ref[...] = x_ref[...] + _mod_row(gate_ref, row) * y


def proj(a, w, layer, *, col0=0, ncols=None, tm=512, tn=PROJ_TN, out_dtype=jnp.float32, residual=None):
    parts = list(a) if isinstance(a, (list, tuple)) else [a]
    M = parts[0].shape[0]
    K = sum(p.shape[1] for p in parts)
    ncols = w.shape[2] - col0 if ncols is None else ncols
    tm = min(tm, M)
    assert M % tm == 0 and ncols % tn == 0 and col0 % tn == 0, (M, ncols, col0, tm, tn)
    j0 = col0 // tn
    in_specs = [pl.BlockSpec((tm, p.shape[1]), lambda j, i: (i, 0)) for p in parts]
    in_specs.append(pl.BlockSpec((None, K, tn), lambda j, i: (layer, 0, j0 + j)))
    args = parts + [w]
    rows_per_sample = ctx_row = None
    if residual is not None:
        x, mod, k_gate, rows_per_sample, ctx_row = residual
        assert ctx_row is not None or rows_per_sample % tm == 0
        in_specs += [pl.BlockSpec((tm, tn), lambda j, i: (i, j)),
                     pl.BlockSpec((MOD_ROWS, tn), lambda j, i: (0, k_gate * (ncols // tn) + j))]
        args += [x, mod]
    return pl.pallas_call(
        functools.partial(_proj_body, len(parts), rows_per_sample, ctx_row),
        grid=(ncols // tn, M // tm),
        in_specs=in_specs,
        out_specs=pl.BlockSpec((tm, tn), lambda j, i: (i, j)),
        out_shape=jax.ShapeDtypeStruct((M, ncols), out_dtype),
        scratch_shapes=[pltpu.VMEM((K, tn), jnp.bfloat16)],
        compiler_params=_cparams("arbitrary", "arbitrary"), name="proj",
    )(*args)


def _normmod_body(ctx_row, with_router, x_ref, g_ref, shift_ref, scale_ref, *rest):
    row = pl.program_id(0) if ctx_row is None else ctx_row
    x = x_ref[0]
    y = x * lax.rsqrt(jnp.mean(x * x, axis=-1, keepdims=True) + NORM_EPS) * g_ref[...]
    h = (y * (1.0 + _mod_row(scale_ref, row)) + _mod_row(shift_ref, row)).astype(jnp.bfloat16)
    if not with_router:
        (h_ref,) = rest
        h_ref[0] = h
        return
    r_ref, h_ref, aff_ref = rest
    h_ref[0] = h
    logits = jnp.dot(h, r_ref[...], preferred_element_type=jnp.float32)
    lane = lax.broadcasted_iota(jnp.int32, logits.shape, 1)
    logits = jnp.where(lane < N_EXPERTS, logits, -jnp.inf)
    e = jnp.exp(logits - jnp.max(logits, axis=-1, keepdims=True))
    aff_ref[0] = e / jnp.sum(e, axis=-1, keepdims=True)


def normmod(x, gain, mod, k_shift, k_scale, ctx_row=None, router=None, tr=256):
    B, L, D = x.shape
    tr = min(tr, L)
    in_specs = [pl.BlockSpec((1, tr, D), lambda b, t: (b, t, 0)),
                pl.BlockSpec((1, D), lambda b, t: (0, 0)),
                pl.BlockSpec((MOD_ROWS, D), lambda b, t: (0, k_shift)),
                pl.BlockSpec((MOD_ROWS, D), lambda b, t: (0, k_scale))]
    args = [x, gain.reshape(1, D), mod, mod]
    out_specs = [pl.BlockSpec((1, tr, D), lambda b, t: (b, t, 0))]
    out_shape = [jax.ShapeDtypeStruct((B, L, D), jnp.bfloat16)]
    if router is not None:
        rp = jnp.pad(router.astype(jnp.bfloat16), ((0, 0), (0, LANES - router.shape[1])))
        in_specs.append(pl.BlockSpec((D, LANES), lambda b, t: (0, 0)))
        args.append(rp)
        out_specs.append(pl.BlockSpec((1, tr, LANES), lambda b, t: (b, t, 0)))
        out_shape.append(jax.ShapeDtypeStruct((B, L, LANES), jnp.float32))
    out = pl.pallas_call(
        functools.partial(_normmod_body, ctx_row, router is not None),
        grid=(B, L // tr), in_specs=in_specs, out_specs=out_specs, out_shape=out_shape,
        compiler_params=_cparams("arbitrary", "arbitrary"), name="normmod",
    )(*args)
    return out if router is not None else out[0]


def _final_norm_body(x_ref, g_ref, o_ref):
    x = x_ref[...]
    o_ref[...] = x * lax.rsqrt(jnp.mean(x * x, axis=-1, keepdims=True) + NORM_EPS) * g_ref[...]


def final_rmsnorm(x, gain, tr=512):
    B, L, D = x.shape
    M = B * L
    tr = min(tr, M)
    return pl.pallas_call(
        _final_norm_body,
        grid=(M // tr,),
        in_specs=[pl.BlockSpec((tr, D), lambda i: (i, 0)), pl.BlockSpec((1, D), lambda i: (0, 0))],
        out_specs=pl.BlockSpec((tr, D), lambda i: (i, 0)),
        out_shape=jax.ShapeDtypeStruct((M, D), x.dtype),
        compiler_params=_cparams("arbitrary"), name="final_rmsnorm",
    )(x.reshape(M, D), gain.reshape(1, D)).reshape(B, L, D)


def _gates_body(st_ref, alog_ref, dtb_ref, rows_ref, cols_ref):
    T = st_ref.shape[1]
    H = GDN_HEADS
    x = st_ref[0].T
    t = x + dtb_ref[...]
    softplus = jnp.maximum(t, 0.0) + jnp.log1p(jnp.exp(-jnp.abs(t)))
    r_full = lax.broadcasted_iota(jnp.int32, x.shape, 0)
    val = jnp.where(r_full < 2 * H, -jnp.exp(alog_ref[...]) * softplus, jax.nn.sigmoid(x))
    pos = lax.broadcasted_iota(jnp.int32, (LANES, LANES), 1) & (GDN_CHUNK - 1)
    r = lax.broadcasted_iota(jnp.int32, (LANES, LANES), 0)
    tiles = []
    for c in range(T // LANES):
        v = val[:, c * LANES:(c + 1) * LANES]
        pre, suf, s = v, v, 1
        while s < GDN_CHUNK:
            pre = pre + jnp.where(pos >= s, pltpu.roll(pre, s, 1), 0.0)
            suf = suf + jnp.where(pos < GDN_CHUNK - s, pltpu.roll(suf, LANES - s, 1), 0.0)
            s *= 2
        tiles.append(jnp.where(r < H, pre, jnp.where(r < 2 * H, suf, v)))
    out = jnp.concatenate(tiles, axis=1) if len(tiles) > 1 else tiles[0]
    rows_ref[0] = out
    cols_ref[0] = out.T


def gdn_gates(p, a_log_f, a_log_b, dt_bias_f, dt_bias_b):
    B, T, _ = p.shape
    pad = jnp.zeros((LANES - 2 * GDN_HEADS,), jnp.float32)
    alog = jnp.concatenate([a_log_f, a_log_b, pad]).astype(jnp.float32).reshape(LANES, 1)
    dtb = jnp.concatenate([dt_bias_f, dt_bias_b, pad]).astype(jnp.float32).reshape(LANES, 1)
    return pl.pallas_call(
        _gates_body,
        grid=(B,),
        in_specs=[pl.BlockSpec((1, T, LANES), lambda b: (b, 0, REST_TILE0)),
                  pl.BlockSpec((LANES, 1), lambda b: (0, 0)),
                  pl.BlockSpec((LANES, 1), lambda b: (0, 0))],
        out_specs=[pl.BlockSpec((1, LANES, T), lambda b: (b, 0, 0)),
                   pl.BlockSpec((1, T, LANES), lambda b: (b, 0, 0))],
        out_shape=[jax.ShapeDtypeStruct((B, LANES, T), jnp.float32),
                   jax.ShapeDtypeStruct((B, T, LANES), jnp.float32)],
        compiler_params=_cparams("arbitrary"), name="gdn_gates",
    )(p, alog, dtb)


def _conv_silu(x_ref, w_ref, pad_sc, dst_sc, T, unit_norm):
    half = GDN_CONV // 2
    zeros = jnp.zeros((CONV_PAD, LANES), jnp.float32)
    pad_sc[0:CONV_PAD, :] = zeros
    pad_sc[CONV_PAD:CONV_PAD + T, :] = x_ref[0]
    pad_sc[CONV_PAD + T:2 * CONV_PAD + T, :] = zeros
    w = w_ref[...]
    for r0 in range(0, T, ROW_BLOCK):
        y = jnp.zeros((ROW_BLOCK, LANES), jnp.float32)
        for j in range(GDN_CONV):
            s = CONV_PAD + r0 + j - half
            y = y + w[j:j + 1, :] * pad_sc[s:s + ROW_BLOCK, :]
        y = y * jax.nn.sigmoid(y)
        if unit_norm:
            y = y * lax.rsqrt(jnp.sum(y * y, axis=-1, keepdims=True) + NORM_EPS)
        dst_sc[r0:r0 + ROW_BLOCK, :] = y


def _nt(a, b):
    return lax.dot_general(a, b, (((1,), (1,)), ((), ())), preferred_element_type=jnp.float32)


def _gdn_segment(T, h, refs, sc, s_init):
    pq_ref, pk_ref, pv_ref, cols_ref, rows_ref, za_ref, zb_ref, y_ref = refs
    (cwq_ref, cwk_ref, cwv_ref, gn_ref, pad_sc, q_sc, k_sc, v_sc, u_sc, w_sc, qd_sc, kdt_sc, attn_sc, gend_sc,
     o_sc, grow_sc) = sc
    C, P, H = GDN_CHUNK, GDN_PAIR, GDN_HEADS
    n_pairs = T // P
    bf16 = jnp.bfloat16
    scale = GDN_HEAD_DIM ** -0.5
    for d in range(2):
        grow_sc[d, 0:1, 0:T] = rows_ref[0, pl.ds(d * H + h, 1), :]
    _conv_silu(pq_ref, cwq_ref, pad_sc, q_sc, T, True)
    _conv_silu(pk_ref, cwk_ref, pad_sc, k_sc, T, True)
    _conv_silu(pv_ref, cwv_ref, pad_sc, v_sc, T, False)

    lane = lax.broadcasted_iota(jnp.int32, (P, LANES), 1)
    ii = lax.broadcasted_iota(jnp.int32, (C, C), 0)
    jj = lax.broadcasted_iota(jnp.int32, (C, C), 1)

    halves = [slice(c * C, (c + 1) * C) for c in range(2)]

    G = GDN_PAIRS_PER_STEP
    n_steps = n_pairs // G

    def units(i):
        fwd = [(0, G * i + g) for g in range(G)]
        return fwd + [(1, n_pairs - 1 - m) for _, m in fwd]

    def mm(a, b):
        return jnp.dot(a, b, preferred_element_type=jnp.float32)


    def prep_stages(i):
        neg_a, rhs, dest, held = [], [], [], []
        for d, m in units(i):
            r0 = pl.multiple_of(m * P, P)
            q2 = q_sc[pl.ds(r0, P), :]
            k2 = k_sc[pl.ds(r0, P), :]
            v2 = v_sc[pl.ds(r0, P), :]
            cols = cols_ref[0, pl.ds(r0, P), :]
            qs = q2 * scale
            kh = [k2[sl].astype(bf16) for sl in halves]
            qk = [_nt(qs[sl].astype(bf16), kh[c]) for c, sl in enumerate(halves)]
            gcol = jnp.sum(jnp.where(lane == d * H + h, cols, 0.0), axis=1, keepdims=True)
            bcol = jnp.sum(jnp.where(lane == (2 + d) * H + h, cols, 0.0), axis=1, keepdims=True)
            eg = jnp.exp(gcol)
            kb = k2 * bcol
            kk = [_nt(kb[sl].astype(bf16), kh[c]) for c, sl in enumerate(halves)]
            held.append((d, m, r0, k2, v2, qs, qk, kk, gcol, bcol, eg, kb))
        yield
        for d, m, r0, k2, v2, qs, qk, kk, gcol, bcol, eg, kb in held:
            grow = grow_sc[d, 0:1, pl.ds(r0, P)]
            keep = (ii >= jj) if d == 0 else (ii <= jj)
            strict = (ii > jj) if d == 0 else (ii < jj)
            rhs_uw = jnp.concatenate([v2 * bcol, kb * eg], axis=1)
            qd_sc[d, pl.ds(r0, P), :] = (qs * eg).astype(bf16)
            kd = []
            for c, sl in enumerate(halves):
                rc = pl.multiple_of(r0 + c * C, C)
                gc = gcol[sl]
                decay = jnp.exp(jnp.where(keep, gc - grow[:, sl], -jnp.inf))
                neg_a.append(jnp.where(strict, -(kk[c] * decay), 0.0))
                rhs.append(rhs_uw[sl])
                dest.append((d, rc))
                attn_sc[d, pl.ds(rc, C), :] = (qk[c] * decay).astype(bf16)
                g_last = gc[C - 1:C] if d == 0 else gc[0:1]
                kd.append(k2[sl] * jnp.exp(g_last - gc))
                gend_sc[d, pl.ds(pl.multiple_of((2 * m + c) * 8, 8), 8), :] = jnp.broadcast_to(
                    jnp.exp(g_last), (8, LANES))
            kdt_sc[d, :, pl.ds(r0, P)] = jnp.concatenate(kd, axis=0).T.astype(bf16)
        levels = C.bit_length() - 2
        e = list(neg_a)
        nb = [x.astype(bf16) for x in neg_a]
        p = [mm(x, x) for x in nb]
        yield
        for k in range(levels):
            pb = [x.astype(bf16) for x in p]
            ep = [mm(a.astype(bf16), b) for a, b in zip(e, pb)]
            p_next = [mm(b, b) for b in pb] if k + 1 < levels else None
            yield
            e = [a + b + ab for a, b, ab in zip(e, p, ep)]
            p = p_next
        corr = [mm(a.astype(bf16), r.astype(bf16)) for a, r in zip(e, rhs)]
        yield
        for (d, rc), r, cr in zip(dest, rhs, corr):
            s = r + cr
            u_sc[d, pl.ds(rc, C), :] = s[:, :GDN_HEAD_DIM]
            w_sc[d, pl.ds(rc, C), :] = s[:, GDN_HEAD_DIM:].astype(bf16)

    def scan_operands(i):
        ops = ([], [])
        for d, m in units(i):
            r0 = pl.multiple_of(m * P, P)
            kdt = kdt_sc[d, :, pl.ds(r0, P)]
            for c in ((0, 1) if d == 0 else (1, 0)):
                rc = pl.multiple_of(r0 + c * C, C)
                g_end = gend_sc[d, pl.ds(pl.multiple_of((2 * m + c) * 8, 8), 8), :][0:1]
                ops[d].append((d, rc, u_sc[d, pl.ds(rc, C), :], w_sc[d, pl.ds(rc, C), :], qd_sc[d, pl.ds(rc, C), :],
                               attn_sc[d, pl.ds(rc, C), :], kdt[:, c * C:(c + 1) * C], g_end))
        return ops

    def scan_stages(ops, carry, result):
        s = list(carry)
        for k in range(2 * G):
            cur = [ops[d][k] for d in range(2)]
            sb = [s[d].astype(bf16) for d in range(2)]
            ws = [mm(cur[d][3], sb[d]) for d in range(2)]
            qs_ = [mm(cur[d][4], sb[d]) for d in range(2)]
            yield
            vb = [(cur[d][2] - ws[d]).astype(bf16) for d in range(2)]
            av = [mm(cur[d][5], vb[d]) for d in range(2)]
            kv = [mm(cur[d][6], vb[d]) for d in range(2)]
            yield
            for d in range(2):
                o_sc[d, pl.ds(cur[d][1], C), :] = qs_[d] + av[d]
                s[d] = s[d] * cur[d][7] + kv[d]
        result.append(tuple(s))

    def run_interleaved(*gens):
        gens = list(gens)
        while gens:
            for g in list(gens):
                try:
                    next(g)
                except StopIteration:
                    gens.remove(g)

    def step(i, carry):
        result = []
        run_interleaved(prep_stages(i + 1), scan_stages(scan_operands(i), carry, result))
        return result[0]

    run_interleaved(prep_stages(0))
    carry = lax.fori_loop(0, n_steps - 1, step, s_init)
    result = []
    run_interleaved(scan_stages(scan_operands(n_steps - 1), carry, result))
    s_f, s_b = result[0]

    lane_b = lax.broadcasted_iota(jnp.int32, (ROW_BLOCK, LANES), 1)
    for r0 in range(0, T, ROW_BLOCK):
        o = o_sc[0, r0:r0 + ROW_BLOCK, :] + o_sc[1, r0:r0 + ROW_BLOCK, :]
        z = pltpu.roll(jnp.where(lane_b >= LANES - Z_SHIFT, za_ref[0, r0:r0 + ROW_BLOCK, :],
                                 zb_ref[0, r0:r0 + ROW_BLOCK, :]), Z_SHIFT, 1)
        y = o * lax.rsqrt(jnp.mean(o * o, axis=-1, keepdims=True) + NORM_EPS) * gn_ref[...]
        y_ref[0, r0:r0 + ROW_BLOCK, :] = (y * (z * jax.nn.sigmoid(z))).astype(y_ref.dtype)
    return s_f, s_b


def _gdn_body(Tc, Tl, *refs):
    h = pl.program_id(1)
    ctx_refs = refs[0:7]
    lat_refs = refs[7:14]
    shared = refs[14:18]
    y_ctx_ref, y_lat_ref = refs[18:20]
    sc = shared + refs[20:]
    zero = jnp.zeros((GDN_HEAD_DIM, GDN_HEAD_DIM), jnp.float32)
    states = _gdn_segment(Tc, h, ctx_refs + (y_ctx_ref,), sc, (zero, zero))
    _gdn_segment(Tl, h, lat_refs + (y_lat_ref,), sc, states)


def gdn_mix(p_ctx, gates_ctx, p_lat, gates_lat, conv_w, gdn_norm):
    B, Tc, _ = p_ctx.shape
    Tl = p_lat.shape[1]
    H, dh = GDN_HEADS, GDN_HEAD_DIM

    def seg_specs(T):
        return [pl.BlockSpec((1, T, dh), lambda b, h: (b, 0, h)),
                pl.BlockSpec((1, T, dh), lambda b, h: (b, 0, H + h)),
                pl.BlockSpec((1, T, dh), lambda b, h: (b, 0, 2 * H + h)),
                pl.BlockSpec((1, T, LANES), lambda b, h: (b, 0, 0)),
                pl.BlockSpec((1, LANES, T), lambda b, h: (b, 0, 0)),
                pl.BlockSpec((1, T, LANES), lambda b, h: (b, 0, Z_TILE0 + h)),
                pl.BlockSpec((1, T, LANES), lambda b, h: (b, 0, Z_TILE0 + 1 + h))]

    shared_specs = [pl.BlockSpec((GDN_CONV, dh), lambda b, h: (0, h)),
                    pl.BlockSpec((GDN_CONV, dh), lambda b, h: (0, H + h)),
                    pl.BlockSpec((GDN_CONV, dh), lambda b, h: (0, 2 * H + h)),
                    pl.BlockSpec((1, dh), lambda b, h: (0, 0))]
    rows_c, cols_c = gates_ctx
    rows_l, cols_l = gates_lat
    f32, bf16 = jnp.float32, jnp.bfloat16
    scratch = [pltpu.VMEM((Tl + 2 * CONV_PAD, LANES), f32),
               pltpu.VMEM((Tl, dh), f32), pltpu.VMEM((Tl, dh), f32), pltpu.VMEM((Tl, dh), f32),
               pltpu.VMEM((2, Tl, dh), f32), pltpu.VMEM((2, Tl, dh), bf16), pltpu.VMEM((2, Tl, dh), bf16),
               pltpu.VMEM((2, dh, Tl), bf16), pltpu.VMEM((2, Tl, GDN_CHUNK), bf16),
               pltpu.VMEM((2, Tl // GDN_CHUNK * 8, LANES), f32), pltpu.VMEM((2, Tl, dh), f32),
               pltpu.VMEM((2, 8, Tl), f32)]
    return pl.pallas_call(
        functools.partial(_gdn_body, Tc, Tl),
        grid=(B, H),
        in_specs=seg_specs(Tc) + seg_specs(Tl) + shared_specs,
        out_specs=[pl.BlockSpec((1, Tc, dh), lambda b, h: (b, 0, h)),
                   pl.BlockSpec((1, Tl, dh), lambda b, h: (b, 0, h))],
        out_shape=[jax.ShapeDtypeStruct((B, Tc, GDN_WIDTH), bf16), jax.ShapeDtypeStruct((B, Tl, GDN_WIDTH), bf16)],
        scratch_shapes=scratch,
        compiler_params=_cparams("arbitrary", "arbitrary"), name="gdn_mix",
    )(p_ctx, p_ctx, p_ctx, cols_c, rows_c, p_ctx, p_ctx,
      p_lat, p_lat, p_lat, cols_l, rows_l, p_lat, p_lat,
      conv_w, conv_w, conv_w, gdn_norm.reshape(1, dh))


def _pool_body(x_ref, w_ref, sc_ref, y_ref, pad_sc):
    T = x_ref.shape[1]
    G = POOL_GROUP
    zeros = jnp.zeros((POOL_PAD, G), jnp.float32)
    t = lax.broadcasted_iota(jnp.int32, (ROW_BLOCK, 1), 0)
    for i, win in enumerate(POOL_WINDOWS):
        half = win // 2
        lo = (STATE_W + i * G) // LANES * LANES
        off = STATE_W + i * G - lo
        pad_sc[0:POOL_PAD, :] = zeros
        pad_sc[POOL_PAD:POOL_PAD + T, :] = x_ref[0, :, lo:lo + G + LANES][:, off:off + G]
        pad_sc[POOL_PAD + T:2 * POOL_PAD + T, :] = zeros
        wb = w_ref[i].astype(jnp.bfloat16)
        for r0 in range(0, T, ROW_BLOCK):
            tot = jnp.zeros((ROW_BLOCK, G), jnp.float32)
            for s in range(-half, half):
                tot = tot + pad_sc[POOL_PAD + r0 + s:POOL_PAD + r0 + s + ROW_BLOCK, :]
            cnt = (jnp.minimum(t + r0 + half, T) - jnp.maximum(t + r0 - half, 0)).astype(jnp.float32)
            centred = tot / cnt - pad_sc[POOL_PAD + r0:POOL_PAD + r0 + ROW_BLOCK, :]
            y = jnp.dot(centred.astype(jnp.bfloat16), wb, preferred_element_type=jnp.float32)
            y_ref[0, r0:r0 + ROW_BLOCK, i * G:(i + 1) * G] = (y * sc_ref[:, i * G:(i + 1) * G]).astype(y_ref.dtype)


def pool_mix(p, pool_w, pool_scale):
    B, T, _ = p.shape
    n = len(POOL_WINDOWS)
    assert QKV_COLS % POOL_IN_W == 0 and STATE_W + POOL_WIDTH <= POOL_IN_W
    return pl.pallas_call(
        _pool_body,
        grid=(B,),
        in_specs=[pl.BlockSpec((1, T, POOL_IN_W), lambda b: (b, 0, QKV_COLS // POOL_IN_W)),
                  pl.BlockSpec((n, POOL_GROUP, POOL_GROUP), lambda b: (0, 0, 0)),
                  pl.BlockSpec((1, POOL_WIDTH), lambda b: (0, 0))],
        out_specs=pl.BlockSpec((1, T, POOL_WIDTH), lambda b: (b, 0, 0)),
        out_shape=jax.ShapeDtypeStruct((B, T, POOL_WIDTH), jnp.bfloat16),
        scratch_shapes=[pltpu.VMEM((T + 2 * POOL_PAD, POOL_GROUP), jnp.float32)],
        compiler_params=_cparams("arbitrary"), name="pool_mix",
    )(p, pool_w, pool_scale.reshape(1, POOL_WIDTH))


def pool_gdn_mixer(h_ctx, h_lat, w_in, j, pool_w, pool_scale, conv_w, a_log_f, a_log_b, dt_bias_f, dt_bias_b,
                   gdn_norm):
    B, Tl, D = h_lat.shape
    Tc = h_ctx.shape[1]
    p_lat = proj(h_lat.reshape(B * Tl, D), w_in, j, ncols=IN_COLS_PADDED, tm=1024).reshape(B, Tl, IN_COLS_PADDED)
    p_ctx = proj(h_ctx.reshape(B * Tc, D), w_in, j, ncols=IN_COLS_PADDED, tm=1024).reshape(B, Tc, IN_COLS_PADDED)
    gates_ctx = gdn_gates(p_ctx, a_log_f, a_log_b, dt_bias_f, dt_bias_b)
    gates_lat = gdn_gates(p_lat, a_log_f, a_log_b, dt_bias_f, dt_bias_b)
    g_ctx, g_lat = gdn_mix(p_ctx, gates_ctx, p_lat, gates_lat, conv_w, gdn_norm)
    return [pool_mix(p_ctx, pool_w, pool_scale), g_ctx], [pool_mix(p_lat, pool_w, pool_scale), g_lat]


def axial_rope_tables(rows):
    pos_row = jnp.repeat(jnp.arange(rows), GRID_W).astype(jnp.float32)
    pos_col = jnp.tile(jnp.arange(GRID_W), rows).astype(jnp.float32)
    quarter = DIFF_HEAD_DIM // 4
    inv_freq = ROPE_BASE ** (-jnp.arange(quarter, dtype=jnp.float32) / quarter)
    ang_r = pos_row[:, None] * inv_freq[None, :]
    ang_c = pos_col[:, None] * inv_freq[None, :]
    ang = jnp.concatenate([ang_r, ang_r, ang_c, ang_c], axis=-1)
    return jnp.cos(ang), jnp.sin(ang)


def _rope(x, cos, sin_signed):
    d = x.shape[-1]
    first = (lax.broadcasted_iota(jnp.int32, x.shape, 1) & (d // 2 - 1)) < d // 4
    partner = jnp.where(first, pltpu.roll(x, d - d // 4, 1), pltpu.roll(x, d // 4, 1))
    return x * cos + partner * sin_signed


def _diff_attn_body(lambda_init, tq, q_ref, kl_ref, vl_ref, kc_ref, vc_ref, cos_ref, sin_ref, lam_ref, g_ref,
                    o_ref, k_sc, v_sc):
    d = DIFF_HEAD_DIM
    L = kl_ref.shape[1]
    scale = d ** -0.5
    lv = lam_ref[...]
    lam = (jnp.exp(jnp.sum(lv[0:1] * lv[1:2], axis=1, keepdims=True))
           - jnp.exp(jnp.sum(lv[2:3] * lv[3:4], axis=1, keepdims=True)) + lambda_init)
    cos = cos_ref[...]
    sin = sin_ref[...]
    for m in range(2):
        k_sc[0:L, m * d:(m + 1) * d] = _rope(kl_ref[0, :, m * d:(m + 1) * d], cos, sin).astype(jnp.bfloat16)
    k_sc[L:, :] = kc_ref[0].astype(jnp.bfloat16)
    v_sc[0:L, :] = vl_ref[0].astype(jnp.bfloat16)
    v_sc[L:, :] = vc_ref[0].astype(jnp.bfloat16)

    def scores(r0, m):
        q = _rope(q_ref[0, pl.ds(r0, tq), m * d:(m + 1) * d], cos_ref[pl.ds(r0, tq), :],
                  sin_ref[pl.ds(r0, tq), :]).astype(jnp.bfloat16)
        return lax.dot_general(q, k_sc[:, m * d:(m + 1) * d], (((1,), (1,)), ((), ())),
                               preferred_element_type=jnp.float32)

    def finish(r0, s_pair):
        probs = []
        for s in s_pair:
            e = jnp.exp2((s - jnp.max(s, axis=-1, keepdims=True)) * (scale * math.log2(math.e)))
            probs.append(e * (1.0 / jnp.sum(e, axis=-1, keepdims=True)))
        a = (probs[0] - lam * probs[1]).astype(jnp.bfloat16)
        o = jnp.dot(a, v_sc[...], preferred_element_type=jnp.float32)
        o = o * lax.rsqrt(jnp.mean(o * o, axis=-1, keepdims=True) + NORM_EPS) * g_ref[...]
        o_ref[0, pl.ds(r0, tq), :] = (o * (1 - lambda_init)).astype(o_ref.dtype)

    def q_tile_pair(t, carry):
        ra = pl.multiple_of(2 * t * tq, tq)
        rb = pl.multiple_of(ra + tq, tq)
        sa = [scores(ra, m) for m in range(2)]
        sb = [scores(rb, m) for m in range(2)]
        finish(ra, sa)
        finish(rb, sb)
        return carry

    lax.fori_loop(0, L // (2 * tq), q_tile_pair, 0)


def diff_attention_lat(qkv_lat, kv_ctx, cos, sin_signed, lam_vecs, subln, lambda_init, tq=128):
    B, L, _ = qkv_lat.shape
    Lc = kv_ctx.shape[1]
    H, hd = DIFF_HEADS, 2 * DIFF_HEAD_DIM
    tq = min(tq, L // 2)
    assert L % (2 * tq) == 0
    return pl.pallas_call(
        functools.partial(_diff_attn_body, lambda_init, tq),
        grid=(B, H),
        in_specs=[pl.BlockSpec((1, L, hd), lambda b, h: (b, 0, h)),
                  pl.BlockSpec((1, L, hd), lambda b, h: (b, 0, H + h)),
                  pl.BlockSpec((1, L, hd), lambda b, h: (b, 0, 2 * H + h)),
                  pl.BlockSpec((1, Lc, hd), lambda b, h: (b, 0, h)),
                  pl.BlockSpec((1, Lc, hd), lambda b, h: (b, 0, H + h)),
                  pl.BlockSpec((L, DIFF_HEAD_DIM), lambda b, h: (0, 0)),
                  pl.BlockSpec((L, DIFF_HEAD_DIM), lambda b, h: (0, 0)),
                  pl.BlockSpec((4, DIFF_HEAD_DIM), lambda b, h: (0, 0)),
                  pl.BlockSpec((1, hd), lambda b, h: (0, 0))],
        out_specs=pl.BlockSpec((1, L, hd), lambda b, h: (b, 0, h)),
        out_shape=jax.ShapeDtypeStruct((B, L, H * hd), jnp.bfloat16),
        scratch_shapes=[pltpu.VMEM((L + Lc, hd), jnp.bfloat16), pltpu.VMEM((L + Lc, hd), jnp.bfloat16)],
        compiler_params=_cparams("arbitrary", "arbitrary"), name="diff_attention",
    )(qkv_lat, qkv_lat, qkv_lat, kv_ctx, kv_ctx, cos, sin_signed, lam_vecs, subln.reshape(1, hd))


def diff_attn_mixer(h_ctx, h_lat, w_qkv, j, lam_q1, lam_k1, lam_q2, lam_k2, subln, lambda_init, cos, sin):
    B, L, D = h_lat.shape
    Lc = h_ctx.shape[1]
    qkv = proj(h_lat.reshape(B * L, D), w_qkv, j, tm=1024).reshape(B, L, 3 * D)
    kv_ctx = proj(h_ctx.reshape(B * Lc, D), w_qkv, j, col0=D, tm=1024).reshape(B, Lc, 2 * D)
    quarter = DIFF_HEAD_DIM // 4
    first = (jnp.arange(DIFF_HEAD_DIM) % (2 * quarter)) < quarter
    sin_signed = jnp.where(first[None, :], -sin, sin)
    lam_vecs = jnp.stack([lam_q1, lam_k1, lam_q2, lam_k2]).astype(jnp.float32)
    return diff_attention_lat(qkv, kv_ctx, cos, sin_signed, lam_vecs, subln, lambda_init)


def _count_before(mask):
    R, L = mask.shape
    W = PREFIX_BLOCK
    m = mask.astype(jnp.float32)
    before = (lax.broadcasted_iota(jnp.int32, (W, W), 0) < lax.broadcasted_iota(jnp.int32, (W, W), 1)
              ).astype(jnp.bfloat16)
    carry = jnp.zeros((R, 1), jnp.float32)
    out = []
    for c in range(L // W):
        blk = m[:, c * W:(c + 1) * W]
        out.append(jnp.dot(blk.astype(jnp.bfloat16), before, preferred_element_type=jnp.float32) + carry)
        carry = carry + jnp.sum(blk, axis=1, keepdims=True)
    return out[0] if len(out) == 1 else jnp.concatenate(out, axis=1)


def _select_body(cap, aff_ref, pos_e_ref, gate_e_ref, pos_t_ref):
    L = aff_ref.shape[1]
    a = aff_ref[0].T[:N_EXPERTS]
    bits = pltpu.bitcast(a, jnp.int32)
    thr = jnp.zeros((N_EXPERTS, 1), jnp.int32)
    for bit in range(30, -1, -1):
        cand = thr | (1 << bit)
        cnt = jnp.sum((bits >= cand).astype(jnp.float32), axis=1, keepdims=True)
        thr = jnp.where(cnt >= cap, cand, thr)
    gt = bits > thr
    eq = bits == thr
    need = cap - jnp.sum(gt.astype(jnp.float32), axis=1, keepdims=True)
    sel = gt | (eq & (_count_before(eq) < need))
    pos = jnp.where(sel, _count_before(sel).astype(jnp.int32), -1)
    pos_e_ref[0] = pos
    gate_e_ref[0] = jnp.where(sel, a, 0.0)
    pad = jnp.full((LANES - N_EXPERTS, L), -1, jnp.int32)
    pos_t_ref[0] = jnp.concatenate([pos, pad], axis=0).T


def ec_select(aff, cap):
    B, L, _ = aff.shape
    return pl.pallas_call(
        functools.partial(_select_body, cap),
        grid=(B,),
        in_specs=[pl.BlockSpec((1, L, LANES), lambda b: (b, 0, 0))],
        out_specs=[pl.BlockSpec((1, N_EXPERTS, L), lambda b: (b, 0, 0)),
                   pl.BlockSpec((1, N_EXPERTS, L), lambda b: (b, 0, 0)),
                   pl.BlockSpec((1, L, LANES), lambda b: (b, 0, 0))],
        out_shape=[jax.ShapeDtypeStruct((B, N_EXPERTS, L), jnp.int32),
                   jax.ShapeDtypeStruct((B, N_EXPERTS, L), jnp.float32),
                   jax.ShapeDtypeStruct((B, L, LANES), jnp.int32)],
        compiler_params=_cparams("arbitrary"), name="ec_select",
    )(aff)


def _gather_body(h_ref, pos_e_ref, gate_e_ref, xe_ref, gc_ref):
    e = pl.program_id(1)
    cap = xe_ref.shape[2]
    L = h_ref.shape[1]
    pos_row = pos_e_ref[0, pl.ds(e, 1), :]
    gate_row = gate_e_ref[0, pl.ds(e, 1), :]
    hit = lax.broadcasted_iota(jnp.int32, (cap, L), 0) == pos_row
    xe_ref[0, 0] = jnp.dot(hit.astype(jnp.bfloat16), h_ref[0],
                           preferred_element_type=jnp.float32).astype(jnp.bfloat16)
    gc_ref[0, 0] = jnp.sum(jnp.where(hit, gate_row, 0.0), axis=1, keepdims=True)


def ec_gather(h, pos_e, gate_e, cap):
    B, L, D = h.shape
    return pl.pallas_call(
        _gather_body,
        grid=(B, N_EXPERTS),
        in_specs=[pl.BlockSpec((1, L, D), lambda b, e: (b, 0, 0)),
                  pl.BlockSpec((1, N_EXPERTS, L), lambda b, e: (b, 0, 0)),
                  pl.BlockSpec((1, N_EXPERTS, L), lambda b, e: (b, 0, 0))],
        out_specs=[pl.BlockSpec((1, 1, cap, D), lambda b, e: (b, e, 0, 0)),
                   pl.BlockSpec((1, 1, cap, 1), lambda b, e: (b, e, 0, 0))],
        out_shape=[jax.ShapeDtypeStruct((B, N_EXPERTS, cap, D), jnp.bfloat16),
                   jax.ShapeDtypeStruct((B, N_EXPERTS, cap, 1), jnp.float32)],
        compiler_params=_cparams("arbitrary", "arbitrary"), name="ec_gather",
    )(h, pos_e, gate_e)


def _experts_body(nf, n_groups, *refs):
    xg = refs[:2 * n_groups]
    w1_ref, w3_ref, w2_ref = refs[2 * n_groups:2 * n_groups + 3]
    ye_refs = refs[2 * n_groups + 3:3 * n_groups + 3]
    hid_refs = refs[3 * n_groups + 3:]
    s = pl.program_id(1)

    @pl.when(s < nf)
    def _():
        w1b = w1_ref[0].astype(jnp.bfloat16)
        w3b = w3_ref[0].astype(jnp.bfloat16)
        for g in range(n_groups):
            xe_ref = xg[2 * g]
            B, _, cap, D = xe_ref.shape
            x = xe_ref[...].reshape(B * cap, D)
            h1 = jnp.dot(x, w1b, preferred_element_type=jnp.float32)
            h3 = jnp.dot(x, w3b, preferred_element_type=jnp.float32)
            hid_refs[g][s] = (h1 * jax.nn.sigmoid(h1) * h3).astype(jnp.bfloat16)

    @pl.when(s >= nf)
    def _():
        w2b = w2_ref[0].astype(jnp.bfloat16)
        for g in range(n_groups):
            hid_ref, gc_ref, ye_ref = hid_refs[g], xg[2 * g + 1], ye_refs[g]
            hid = jnp.concatenate([hid_ref[f] for f in range(nf)], axis=1) if nf > 1 else hid_ref[0]
            y = jnp.dot(hid, w2b, preferred_element_type=jnp.float32)
            y = y * gc_ref[...].reshape(y.shape[0], 1)
            ye_ref[...] = y.astype(jnp.bfloat16).reshape(ye_ref.shape)


def ec_experts(groups, w1, w3, w2, layer, tf=256, tn=1024):
    _, E, _, D = groups[0][0].shape
    F = w1.shape[3]
    tf = min(tf, F)
    tn = min(tn, D)
    nf, nn = F // tf, D // tn
    in_specs, args, out_specs, out_shape, scratch = [], [], [], [], []
    for xe, gc in groups:
        B, _, cap, _ = xe.shape
        in_specs += [pl.BlockSpec((B, 1, cap, D), lambda e, s: (0, e, 0, 0), pipeline_mode=pl.Buffered(1)),
                     pl.BlockSpec((B, 1, cap, 1), lambda e, s: (0, e, 0, 0))]
        args += [xe, gc]
        out_specs.append(pl.BlockSpec((B, 1, cap, tn), lambda e, s: (0, e, 0, jnp.maximum(s - nf, 0))))
        out_shape.append(jax.ShapeDtypeStruct((B, E, cap, D), jnp.bfloat16))
        scratch.append(pltpu.VMEM((nf, B * cap, tf), jnp.bfloat16))
    in_specs += [pl.BlockSpec((None, 1, D, tf), lambda e, s: (layer, e, 0, jnp.minimum(s, nf - 1))),
                 pl.BlockSpec((None, 1, D, tf), lambda e, s: (layer, e, 0, jnp.minimum(s, nf - 1))),
                 pl.BlockSpec((None, 1, F, tn), lambda e, s: (layer, e, 0, jnp.maximum(s - nf, 0)))]
    return pl.pallas_call(
        functools.partial(_experts_body, nf, len(groups)),
        grid=(E, nf + nn),
        in_specs=in_specs, out_specs=out_specs, out_shape=out_shape, scratch_shapes=scratch,
        compiler_params=_cparams("arbitrary", "arbitrary"), name="ec_experts",
    )(*args, w1, w3, w2)


def _combine_body(ctx_row, cap, pos_t_ref, ye_ref, x_ref, gate_ref, o_ref):
    row = pl.program_id(0) if ctx_row is None else ctx_row
    K = ye_ref.shape[1]
    assert cap & (cap - 1) == 0 and cap <= 256
    col_expert = lax.broadcasted_iota(jnp.int32, (LANES, K), 1) >> (cap.bit_length() - 1)
    spread = (col_expert == lax.broadcasted_iota(jnp.int32, (LANES, K), 0)).astype(jnp.bfloat16)
    pos = jnp.dot(pos_t_ref[0].astype(jnp.float32).astype(jnp.bfloat16), spread,
                  preferred_element_type=jnp.float32)
    col_slot = (lax.broadcasted_iota(jnp.int32, (1, K), 1) & (cap - 1)).astype(jnp.float32)
    hit = (pos == col_slot).astype(jnp.bfloat16)
    y = jnp.dot(hit, ye_ref[0], preferred_element_type=jnp.float32)
    o_ref[0] = x_ref[0] + _mod_row(gate_ref, row) * y


def ec_combine(x, pos_t, ye, mod, k_gate, cap, ctx_row=None, tt=512, tn=1024):
    B, L, D = x.shape
    tt = min(tt, L)
    tn = min(tn, D)
    K = N_EXPERTS * cap
    return pl.pallas_call(
        functools.partial(_combine_body, ctx_row, cap),
        grid=(B, D // tn, L // tt),
        in_specs=[pl.BlockSpec((1, tt, LANES), lambda b, n, t: (b, t, 0)),
                  pl.BlockSpec((1, K, tn), lambda b, n, t: (b, 0, n)),
                  pl.BlockSpec((1, tt, tn), lambda b, n, t: (b, t, n)),
                  pl.BlockSpec((MOD_ROWS, tn), lambda b, n, t: (0, k_gate * (D // tn) + n))],
        out_specs=pl.BlockSpec((1, tt, tn), lambda b, n, t: (b, t, n)),
        out_shape=jax.ShapeDtypeStruct((B, L, D), jnp.float32),
        compiler_params=_cparams("arbitrary", "arbitrary", "arbitrary"), name="ec_combine",
    )(pos_t, ye.reshape(B, K, D), x, mod)


def moe_blocks(streams, gain, mod, router, w1, w3, w2, layer):
    routed = []
    for x, ctx_row in streams:
        cap = EC_CAPACITY * x.shape[1] // N_EXPERTS
        h, aff = normmod(x, gain, mod, 3, 4, ctx_row=ctx_row, router=router)
        pos_e, gate_e, pos_t = ec_select(aff, cap)
        routed.append((ec_gather(h, pos_e, gate_e, cap), pos_t, cap))
    ye = ec_experts([r[0] for r in routed], w1, w3, w2, layer)
    return [ec_combine(x, pos_t, y, mod, 5, cap, ctx_row=ctx_row)
            for (x, ctx_row), (_, pos_t, cap), y in zip(streams, routed, ye)]


def kernel(x, c, ctx, c_ctx, ada_w, ada_b, norm_mix, norm_ffn, w_in, pool_w, pool_scale, conv_w, a_log_f, a_log_b, dt_bias_f, dt_bias_b, gdn_norm, w_out_ab, w_qkv, lam_q1, lam_k1, lam_q2, lam_k2, subln, w_out_c, router, w1, w3, w2, final_norm):
    depth = ada_w.shape[0]
    B, L, D = x.shape
    Lc = ctx.shape[1]
    assert B < MOD_ROWS
    ctx_row = B
    cos, sin = axial_rope_tables(L // GRID_W)
    cond = jnp.zeros((MOD_ROWS, D), jnp.float32).at[:B].set(jax.nn.silu(c)).at[ctx_row].set(jax.nn.silu(c_ctx))
    cond = cond.astype(jnp.bfloat16)
    x_lat, x_ctx = x, ctx
    for layer in range(depth):
        last = layer == depth - 1
        j = layer // 2
        mod = proj(cond, ada_w, layer) + ada_b[layer]
        h_lat = normmod(x_lat, norm_mix[layer], mod, 0, 1)
        h_ctx = normmod(x_ctx, norm_mix[layer], mod, 0, 1, ctx_row=ctx_row)
        if layer % 2 == 0:
            y_ctx, y_lat = pool_gdn_mixer(h_ctx, h_lat, w_in, j, pool_w[j], pool_scale[j], conv_w[j],
                                          a_log_f[j], a_log_b[j], dt_bias_f[j], dt_bias_b[j], gdn_norm[j])
            w_out = w_out_ab
        else:
            assert last, "context outputs of an attention layer are only skipped when no later layer reads them"
            lambda_init = 0.8 - 0.6 * math.exp(-0.3 * layer)
            y_ctx = None
            y_lat = [diff_attn_mixer(h_ctx, h_lat, w_qkv, j, lam_q1[j], lam_k1[j], lam_q2[j], lam_k2[j],
                                     subln[j], lambda_init, cos, sin)]
            w_out = w_out_c
        x_lat = proj([p.reshape(B * L, -1) for p in y_lat], w_out, j, tm=1024,
                     residual=(x_lat.reshape(B * L, D), mod, 2, L, None)).reshape(B, L, D)
        streams = [(x_lat, None)]
        if not last:
            x_ctx = proj([p.reshape(B * Lc, -1) for p in y_ctx], w_out, j, tm=1024,
                         residual=(x_ctx.reshape(B * Lc, D), mod, 2, Lc, ctx_row)).reshape(B, Lc, D)
            streams.append((x_ctx, ctx_row))
        outs = moe_blocks(streams, norm_ffn[layer], mod, router[layer], w1, w3, w2, layer)
        x_lat = outs[0]
        if not last:
            x_ctx = outs[1]
    return final_rmsnorm(x_lat, final_norm)
```

```python
import functools
import math

import jax
import jax.numpy as jnp
from jax import lax
from jax.experimental import pallas as pl
from jax.experimental.pallas import tpu as pltpu

D_MODEL = 4096
GRID_W = 64
NORM_EPS = 1e-6
POOL_WINDOWS = (2, 4, 8, 16)
POOL_GROUP = D_MODEL // 16
POOL_WIDTH = len(POOL_WINDOWS) * POOL_GROUP
GDN_HEAD_DIM = 128
GDN_HEADS = (D_MODEL - POOL_WIDTH) // GDN_HEAD_DIM
GDN_WIDTH = GDN_HEADS * GDN_HEAD_DIM
GDN_CONV = 5
GDN_CHUNK = 64
QKV_COLS = 3 * GDN_WIDTH
DIFF_HEAD_DIM = 128
DIFF_HEADS = D_MODEL // (2 * DIFF_HEAD_DIM)
ROPE_BASE = 10000.0
N_EXPERTS = 16
EC_CAPACITY = 2

LANES = 128
MOD_ROWS = 8
PREFIX_BLOCK = 256
VMEM_LIMIT_BYTES = 56 * 1024 * 1024

PROJ_TN = 512
GDN_PAIR = 2 * GDN_CHUNK
GDN_PAIRS_PER_STEP = 2
CONV_PAD = 8
ROW_BLOCK = 128
STATE_W = 4 * GDN_HEADS
REST_TILE0 = QKV_COLS // LANES
IN_COLS_PADDED = -(-(QKV_COLS + STATE_W + POOL_WIDTH + GDN_WIDTH) // PROJ_TN) * PROJ_TN
Z_TILE0 = REST_TILE0 + (STATE_W + POOL_WIDTH) // LANES
Z_SHIFT = LANES - (STATE_W + POOL_WIDTH) % LANES
POOL_PAD = 16
POOL_IN_W = 1152


def _cparams(*sem):
    return pltpu.CompilerParams(dimension_semantics=sem, vmem_limit_bytes=VMEM_LIMIT_BYTES)


def _mod_row(mod_ref, row):
    return mod_ref[pl.ds(row, 1), :]


def _proj_body(n_parts, rows_per_sample, ctx_row, valid_cols, w_out_major, *refs):
    a_refs, w_ref = refs[:n_parts], refs[n_parts]
    if rows_per_sample is None:
        o_ref, wb_ref = refs[n_parts + 1:]
    else:
        x_ref, gate_ref, o_ref, wb_ref = refs[n_parts + 1:]
    i = pl.program_id(1)
    out_axis = 0 if w_out_major else 1

    @pl.when(i == 0)
    def _():
        w = w_ref[...]
        if valid_cols is not None:
            col = pl.program_id(0) * w.shape[out_axis] + lax.broadcasted_iota(jnp.int32, w.shape, out_axis)
            w = jnp.where(col < valid_cols, w, 0.0)
        wb_ref[...] = w.astype(jnp.bfloat16)

    y, k0 = None, 0
    for a_ref in a_refs:
        k1 = k0 + a_ref.shape[1]
        if w_out_major:
            part = _nt(a_ref[...], wb_ref[:, k0:k1])
        else:
            part = jnp.dot(a_ref[...], wb_ref[k0:k1, :], preferred_element_type=jnp.float32)
        y = part if y is None else y + part
        k0 = k1
    if rows_per_sample is None:
        o_ref[...] = y.astype(o_ref.dtype)
    else:
        row = (i * o_ref.shape[0]) // rows_per_sample if ctx_row is None else ctx_row
        o_ref[...] = x_ref[...] + _mod_row(gate_ref, row) * y


def proj(a, w, layer, *, col0=0, ncols=None, tm=512, tn=PROJ_TN, out_dtype=jnp.float32, residual=None,
         w_out_major=False):
    parts = list(a) if isinstance(a, (list, tuple)) else [a]
    M = parts[0].shape[0]
    K = sum(p.shape[1] for p in parts)
    n_out = w.shape[1] if w_out_major else w.shape[2]
    ncols = n_out - col0 if ncols is None else ncols
    tm = min(tm, M)
    assert M % tm == 0 and ncols % tn == 0 and col0 % tn == 0, (M, ncols, col0, tm, tn)
    j0 = col0 // tn
    valid_cols = n_out - col0 if col0 + ncols > n_out else None
    in_specs = [pl.BlockSpec((tm, p.shape[1]), lambda j, i: (i, 0)) for p in parts]
    if w_out_major:
        in_specs.append(pl.BlockSpec((None, tn, K), lambda j, i: (layer, j0 + j, 0)))
    else:
        in_specs.append(pl.BlockSpec((None, K, tn), lambda j, i: (layer, 0, j0 + j)))
    args = parts + [w]
    rows_per_sample = ctx_row = None
    if residual is not None:
        x, mod, k_gate, rows_per_sample, ctx_row = residual
        assert ctx_row is not None or rows_per_sample % tm == 0
        in_specs += [pl.BlockSpec((tm, tn), lambda j, i: (i, j)),
                     pl.BlockSpec((MOD_ROWS, tn), lambda j, i: (0, k_gate * (ncols // tn) + j))]
        args += [x, mod]
    return pl.pallas_call(
        functools.partial(_proj_body, len(parts), rows_per_sample, ctx_row, valid_cols, w_out_major),
        grid=(ncols // tn, M // tm),
        in_specs=in_specs,
        out_specs=pl.BlockSpec((tm, tn), lambda j, i: (i, j)),
        out_shape=jax.ShapeDtypeStruct((M, ncols), out_dtype),
        scratch_shapes=[pltpu.VMEM((tn, K) if w_out_major else (K, tn), jnp.bfloat16)],
        compiler_params=_cparams("arbitrary", "arbitrary"), name="proj",
    )(*args)


def _normmod_body(ctx_row, with_router, x_ref, g_ref, shift_ref, scale_ref, *rest):
    row = pl.program_id(0) if ctx_row is None else ctx_row
    x = x_ref[0]
    y = x * lax.rsqrt(jnp.mean(x * x, axis=-1, keepdims=True) + NORM_EPS) * g_ref[...]
    h = (y * (1.0 + _mod_row(scale_ref, row)) + _mod_row(shift_ref, row)).astype(jnp.bfloat16)
    if not with_router:
        (h_ref,) = rest
        h_ref[0] = h
        return
    r_ref, h_ref, aff_ref = rest
    h_ref[0] = h
    logits = jnp.dot(h, r_ref[...], preferred_element_type=jnp.float32)
    lane = lax.broadcasted_iota(jnp.int32, logits.shape, 1)
    logits = jnp.where(lane < N_EXPERTS, logits, -jnp.inf)
    e = jnp.exp(logits - jnp.max(logits, axis=-1, keepdims=True))
    aff_ref[0] = e / jnp.sum(e, axis=-1, keepdims=True)


def normmod(x, gain, mod, k_shift, k_scale, ctx_row=None, router=None, tr=256):
    B, L, D = x.shape
    tr = min(tr, L)
    in_specs = [pl.BlockSpec((1, tr, D), lambda b, t: (b, t, 0)),
                pl.BlockSpec((1, D), lambda b, t: (0, 0)),
                pl.BlockSpec((MOD_ROWS, D), lambda b, t: (0, k_shift)),
                pl.BlockSpec((MOD_ROWS, D), lambda b, t: (0, k_scale))]
    args = [x, gain.reshape(1, D), mod, mod]
    out_specs = [pl.BlockSpec((1, tr, D), lambda b, t: (b, t, 0))]
    out_shape = [jax.ShapeDtypeStruct((B, L, D), jnp.bfloat16)]
    if router is not None:
        rp = jnp.pad(router.astype(jnp.bfloat16), ((0, 0), (0, LANES - router.shape[1])))
        in_specs.append(pl.BlockSpec((D, LANES), lambda b, t: (0, 0)))
        args.append(rp)
        out_specs.append(pl.BlockSpec((1, tr, LANES), lambda b, t: (b, t, 0)))
        out_shape.append(jax.ShapeDtypeStruct((B, L, LANES), jnp.float32))
    out = pl.pallas_call(
        functools.partial(_normmod_body, ctx_row, router is not None),
        grid=(B, L // tr), in_specs=in_specs, out_specs=out_specs, out_shape=out_shape,
        compiler_params=_cparams("arbitrary", "arbitrary"), name="normmod",
    )(*args)
    return out if router is not None else out[0]


def _final_norm_body(x_ref, g_ref, o_ref):
    x = x_ref[...]
    o_ref[...] = x * lax.rsqrt(jnp.mean(x * x, axis=-1, keepdims=True) + NORM_EPS) * g_ref[...]


def final_rmsnorm(x, gain, tr=512):
    B, L, D = x.shape
    M = B * L
    tr = min(tr, M)
    return pl.pallas_call(
        _final_norm_body,
        grid=(M // tr,),
        in_specs=[pl.BlockSpec((tr, D), lambda i: (i, 0)), pl.BlockSpec((1, D), lambda i: (0, 0))],
        out_specs=pl.BlockSpec((tr, D), lambda i: (i, 0)),
        out_shape=jax.ShapeDtypeStruct((M, D), x.dtype),
        compiler_params=_cparams("arbitrary"), name="final_rmsnorm",
    )(x.reshape(M, D), gain.reshape(1, D)).reshape(B, L, D)


def _gates_body(st_ref, alog_ref, dtb_ref, rows_ref, cols_ref):
    T = st_ref.shape[1]
    H = GDN_HEADS
    x = st_ref[0].T
    t = x + dtb_ref[...]
    softplus = jnp.maximum(t, 0.0) + jnp.log1p(jnp.exp(-jnp.abs(t)))
    r_full = lax.broadcasted_iota(jnp.int32, x.shape, 0)
    val = jnp.where(r_full < 2 * H, -jnp.exp(alog_ref[...]) * softplus, jax.nn.sigmoid(x))
    pos = lax.broadcasted_iota(jnp.int32, (LANES, LANES), 1) & (GDN_CHUNK - 1)
    r = lax.broadcasted_iota(jnp.int32, (LANES, LANES), 0)
    tiles = []
    for c in range(T // LANES):
        v = val[:, c * LANES:(c + 1) * LANES]
        pre, suf, s = v, v, 1
        while s < GDN_CHUNK:
            pre = pre + jnp.where(pos >= s, pltpu.roll(pre, s, 1), 0.0)
            suf = suf + jnp.where(pos < GDN_CHUNK - s, pltpu.roll(suf, LANES - s, 1), 0.0)
            s *= 2
        tiles.append(jnp.where(r < H, pre, jnp.where(r < 2 * H, suf, v)))
    out = jnp.concatenate(tiles, axis=1) if len(tiles) > 1 else tiles[0]
    rows_ref[0] = out
    cols_ref[0] = out.T


def gdn_gates(p, a_log_f, a_log_b, dt_bias_f, dt_bias_b):
    B, T, _ = p.shape
    pad = jnp.zeros((LANES - 2 * GDN_HEADS,), jnp.float32)
    alog = jnp.concatenate([a_log_f, a_log_b, pad]).astype(jnp.float32).reshape(LANES, 1)
    dtb = jnp.concatenate([dt_bias_f, dt_bias_b, pad]).astype(jnp.float32).reshape(LANES, 1)
    return pl.pallas_call(
        _gates_body,
        grid=(B,),
        in_specs=[pl.BlockSpec((1, T, LANES), lambda b: (b, 0, REST_TILE0)),
                  pl.BlockSpec((LANES, 1), lambda b: (0, 0)),
                  pl.BlockSpec((LANES, 1), lambda b: (0, 0))],
        out_specs=[pl.BlockSpec((1, LANES, T), lambda b: (b, 0, 0)),
                   pl.BlockSpec((1, T, LANES), lambda b: (b, 0, 0))],
        out_shape=[jax.ShapeDtypeStruct((B, LANES, T), jnp.float32),
                   jax.ShapeDtypeStruct((B, T, LANES), jnp.float32)],
        compiler_params=_cparams("arbitrary"), name="gdn_gates",
    )(p, alog, dtb)


def _conv_silu(x_ref, w_ref, pad_sc, dst_sc, T, unit_norm):
    half = GDN_CONV // 2
    zeros = jnp.zeros((CONV_PAD, LANES), jnp.float32)
    pad_sc[0:CONV_PAD, :] = zeros
    pad_sc[CONV_PAD:CONV_PAD + T, :] = x_ref[0]
    pad_sc[CONV_PAD + T:2 * CONV_PAD + T, :] = zeros
    w = w_ref[...]
    for r0 in range(0, T, ROW_BLOCK):
        y = jnp.zeros((ROW_BLOCK, LANES), jnp.float32)
        for j in range(GDN_CONV):
            s = CONV_PAD + r0 + j - half
            y = y + w[j:j + 1, :] * pad_sc[s:s + ROW_BLOCK, :]
        y = y * jax.nn.sigmoid(y)
        if unit_norm:
            y = y * lax.rsqrt(jnp.sum(y * y, axis=-1, keepdims=True) + NORM_EPS)
        dst_sc[r0:r0 + ROW_BLOCK, :] = y


def _nt(a, b):
    return lax.dot_general(a, b, (((1,), (1,)), ((), ())), preferred_element_type=jnp.float32)


def _gdn_segment(T, h, refs, sc, s_init):
    pq_ref, pk_ref, pv_ref, cols_ref, rows_ref, za_ref, zb_ref, y_ref = refs
    (cwq_ref, cwk_ref, cwv_ref, gn_ref, pad_sc, q_sc, k_sc, v_sc, u_sc, w_sc, qd_sc, kdt_sc, attn_sc, gend_sc,
     o_sc, grow_sc) = sc
    C, P, H = GDN_CHUNK, GDN_PAIR, GDN_HEADS
    n_pairs = T // P
    bf16 = jnp.bfloat16
    scale = GDN_HEAD_DIM ** -0.5
    for d in range(2):
        grow_sc[d, 0:1, 0:T] = rows_ref[0, pl.ds(d * H + h, 1), :]
    _conv_silu(pq_ref, cwq_ref, pad_sc, q_sc, T, True)
    _conv_silu(pk_ref, cwk_ref, pad_sc, k_sc, T, True)
    _conv_silu(pv_ref, cwv_ref, pad_sc, v_sc, T, False)

    lane = lax.broadcasted_iota(jnp.int32, (P, LANES), 1)
    ii = lax.broadcasted_iota(jnp.int32, (C, C), 0)
    jj = lax.broadcasted_iota(jnp.int32, (C, C), 1)

    halves = [slice(c * C, (c + 1) * C) for c in range(2)]

    G = min(GDN_PAIRS_PER_STEP, n_pairs)
    assert n_pairs % G == 0
    n_steps = n_pairs // G

    def units(i):
        fwd = [(0, G * i + g) for g in range(G)]
        return fwd + [(1, n_pairs - 1 - m) for _, m in fwd]

    def mm(a, b):
        return jnp.dot(a, b, preferred_element_type=jnp.float32)


    def prep_stages(i):
        neg_a, rhs, dest, held = [], [], [], []
        for d, m in units(i):
            r0 = pl.multiple_of(m * P, P)
            q2 = q_sc[pl.ds(r0, P), :]
            k2 = k_sc[pl.ds(r0, P), :]
            v2 = v_sc[pl.ds(r0, P), :]
            cols = cols_ref[0, pl.ds(r0, P), :]
            qs = q2 * scale
            kh = [k2[sl].astype(bf16) for sl in halves]
            qk = [_nt(qs[sl].astype(bf16), kh[c]) for c, sl in enumerate(halves)]
            gcol = jnp.sum(jnp.where(lane == d * H + h, cols, 0.0), axis=1, keepdims=True)
            bcol = jnp.sum(jnp.where(lane == (2 + d) * H + h, cols, 0.0), axis=1, keepdims=True)
            eg = jnp.exp(gcol)
            kb = k2 * bcol
            kk = [_nt(kb[sl].astype(bf16), kh[c]) for c, sl in enumerate(halves)]
            held.append((d, m, r0, k2, v2, qs, qk, kk, gcol, bcol, eg, kb))
        yield
        for d, m, r0, k2, v2, qs, qk, kk, gcol, bcol, eg, kb in held:
            grow = grow_sc[d, 0:1, pl.ds(r0, P)]
            keep = (ii >= jj) if d == 0 else (ii <= jj)
            strict = (ii > jj) if d == 0 else (ii < jj)
            rhs_uw = jnp.concatenate([v2 * bcol, kb * eg], axis=1)
            qd_sc[d, pl.ds(r0, P), :] = (qs * eg).astype(bf16)
            kd = []
            for c, sl in enumerate(halves):
                rc = pl.multiple_of(r0 + c * C, C)
                gc = gcol[sl]
                decay = jnp.exp(jnp.where(keep, gc - grow[:, sl], -jnp.inf))
                neg_a.append(jnp.where(strict, -(kk[c] * decay), 0.0))
                rhs.append(rhs_uw[sl])
                dest.append((d, rc))
                attn_sc[d, pl.ds(rc, C), :] = (qk[c] * decay).astype(bf16)
                g_last = gc[C - 1:C] if d == 0 else gc[0:1]
                kd.append(k2[sl] * jnp.exp(g_last - gc))
                gend_sc[d, pl.ds(pl.multiple_of((2 * m + c) * 8, 8), 8), :] = jnp.broadcast_to(
                    jnp.exp(g_last), (8, LANES))
            kdt_sc[d, :, pl.ds(r0, P)] = jnp.concatenate(kd, axis=0).T.astype(bf16)
        levels = C.bit_length() - 2
        e = list(neg_a)
        nb = [x.astype(bf16) for x in neg_a]
        p = [mm(x, x) for x in nb]
        yield
        for k in range(levels):
            pb = [x.astype(bf16) for x in p]
            ep = [mm(a.astype(bf16), b) for a, b in zip(e, pb)]
            p_next = [mm(b, b) for b in pb] if k + 1 < levels else None
            yield
            e = [a + b + ab for a, b, ab in zip(e, p, ep)]
            p = p_next
        corr = [mm(a.astype(bf16), r.astype(bf16)) for a, r in zip(e, rhs)]
        yield
        for (d, rc), r, cr in zip(dest, rhs, corr):
            s = r + cr
            u_sc[d, pl.ds(rc, C), :] = s[:, :GDN_HEAD_DIM]
            w_sc[d, pl.ds(rc, C), :] = s[:, GDN_HEAD_DIM:].astype(bf16)

    def scan_operands(i):
        ops = ([], [])
        for d, m in units(i):
            r0 = pl.multiple_of(m * P, P)
            kdt = kdt_sc[d, :, pl.ds(r0, P)]
            for c in ((0, 1) if d == 0 else (1, 0)):
                rc = pl.multiple_of(r0 + c * C, C)
                g_end = gend_sc[d, pl.ds(pl.multiple_of((2 * m + c) * 8, 8), 8), :][0:1]
                ops[d].append((d, rc, u_sc[d, pl.ds(rc, C), :], w_sc[d, pl.ds(rc, C), :], qd_sc[d, pl.ds(rc, C), :],
                               attn_sc[d, pl.ds(rc, C), :], kdt[:, c * C:(c + 1) * C], g_end))
        return ops

    def scan_stages(ops, carry, result):
        s = list(carry)
        for k in range(2 * G):
            cur = [ops[d][k] for d in range(2)]
            sb = [s[d].astype(bf16) for d in range(2)]
            ws = [mm(cur[d][3], sb[d]) for d in range(2)]
            qs_ = [mm(cur[d][4], sb[d]) for d in range(2)]
            yield
            vb = [(cur[d][2] - ws[d]).astype(bf16) for d in range(2)]
            av = [mm(cur[d][5], vb[d]) for d in range(2)]
            kv = [mm(cur[d][6], vb[d]) for d in range(2)]
            yield
            for d in range(2):
                o_sc[d, pl.ds(cur[d][1], C), :] = qs_[d] + av[d]
                s[d] = s[d] * cur[d][7] + kv[d]
        result.append(tuple(s))

    def run_interleaved(*gens):
        gens = list(gens)
        while gens:
            for g in list(gens):
                try:
                    next(g)
                except StopIteration:
                    gens.remove(g)

    def step(i, carry):
        result = []
        run_interleaved(prep_stages(i + 1), scan_stages(scan_operands(i), carry, result))
        return result[0]

    run_interleaved(prep_stages(0))
    carry = lax.fori_loop(0, n_steps - 1, step, s_init)
    result = []
    run_interleaved(scan_stages(scan_operands(n_steps - 1), carry, result))
    s_f, s_b = result[0]

    lane_b = lax.broadcasted_iota(jnp.int32, (ROW_BLOCK, LANES), 1)
    for r0 in range(0, T, ROW_BLOCK):
        o = o_sc[0, r0:r0 + ROW_BLOCK, :] + o_sc[1, r0:r0 + ROW_BLOCK, :]
        z = pltpu.roll(jnp.where(lane_b >= LANES - Z_SHIFT, za_ref[0, r0:r0 + ROW_BLOCK, :],
                                 zb_ref[0, r0:r0 + ROW_BLOCK, :]), Z_SHIFT, 1)
        y = o * lax.rsqrt(jnp.mean(o * o, axis=-1, keepdims=True) + NORM_EPS) * gn_ref[...]
        y_ref[0, r0:r0 + ROW_BLOCK, :] = (y * (z * jax.nn.sigmoid(z))).astype(y_ref.dtype)
    return s_f, s_b


def _gdn_body(Tc, Tl, *refs):
    h = pl.program_id(1)
    ctx_refs = refs[0:7]
    lat_refs = refs[7:14]
    shared = refs[14:18]
    y_ctx_ref, y_lat_ref = refs[18:20]
    sc = shared + refs[20:]
    zero = jnp.zeros((GDN_HEAD_DIM, GDN_HEAD_DIM), jnp.float32)
    states = _gdn_segment(Tc, h, ctx_refs + (y_ctx_ref,), sc, (zero, zero))
    _gdn_segment(Tl, h, lat_refs + (y_lat_ref,), sc, states)


def gdn_mix(p_ctx, gates_ctx, p_lat, gates_lat, conv_w, gdn_norm):
    B, Tc, _ = p_ctx.shape
    Tl = p_lat.shape[1]
    H, dh = GDN_HEADS, GDN_HEAD_DIM

    def seg_specs(T):
        return [pl.BlockSpec((1, T, dh), lambda b, h: (b, 0, h)),
                pl.BlockSpec((1, T, dh), lambda b, h: (b, 0, H + h)),
                pl.BlockSpec((1, T, dh), lambda b, h: (b, 0, 2 * H + h)),
                pl.BlockSpec((1, T, LANES), lambda b, h: (b, 0, 0)),
                pl.BlockSpec((1, LANES, T), lambda b, h: (b, 0, 0)),
                pl.BlockSpec((1, T, LANES), lambda b, h: (b, 0, Z_TILE0 + h)),
                pl.BlockSpec((1, T, LANES), lambda b, h: (b, 0, Z_TILE0 + 1 + h))]

    shared_specs = [pl.BlockSpec((GDN_CONV, dh), lambda b, h: (0, h)),
                    pl.BlockSpec((GDN_CONV, dh), lambda b, h: (0, H + h)),
                    pl.BlockSpec((GDN_CONV, dh), lambda b, h: (0, 2 * H + h)),
                    pl.BlockSpec((1, dh), lambda b, h: (0, 0))]
    rows_c, cols_c = gates_ctx
    rows_l, cols_l = gates_lat
    f32, bf16 = jnp.float32, jnp.bfloat16
    scratch = [pltpu.VMEM((Tl + 2 * CONV_PAD, LANES), f32),
               pltpu.VMEM((Tl, dh), f32), pltpu.VMEM((Tl, dh), f32), pltpu.VMEM((Tl, dh), f32),
               pltpu.VMEM((2, Tl, dh), f32), pltpu.VMEM((2, Tl, dh), bf16), pltpu.VMEM((2, Tl, dh), bf16),
               pltpu.VMEM((2, dh, Tl), bf16), pltpu.VMEM((2, Tl, GDN_CHUNK), bf16),
               pltpu.VMEM((2, Tl // GDN_CHUNK * 8, LANES), f32), pltpu.VMEM((2, Tl, dh), f32),
               pltpu.VMEM((2, 8, Tl), f32)]
    return pl.pallas_call(
        functools.partial(_gdn_body, Tc, Tl),
        grid=(B, H),
        in_specs=seg_specs(Tc) + seg_specs(Tl) + shared_specs,
        out_specs=[pl.BlockSpec((1, Tc, dh), lambda b, h: (b, 0, h)),
                   pl.BlockSpec((1, Tl, dh), lambda b, h: (b, 0, h))],
        out_shape=[jax.ShapeDtypeStruct((B, Tc, GDN_WIDTH), bf16), jax.ShapeDtypeStruct((B, Tl, GDN_WIDTH), bf16)],
        scratch_shapes=scratch,
        compiler_params=_cparams("arbitrary", "arbitrary"), name="gdn_mix",
    )(p_ctx, p_ctx, p_ctx, cols_c, rows_c, p_ctx, p_ctx,
      p_lat, p_lat, p_lat, cols_l, rows_l, p_lat, p_lat,
      conv_w, conv_w, conv_w, gdn_norm.reshape(1, dh))


def _pool_body(x_ref, w_ref, sc_ref, y_ref, pad_sc):
    T = x_ref.shape[1]
    G = POOL_GROUP
    zeros = jnp.zeros((POOL_PAD, G), jnp.float32)
    t = lax.broadcasted_iota(jnp.int32, (ROW_BLOCK, 1), 0)
    for i, win in enumerate(POOL_WINDOWS):
        half = win // 2
        lo = (STATE_W + i * G) // LANES * LANES
        off = STATE_W + i * G - lo
        pad_sc[0:POOL_PAD, :] = zeros
        pad_sc[POOL_PAD:POOL_PAD + T, :] = x_ref[0, :, lo:lo + G + LANES][:, off:off + G]
        pad_sc[POOL_PAD + T:2 * POOL_PAD + T, :] = zeros
        wb = w_ref[i].astype(jnp.bfloat16)
        for r0 in range(0, T, ROW_BLOCK):
            tot = jnp.zeros((ROW_BLOCK, G), jnp.float32)
            for s in range(-half, half):
                tot = tot + pad_sc[POOL_PAD + r0 + s:POOL_PAD + r0 + s + ROW_BLOCK, :]
            cnt = (jnp.minimum(t + r0 + half, T) - jnp.maximum(t + r0 - half, 0)).astype(jnp.float32)
            centred = tot / cnt - pad_sc[POOL_PAD + r0:POOL_PAD + r0 + ROW_BLOCK, :]
            y = jnp.dot(centred.astype(jnp.bfloat16), wb, preferred_element_type=jnp.float32)
            y_ref[0, r0:r0 + ROW_BLOCK, i * G:(i + 1) * G] = (y * sc_ref[:, i * G:(i + 1) * G]).astype(y_ref.dtype)


def pool_mix(p, pool_w, pool_scale):
    B, T, _ = p.shape
    n = len(POOL_WINDOWS)
    assert QKV_COLS % POOL_IN_W == 0 and STATE_W + POOL_WIDTH <= POOL_IN_W
    return pl.pallas_call(
        _pool_body,
        grid=(B,),
        in_specs=[pl.BlockSpec((1, T, POOL_IN_W), lambda b: (b, 0, QKV_COLS // POOL_IN_W)),
                  pl.BlockSpec((n, POOL_GROUP, POOL_GROUP), lambda b: (0, 0, 0)),
                  pl.BlockSpec((1, POOL_WIDTH), lambda b: (0, 0))],
        out_specs=pl.BlockSpec((1, T, POOL_WIDTH), lambda b: (b, 0, 0)),
        out_shape=jax.ShapeDtypeStruct((B, T, POOL_WIDTH), jnp.bfloat16),
        scratch_shapes=[pltpu.VMEM((T + 2 * POOL_PAD, POOL_GROUP), jnp.float32)],
        compiler_params=_cparams("arbitrary"), name="pool_mix",
    )(p, pool_w, pool_scale.reshape(1, POOL_WIDTH))


def pool_gdn_mixer(h_ctx, h_lat, w_in, j, pool_w, pool_scale, conv_w, a_log_f, a_log_b, dt_bias_f, dt_bias_b,
                   gdn_norm):
    B, Tl, D = h_lat.shape
    Tc = h_ctx.shape[1]
    w_in_t = jnp.swapaxes(w_in, 1, 2)
    p_lat = proj(h_lat.reshape(B * Tl, D), w_in_t, j, ncols=IN_COLS_PADDED, tm=1024,
                 w_out_major=True).reshape(B, Tl, IN_COLS_PADDED)
    p_ctx = proj(h_ctx.reshape(B * Tc, D), w_in_t, j, ncols=IN_COLS_PADDED, tm=1024,
                 w_out_major=True).reshape(B, Tc, IN_COLS_PADDED)
    gates_ctx = gdn_gates(p_ctx, a_log_f, a_log_b, dt_bias_f, dt_bias_b)
    gates_lat = gdn_gates(p_lat, a_log_f, a_log_b, dt_bias_f, dt_bias_b)
    g_ctx, g_lat = gdn_mix(p_ctx, gates_ctx, p_lat, gates_lat, conv_w, gdn_norm)
    return [pool_mix(p_ctx, pool_w, pool_scale), g_ctx], [pool_mix(p_lat, pool_w, pool_scale), g_lat]


def axial_rope_tables(rows):
    pos_row = jnp.repeat(jnp.arange(rows), GRID_W).astype(jnp.float32)
    pos_col = jnp.tile(jnp.arange(GRID_W), rows).astype(jnp.float32)
    quarter = DIFF_HEAD_DIM // 4
    inv_freq = ROPE_BASE ** (-jnp.arange(quarter, dtype=jnp.float32) / quarter)
    ang_r = pos_row[:, None] * inv_freq[None, :]
    ang_c = pos_col[:, None] * inv_freq[None, :]
    ang = jnp.concatenate([ang_r, ang_r, ang_c, ang_c], axis=-1)
    return jnp.cos(ang), jnp.sin(ang)


def _rope(x, cos, sin_signed):
    d = x.shape[-1]
    first = (lax.broadcasted_iota(jnp.int32, x.shape, 1) & (d // 2 - 1)) < d // 4
    partner = jnp.where(first, pltpu.roll(x, d - d // 4, 1), pltpu.roll(x, d // 4, 1))
    return x * cos + partner * sin_signed


def _diff_attn_body(lambda_init, tq, q_ref, kl_ref, vl_ref, kc_ref, vc_ref, cos_ref, sin_ref, lam_ref, g_ref,
                    o_ref, k_sc, v_sc):
    d = DIFF_HEAD_DIM
    L = kl_ref.shape[1]
    scale = d ** -0.5
    lv = lam_ref[...]
    lam = (jnp.exp(jnp.sum(lv[0:1] * lv[1:2], axis=1, keepdims=True))
           - jnp.exp(jnp.sum(lv[2:3] * lv[3:4], axis=1, keepdims=True)) + lambda_init)
    cos = cos_ref[...]
    sin = sin_ref[...]
    for m in range(2):
        k_sc[0:L, m * d:(m + 1) * d] = _rope(kl_ref[0, :, m * d:(m + 1) * d], cos, sin).astype(jnp.bfloat16)
    k_sc[L:, :] = kc_ref[0].astype(jnp.bfloat16)
    v_sc[0:L, :] = vl_ref[0].astype(jnp.bfloat16)
    v_sc[L:, :] = vc_ref[0].astype(jnp.bfloat16)

    def scores(r0, m):
        q = _rope(q_ref[0, pl.ds(r0, tq), m * d:(m + 1) * d], cos_ref[pl.ds(r0, tq), :],
                  sin_ref[pl.ds(r0, tq), :]).astype(jnp.bfloat16)
        return lax.dot_general(q, k_sc[:, m * d:(m + 1) * d], (((1,), (1,)), ((), ())),
                               preferred_element_type=jnp.float32)

    def finish(r0, s_pair):
        probs = []
        for s in s_pair:
            e = jnp.exp2((s - jnp.max(s, axis=-1, keepdims=True)) * (scale * math.log2(math.e)))
            probs.append(e * (1.0 / jnp.sum(e, axis=-1, keepdims=True)))
        a = (probs[0] - lam * probs[1]).astype(jnp.bfloat16)
        o = jnp.dot(a, v_sc[...], preferred_element_type=jnp.float32)
        o = o * lax.rsqrt(jnp.mean(o * o, axis=-1, keepdims=True) + NORM_EPS) * g_ref[...]
        o_ref[0, pl.ds(r0, tq), :] = (o * (1 - lambda_init)).astype(o_ref.dtype)

    def q_tile_pair(t, carry):
        ra = pl.multiple_of(2 * t * tq, tq)
        rb = pl.multiple_of(ra + tq, tq)
        sa = [scores(ra, m) for m in range(2)]
        sb = [scores(rb, m) for m in range(2)]
        finish(ra, sa)
        finish(rb, sb)
        return carry

    lax.fori_loop(0, L // (2 * tq), q_tile_pair, 0)


def diff_attention_lat(qkv_lat, kv_ctx, cos, sin_signed, lam_vecs, subln, lambda_init, tq=256):
    B, L, _ = qkv_lat.shape
    Lc = kv_ctx.shape[1]
    H, hd = DIFF_HEADS, 2 * DIFF_HEAD_DIM
    tq = min(tq, L // 2)
    assert L % (2 * tq) == 0
    return pl.pallas_call(
        functools.partial(_diff_attn_body, lambda_init, tq),
        grid=(B, H),
        in_specs=[pl.BlockSpec((1, L, hd), lambda b, h: (b, 0, h)),
                  pl.BlockSpec((1, L, hd), lambda b, h: (b, 0, H + h)),
                  pl.BlockSpec((1, L, hd), lambda b, h: (b, 0, 2 * H + h)),
                  pl.BlockSpec((1, Lc, hd), lambda b, h: (b, 0, h)),
                  pl.BlockSpec((1, Lc, hd), lambda b, h: (b, 0, H + h)),
                  pl.BlockSpec((L, DIFF_HEAD_DIM), lambda b, h: (0, 0)),
                  pl.BlockSpec((L, DIFF_HEAD_DIM), lambda b, h: (0, 0)),
                  pl.BlockSpec((4, DIFF_HEAD_DIM), lambda b, h: (0, 0)),
                  pl.BlockSpec((1, hd), lambda b, h: (0, 0))],
        out_specs=pl.BlockSpec((1, L, hd), lambda b, h: (b, 0, h)),
        out_shape=jax.ShapeDtypeStruct((B, L, H * hd), jnp.bfloat16),
        scratch_shapes=[pltpu.VMEM((L + Lc, hd), jnp.bfloat16), pltpu.VMEM((L + Lc, hd), jnp.bfloat16)],
        compiler_params=_cparams("arbitrary", "arbitrary"), name="diff_attention",
    )(qkv_lat, qkv_lat, qkv_lat, kv_ctx, kv_ctx, cos, sin_signed, lam_vecs, subln.reshape(1, hd))


def diff_attn_mixer(h_ctx, h_lat, w_qkv, j, lam_q1, lam_k1, lam_q2, lam_k2, subln, lambda_init, cos, sin):
    B, L, D = h_lat.shape
    Lc = h_ctx.shape[1]
    qkv = proj(h_lat.reshape(B * L, D), w_qkv, j, tm=1024).reshape(B, L, 3 * D)
    kv_ctx = proj(h_ctx.reshape(B * Lc, D), w_qkv, j, col0=D, tm=1024).reshape(B, Lc, 2 * D)
    quarter = DIFF_HEAD_DIM // 4
    first = (jnp.arange(DIFF_HEAD_DIM) % (2 * quarter)) < quarter
    sin_signed = jnp.where(first[None, :], -sin, sin)
    lam_vecs = jnp.stack([lam_q1, lam_k1, lam_q2, lam_k2]).astype(jnp.float32)
    return diff_attention_lat(qkv, kv_ctx, cos, sin_signed, lam_vecs, subln, lambda_init)


def _count_before(mask):
    R, L = mask.shape
    W = PREFIX_BLOCK
    m = mask.astype(jnp.float32)
    before = (lax.broadcasted_iota(jnp.int32, (W, W), 0) < lax.broadcasted_iota(jnp.int32, (W, W), 1)
              ).astype(jnp.bfloat16)
    carry = jnp.zeros((R, 1), jnp.float32)
    out = []
    for c in range(L // W):
        blk = m[:, c * W:(c + 1) * W]
        out.append(jnp.dot(blk.astype(jnp.bfloat16), before, preferred_element_type=jnp.float32) + carry)
        carry = carry + jnp.sum(blk, axis=1, keepdims=True)
    return out[0] if len(out) == 1 else jnp.concatenate(out, axis=1)


def _select_body(cap, aff_ref, pos_e_ref, gate_e_ref, pos_t_ref):
    L = aff_ref.shape[1]
    a = aff_ref[0].T[:N_EXPERTS]
    bits = pltpu.bitcast(a, jnp.int32)
    thr = jnp.zeros((N_EXPERTS, 1), jnp.int32)
    for bit in range(30, -1, -1):
        cand = thr | (1 << bit)
        cnt = jnp.sum((bits >= cand).astype(jnp.float32), axis=1, keepdims=True)
        thr = jnp.where(cnt >= cap, cand, thr)
    gt = bits > thr
    eq = bits == thr
    need = cap - jnp.sum(gt.astype(jnp.float32), axis=1, keepdims=True)
    sel = gt | (eq & (_count_before(eq) < need))
    pos = jnp.where(sel, _count_before(sel).astype(jnp.int32), -1)
    pos_e_ref[0] = pos
    gate_e_ref[0] = jnp.where(sel, a, 0.0)
    pad = jnp.full((LANES - N_EXPERTS, L), -1, jnp.int32)
    pos_t_ref[0] = jnp.concatenate([pos, pad], axis=0).T


def ec_select(aff, cap):
    B, L, _ = aff.shape
    return pl.pallas_call(
        functools.partial(_select_body, cap),
        grid=(B,),
        in_specs=[pl.BlockSpec((1, L, LANES), lambda b: (b, 0, 0))],
        out_specs=[pl.BlockSpec((1, N_EXPERTS, L), lambda b: (b, 0, 0)),
                   pl.BlockSpec((1, N_EXPERTS, L), lambda b: (b, 0, 0)),
                   pl.BlockSpec((1, L, LANES), lambda b: (b, 0, 0))],
        out_shape=[jax.ShapeDtypeStruct((B, N_EXPERTS, L), jnp.int32),
                   jax.ShapeDtypeStruct((B, N_EXPERTS, L), jnp.float32),
                   jax.ShapeDtypeStruct((B, L, LANES), jnp.int32)],
        compiler_params=_cparams("arbitrary"), name="ec_select",
    )(aff)


def _gather_body(h_ref, pos_e_ref, gate_e_ref, xe_ref, gc_ref):
    e = pl.program_id(1)
    cap = xe_ref.shape[2]
    L = h_ref.shape[1]
    pos_row = pos_e_ref[0, pl.ds(e, 1), :]
    gate_row = gate_e_ref[0, pl.ds(e, 1), :]
    hit = lax.broadcasted_iota(jnp.int32, (cap, L), 0) == pos_row
    xe_ref[0, 0] = jnp.dot(hit.astype(jnp.bfloat16), h_ref[0],
                           preferred_element_type=jnp.float32).astype(jnp.bfloat16)
    gc_ref[0, 0] = jnp.sum(jnp.where(hit, gate_row, 0.0), axis=1, keepdims=True)


def ec_gather(h, pos_e, gate_e, cap):
    B, L, D = h.shape
    return pl.pallas_call(
        _gather_body,
        grid=(B, N_EXPERTS),
        in_specs=[pl.BlockSpec((1, L, D), lambda b, e: (b, 0, 0)),
                  pl.BlockSpec((1, N_EXPERTS, L), lambda b, e: (b, 0, 0)),
                  pl.BlockSpec((1, N_EXPERTS, L), lambda b, e: (b, 0, 0))],
        out_specs=[pl.BlockSpec((1, 1, cap, D), lambda b, e: (b, e, 0, 0)),
                   pl.BlockSpec((1, 1, cap, 1), lambda b, e: (b, e, 0, 0))],
        out_shape=[jax.ShapeDtypeStruct((B, N_EXPERTS, cap, D), jnp.bfloat16),
                   jax.ShapeDtypeStruct((B, N_EXPERTS, cap, 1), jnp.float32)],
        compiler_params=_cparams("arbitrary", "arbitrary"), name="ec_gather",
    )(h, pos_e, gate_e)


def _experts_body(nf, n_groups, *refs):
    xg = refs[:2 * n_groups]
    w1_ref, w3_ref, w2_ref = refs[2 * n_groups:2 * n_groups + 3]
    ye_refs = refs[2 * n_groups + 3:3 * n_groups + 3]
    hid_refs = refs[3 * n_groups + 3:]
    s = pl.program_id(1)

    @pl.when(s < nf)
    def _():
        w1b = w1_ref[0].astype(jnp.bfloat16)
        w3b = w3_ref[0].astype(jnp.bfloat16)
        for g in range(n_groups):
            xe_ref = xg[2 * g]
            B, _, cap, D = xe_ref.shape
            x = xe_ref[...].reshape(B * cap, D)
            h1 = jnp.dot(x, w1b, preferred_element_type=jnp.float32)
            h3 = jnp.dot(x, w3b, preferred_element_type=jnp.float32)
            hid_refs[g][s] = (h1 * jax.nn.sigmoid(h1) * h3).astype(jnp.bfloat16)

    @pl.when(s >= nf)
    def _():
        w2b = w2_ref[0].astype(jnp.bfloat16)
        for g in range(n_groups):
            hid_ref, gc_ref, ye_ref = hid_refs[g], xg[2 * g + 1], ye_refs[g]
            hid = jnp.concatenate([hid_ref[f] for f in range(nf)], axis=1) if nf > 1 else hid_ref[0]
            y = jnp.dot(hid, w2b, preferred_element_type=jnp.float32)
            y = y * gc_ref[...].reshape(y.shape[0], 1)
            ye_ref[...] = y.astype(jnp.bfloat16).reshape(ye_ref.shape)


def ec_experts(groups, w1, w3, w2, layer, tf=256, tn=1024):
    _, E, _, D = groups[0][0].shape
    F = w1.shape[3]
    tf = min(tf, F)
    tn = min(tn, D)
    nf, nn = F // tf, D // tn
    in_specs, args, out_specs, out_shape, scratch = [], [], [], [], []
    for xe, gc in groups:
        B, _, cap, _ = xe.shape
        in_specs += [pl.BlockSpec((B, 1, cap, D), lambda e, s: (0, e, 0, 0), pipeline_mode=pl.Buffered(1)),
                     pl.BlockSpec((B, 1, cap, 1), lambda e, s: (0, e, 0, 0))]
        args += [xe, gc]
        out_specs.append(pl.BlockSpec((B, 1, cap, tn), lambda e, s: (0, e, 0, jnp.maximum(s - nf, 0))))
        out_shape.append(jax.ShapeDtypeStruct((B, E, cap, D), jnp.bfloat16))
        scratch.append(pltpu.VMEM((nf, B * cap, tf), jnp.bfloat16))
    in_specs += [pl.BlockSpec((None, 1, D, tf), lambda e, s: (layer, e, 0, jnp.minimum(s, nf - 1))),
                 pl.BlockSpec((None, 1, D, tf), lambda e, s: (layer, e, 0, jnp.minimum(s, nf - 1))),
                 pl.BlockSpec((None, 1, F, tn), lambda e, s: (layer, e, 0, jnp.maximum(s - nf, 0)))]
    return pl.pallas_call(
        functools.partial(_experts_body, nf, len(groups)),
        grid=(E, nf + nn),
        in_specs=in_specs, out_specs=out_specs, out_shape=out_shape, scratch_shapes=scratch,
        compiler_params=_cparams("arbitrary", "arbitrary"), name="ec_experts",
    )(*args, w1, w3, w2)


def _combine_body(ctx_row, cap, pos_t_ref, ye_ref, x_ref, gate_ref, o_ref):
    row = pl.program_id(0) if ctx_row is None else ctx_row
    K = ye_ref.shape[1]
    assert cap & (cap - 1) == 0 and cap <= 256
    col_expert = lax.broadcasted_iota(jnp.int32, (LANES, K), 1) >> (cap.bit_length() - 1)
    spread = (col_expert == lax.broadcasted_iota(jnp.int32, (LANES, K), 0)).astype(jnp.bfloat16)
    pos = jnp.dot(pos_t_ref[0].astype(jnp.float32).astype(jnp.bfloat16), spread,
                  preferred_element_type=jnp.float32)
    col_slot = (lax.broadcasted_iota(jnp.int32, (1, K), 1) & (cap - 1)).astype(jnp.float32)
    hit = (pos == col_slot).astype(jnp.bfloat16)
    y = jnp.dot(hit, ye_ref[0], preferred_element_type=jnp.float32)
    o_ref[0] = x_ref[0] + _mod_row(gate_ref, row) * y


def ec_combine(x, pos_t, ye, mod, k_gate, cap, ctx_row=None, tt=512, tn=1024):
    B, L, D = x.shape
    tt = min(tt, L)
    tn = min(tn, D)
    K = N_EXPERTS * cap
    return pl.pallas_call(
        functools.partial(_combine_body, ctx_row, cap),
        grid=(B, D // tn, L // tt),
        in_specs=[pl.BlockSpec((1, tt, LANES), lambda b, n, t: (b, t, 0)),
                  pl.BlockSpec((1, K, tn), lambda b, n, t: (b, 0, n)),
                  pl.BlockSpec((1, tt, tn), lambda b, n, t: (b, t, n)),
                  pl.BlockSpec((MOD_ROWS, tn), lambda b, n, t: (0, k_gate * (D // tn) + n))],
        out_specs=pl.BlockSpec((1, tt, tn), lambda b, n, t: (b, t, n)),
        out_shape=jax.ShapeDtypeStruct((B, L, D), jnp.float32),
        compiler_params=_cparams("arbitrary", "arbitrary", "arbitrary"), name="ec_combine",
    )(pos_t, ye.reshape(B, K, D), x, mod)


def moe_blocks(streams, gain, mod, router, w1, w3, w2, layer):
    routed = []
    for x, ctx_row in streams:
        cap = EC_CAPACITY * x.shape[1] // N_EXPERTS
        h, aff = normmod(x, gain, mod, 3, 4, ctx_row=ctx_row, router=router)
        pos_e, gate_e, pos_t = ec_select(aff, cap)
        routed.append((ec_gather(h, pos_e, gate_e, cap), pos_t, cap))
    ye = ec_experts([r[0] for r in routed], w1, w3, w2, layer)
    return [ec_combine(x, pos_t, y, mod, 5, cap, ctx_row=ctx_row)
            for (x, ctx_row), (_, pos_t, cap), y in zip(streams, routed, ye)]


def kernel(x, c, ctx, c_ctx, ada_w, ada_b, norm_mix, norm_ffn, w_in, pool_w, pool_scale, conv_w, a_log_f, a_log_b, dt_bias_f, dt_bias_b, gdn_norm, w_out_ab, w_qkv, lam_q1, lam_k1, lam_q2, lam_k2, subln, w_out_c, router, w1, w3, w2, final_norm):
    depth = ada_w.shape[0]
    B, L, D = x.shape
    Lc = ctx.shape[1]
    assert B < MOD_ROWS
    ctx_row = B
    cos, sin = axial_rope_tables(L // GRID_W)
    cond = jnp.zeros((MOD_ROWS, D), jnp.float32).at[:B].set(jax.nn.silu(c)).at[ctx_row].set(jax.nn.silu(c_ctx))
    cond = cond.astype(jnp.bfloat16)
    x_lat, x_ctx = x, ctx
    for layer in range(depth):
        last = layer == depth - 1
        j = layer // 2
        mod = proj(cond, ada_w, layer) + ada_b[layer]
        h_lat = normmod(x_lat, norm_mix[layer], mod, 0, 1)
        h_ctx = normmod(x_ctx, norm_mix[layer], mod, 0, 1, ctx_row=ctx_row)
        if layer % 2 == 0:
            y_ctx, y_lat = pool_gdn_mixer(h_ctx, h_lat, w_in, j, pool_w[j], pool_scale[j], conv_w[j],
                                          a_log_f[j], a_log_b[j], dt_bias_f[j], dt_bias_b[j], gdn_norm[j])
            w_out = w_out_ab
        else:
            assert last, "context outputs of an attention layer are only skipped when no later layer reads them"
            lambda_init = 0.8 - 0.6 * math.exp(-0.3 * layer)
            y_ctx = None
            y_lat = [diff_attn_mixer(h_ctx, h_lat, w_qkv, j, lam_q1[j], lam_k1[j], lam_q2[j], lam_k2[j],
                                     subln[j], lambda_init, cos, sin)]
            w_out = w_out_c
        x_lat = proj([p.reshape(B * L, -1) for p in y_lat], w_out, j, tm=1024,
                     residual=(x_lat.reshape(B * L, D), mod, 2, L, None)).reshape(B, L, D)
        streams = [(x_lat, None)]
        if not last:
            x_ctx = proj([p.reshape(B * Lc, -1) for p in y_ctx], w_out, j, tm=1024,
                         residual=(x_ctx.reshape(B * Lc, D), mod, 2, Lc, ctx_row)).reshape(B, Lc, D)
            streams.append((x_ctx, ctx_row))
        outs = moe_blocks(streams, norm_ffn[layer], mod, router[layer], w1, w3, w2, layer)
        x_lat = outs[0]
        if not last:
            x_ctx = outs[1]
    return final_rmsnorm(x_lat, final_norm)
```

```python
import functools
import math

import jax
import jax.numpy as jnp
from jax import lax
from jax.experimental import pallas as pl
from jax.experimental.pallas import tpu as pltpu

D_MODEL = 4096
GRID_W = 64
NORM_EPS = 1e-6
POOL_WINDOWS = (2, 4, 8, 16)
POOL_GROUP = D_MODEL // 16
POOL_WIDTH = len(POOL_WINDOWS) * POOL_GROUP
GDN_HEAD_DIM = 128
GDN_HEADS = (D_MODEL - POOL_WIDTH) // GDN_HEAD_DIM
GDN_WIDTH = GDN_HEADS * GDN_HEAD_DIM
GDN_CONV = 5
GDN_CHUNK = 64
QKV_COLS = 3 * GDN_WIDTH
DIFF_HEAD_DIM = 128
DIFF_HEADS = D_MODEL // (2 * DIFF_HEAD_DIM)
ROPE_BASE = 10000.0
N_EXPERTS = 16
EC_CAPACITY = 2

LANES = 128
MOD_ROWS = 8
PREFIX_BLOCK = 256
VMEM_LIMIT_BYTES = 56 * 1024 * 1024

PROJ_TN = 512
GDN_PAIR = 2 * GDN_CHUNK
GDN_PAIRS_PER_STEP = 2
GDN_SOLVE_BASE = 4
CONV_PAD = 8
ROW_BLOCK = 128
STATE_W = 4 * GDN_HEADS
REST_TILE0 = QKV_COLS // LANES
IN_COLS_PADDED = -(-(QKV_COLS + STATE_W + POOL_WIDTH + GDN_WIDTH) // PROJ_TN) * PROJ_TN
Z_TILE0 = REST_TILE0 + (STATE_W + POOL_WIDTH) // LANES
Z_SHIFT = LANES - (STATE_W + POOL_WIDTH) % LANES
POOL_PAD = 16
POOL_IN_W = 1152


def _cparams(*sem):
    return pltpu.CompilerParams(dimension_semantics=sem, vmem_limit_bytes=VMEM_LIMIT_BYTES)


def _mod_row(mod_ref, row):
    return mod_ref[pl.ds(row, 1), :]


def _proj_body(n_parts, rows_per_sample, ctx_row, valid_cols, w_out_major, *refs):
    a_refs, w_ref = refs[:n_parts], refs[n_parts]
    if rows_per_sample is None:
        o_ref, wb_ref = refs[n_parts + 1:]
    else:
        x_ref, gate_ref, o_ref, wb_ref = refs[n_parts + 1:]
    i = pl.program_id(1)
    out_axis = 0 if w_out_major else 1

    @pl.when(i == 0)
    def _():
        w = w_ref[...]
        if valid_cols is not None:
            col = pl.program_id(0) * w.shape[out_axis] + lax.broadcasted_iota(jnp.int32, w.shape, out_axis)
            w = jnp.where(col < valid_cols, w, 0.0)
        wb_ref[...] = w.astype(jnp.bfloat16)

    y, k0 = None, 0
    for a_ref in a_refs:
        k1 = k0 + a_ref.shape[1]
        if w_out_major:
            part = _nt(a_ref[...], wb_ref[:, k0:k1])
        else:
            part = jnp.dot(a_ref[...], wb_ref[k0:k1, :], preferred_element_type=jnp.float32)
        y = part if y is None else y + part
        k0 = k1
    if rows_per_sample is None:
        o_ref[...] = y.astype(o_ref.dtype)
    else:
        row = (i * o_ref.shape[0]) // rows_per_sample if ctx_row is None else ctx_row
        o_ref[...] = x_ref[...] + _mod_row(gate_ref, row) * y


def proj(a, w, layer, *, col0=0, ncols=None, tm=512, tn=PROJ_TN, out_dtype=jnp.float32, residual=None,
         w_out_major=False):
    parts = list(a) if isinstance(a, (list, tuple)) else [a]
    M = parts[0].shape[0]
    K = sum(p.shape[1] for p in parts)
    n_out = w.shape[1] if w_out_major else w.shape[2]
    ncols = n_out - col0 if ncols is None else ncols
    tm = min(tm, M)
    assert M % tm == 0 and ncols % tn == 0 and col0 % tn == 0, (M, ncols, col0, tm, tn)
    j0 = col0 // tn
    valid_cols = n_out - col0 if col0 + ncols > n_out else None
    in_specs = [pl.BlockSpec((tm, p.shape[1]), lambda j, i: (i, 0)) for p in parts]
    if w_out_major:
        in_specs.append(pl.BlockSpec((None, tn, K), lambda j, i: (layer, j0 + j, 0)))
    else:
        in_specs.append(pl.BlockSpec((None, K, tn), lambda j, i: (layer, 0, j0 + j)))
    args = parts + [w]
    rows_per_sample = ctx_row = None
    if residual is not None:
        x, mod, k_gate, rows_per_sample, ctx_row = residual
        assert ctx_row is not None or rows_per_sample % tm == 0
        in_specs += [pl.BlockSpec((tm, tn), lambda j, i: (i, j)),
                     pl.BlockSpec((MOD_ROWS, tn), lambda j, i: (0, k_gate * (ncols // tn) + j))]
        args += [x, mod]
    return pl.pallas_call(
        functools.partial(_proj_body, len(parts), rows_per_sample, ctx_row, valid_cols, w_out_major),
        grid=(ncols // tn, M // tm),
        in_specs=in_specs,
        out_specs=pl.BlockSpec((tm, tn), lambda j, i: (i, j)),
        out_shape=jax.ShapeDtypeStruct((M, ncols), out_dtype),
        scratch_shapes=[pltpu.VMEM((tn, K) if w_out_major else (K, tn), jnp.bfloat16)],
        compiler_params=_cparams("arbitrary", "arbitrary"), name="proj",
    )(*args)


def _normmod_body(ctx_row, with_router, x_ref, g_ref, shift_ref, scale_ref, *rest):
    row = pl.program_id(0) if ctx_row is None else ctx_row
    x = x_ref[0]
    y = x * lax.rsqrt(jnp.mean(x * x, axis=-1, keepdims=True) + NORM_EPS) * g_ref[...]
    h = (y * (1.0 + _mod_row(scale_ref, row)) + _mod_row(shift_ref, row)).astype(jnp.bfloat16)
    if not with_router:
        (h_ref,) = rest
        h_ref[0] = h
        return
    r_ref, h_ref, aff_ref = rest
    h_ref[0] = h
    logits = jnp.dot(h, r_ref[...], preferred_element_type=jnp.float32)
    lane = lax.broadcasted_iota(jnp.int32, logits.shape, 1)
    logits = jnp.where(lane < N_EXPERTS, logits, -jnp.inf)
    e = jnp.exp(logits - jnp.max(logits, axis=-1, keepdims=True))
    aff_ref[0] = e / jnp.sum(e, axis=-1, keepdims=True)


def normmod(x, gain, mod, k_shift, k_scale, ctx_row=None, router=None, tr=256):
    B, L, D = x.shape
    tr = min(tr, L)
    in_specs = [pl.BlockSpec((1, tr, D), lambda b, t: (b, t, 0)),
                pl.BlockSpec((1, D), lambda b, t: (0, 0)),
                pl.BlockSpec((MOD_ROWS, D), lambda b, t: (0, k_shift)),
                pl.BlockSpec((MOD_ROWS, D), lambda b, t: (0, k_scale))]
    args = [x, gain.reshape(1, D), mod, mod]
    out_specs = [pl.BlockSpec((1, tr, D), lambda b, t: (b, t, 0))]
    out_shape = [jax.ShapeDtypeStruct((B, L, D), jnp.bfloat16)]
    if router is not None:
        rp = jnp.pad(router.astype(jnp.bfloat16), ((0, 0), (0, LANES - router.shape[1])))
        in_specs.append(pl.BlockSpec((D, LANES), lambda b, t: (0, 0)))
        args.append(rp)
        out_specs.append(pl.BlockSpec((1, tr, LANES), lambda b, t: (b, t, 0)))
        out_shape.append(jax.ShapeDtypeStruct((B, L, LANES), jnp.float32))
    out = pl.pallas_call(
        functools.partial(_normmod_body, ctx_row, router is not None),
        grid=(B, L // tr), in_specs=in_specs, out_specs=out_specs, out_shape=out_shape,
        compiler_params=_cparams("arbitrary", "arbitrary"), name="normmod",
    )(*args)
    return out if router is not None else out[0]


def _final_norm_body(x_ref, g_ref, o_ref):
    x = x_ref[...]
    o_ref[...] = x * lax.rsqrt(jnp.mean(x * x, axis=-1, keepdims=True) + NORM_EPS) * g_ref[...]


def final_rmsnorm(x, gain, tr=512):
    B, L, D = x.shape
    M = B * L
    tr = min(tr, M)
    return pl.pallas_call(
        _final_norm_body,
        grid=(M // tr,),
        in_specs=[pl.BlockSpec((tr, D), lambda i: (i, 0)), pl.BlockSpec((1, D), lambda i: (0, 0))],
        out_specs=pl.BlockSpec((tr, D), lambda i: (i, 0)),
        out_shape=jax.ShapeDtypeStruct((M, D), x.dtype),
        compiler_params=_cparams("arbitrary"), name="final_rmsnorm",
    )(x.reshape(M, D), gain.reshape(1, D)).reshape(B, L, D)


def _gates_body(st_ref, alog_ref, dtb_ref, rows_ref, cols_ref):
    T = st_ref.shape[1]
    H = GDN_HEADS
    x = st_ref[0].T
    t = x + dtb_ref[...]
    softplus = jnp.maximum(t, 0.0) + jnp.log1p(jnp.exp(-jnp.abs(t)))
    r_full = lax.broadcasted_iota(jnp.int32, x.shape, 0)
    val = jnp.where(r_full < 2 * H, -jnp.exp(alog_ref[...]) * softplus, jax.nn.sigmoid(x))
    pos = lax.broadcasted_iota(jnp.int32, (LANES, LANES), 1) & (GDN_CHUNK - 1)
    r = lax.broadcasted_iota(jnp.int32, (LANES, LANES), 0)
    tiles = []
    for c in range(T // LANES):
        v = val[:, c * LANES:(c + 1) * LANES]
        pre, suf, s = v, v, 1
        while s < GDN_CHUNK:
            pre = pre + jnp.where(pos >= s, pltpu.roll(pre, s, 1), 0.0)
            suf = suf + jnp.where(pos < GDN_CHUNK - s, pltpu.roll(suf, LANES - s, 1), 0.0)
            s *= 2
        tiles.append(jnp.where(r < H, pre, jnp.where(r < 2 * H, suf, v)))
    out = jnp.concatenate(tiles, axis=1) if len(tiles) > 1 else tiles[0]
    rows_ref[0] = out
    cols_ref[0] = out.T


def gdn_gates(p, a_log_f, a_log_b, dt_bias_f, dt_bias_b):
    B, T, _ = p.shape
    pad = jnp.zeros((LANES - 2 * GDN_HEADS,), jnp.float32)
    alog = jnp.concatenate([a_log_f, a_log_b, pad]).astype(jnp.float32).reshape(LANES, 1)
    dtb = jnp.concatenate([dt_bias_f, dt_bias_b, pad]).astype(jnp.float32).reshape(LANES, 1)
    return pl.pallas_call(
        _gates_body,
        grid=(B,),
        in_specs=[pl.BlockSpec((1, T, LANES), lambda b: (b, 0, REST_TILE0)),
                  pl.BlockSpec((LANES, 1), lambda b: (0, 0)),
                  pl.BlockSpec((LANES, 1), lambda b: (0, 0))],
        out_specs=[pl.BlockSpec((1, LANES, T), lambda b: (b, 0, 0)),
                   pl.BlockSpec((1, T, LANES), lambda b: (b, 0, 0))],
        out_shape=[jax.ShapeDtypeStruct((B, LANES, T), jnp.float32),
                   jax.ShapeDtypeStruct((B, T, LANES), jnp.float32)],
        compiler_params=_cparams("arbitrary"), name="gdn_gates",
    )(p, alog, dtb)


def _conv_silu(x_ref, w_ref, pad_sc, dst_sc, T, unit_norm):
    half = GDN_CONV // 2
    zeros = jnp.zeros((CONV_PAD, LANES), jnp.float32)
    pad_sc[0:CONV_PAD, :] = zeros
    pad_sc[CONV_PAD:CONV_PAD + T, :] = x_ref[0]
    pad_sc[CONV_PAD + T:2 * CONV_PAD + T, :] = zeros
    w = w_ref[...]
    for r0 in range(0, T, ROW_BLOCK):
        y = jnp.zeros((ROW_BLOCK, LANES), jnp.float32)
        for j in range(GDN_CONV):
            s = CONV_PAD + r0 + j - half
            y = y + w[j:j + 1, :] * pad_sc[s:s + ROW_BLOCK, :]
        y = y * jax.nn.sigmoid(y)
        if unit_norm:
            y = y * lax.rsqrt(jnp.sum(y * y, axis=-1, keepdims=True) + NORM_EPS)
        dst_sc[r0:r0 + ROW_BLOCK, :] = y


def _nt(a, b):
    return lax.dot_general(a, b, (((1,), (1,)), ((), ())), preferred_element_type=jnp.float32)


def _gdn_segment(T, h, refs, sc, s_init):
    pq_ref, pk_ref, pv_ref, cols_ref, rows_ref, za_ref, zb_ref, y_ref = refs
    (cwq_ref, cwk_ref, cwv_ref, gn_ref, pad_sc, q_sc, k_sc, v_sc, u_sc, w_sc, qd_sc, kdt_sc, attn_sc, gend_sc,
     o_sc, grow_sc) = sc
    C, P, H = GDN_CHUNK, GDN_PAIR, GDN_HEADS
    n_pairs = T // P
    bf16 = jnp.bfloat16
    scale = GDN_HEAD_DIM ** -0.5
    for d in range(2):
        grow_sc[d, 0:1, 0:T] = rows_ref[0, pl.ds(d * H + h, 1), :]
    _conv_silu(pq_ref, cwq_ref, pad_sc, q_sc, T, True)
    _conv_silu(pk_ref, cwk_ref, pad_sc, k_sc, T, True)
    _conv_silu(pv_ref, cwv_ref, pad_sc, v_sc, T, False)

    lane = lax.broadcasted_iota(jnp.int32, (P, LANES), 1)
    ii = lax.broadcasted_iota(jnp.int32, (C, C), 0)
    jj = lax.broadcasted_iota(jnp.int32, (C, C), 1)

    halves = [slice(c * C, (c + 1) * C) for c in range(2)]

    G = min(GDN_PAIRS_PER_STEP, n_pairs)
    assert n_pairs % G == 0
    n_steps = n_pairs // G

    def units(i):
        fwd = [(0, G * i + g) for g in range(G)]
        return fwd + [(1, n_pairs - 1 - m) for _, m in fwd]

    def mm(a, b):
        return jnp.dot(a, b, preferred_element_type=jnp.float32)


    def prep_stages(i):
        neg_a, rhs, dest, held = [], [], [], []
        for d, m in units(i):
            r0 = pl.multiple_of(m * P, P)
            q2 = q_sc[pl.ds(r0, P), :]
            k2 = k_sc[pl.ds(r0, P), :]
            v2 = v_sc[pl.ds(r0, P), :]
            cols = cols_ref[0, pl.ds(r0, P), :]
            qs = q2 * scale
            kh = [k2[sl].astype(bf16) for sl in halves]
            qk = [_nt(qs[sl].astype(bf16), kh[c]) for c, sl in enumerate(halves)]
            gcol = jnp.sum(jnp.where(lane == d * H + h, cols, 0.0), axis=1, keepdims=True)
            bcol = jnp.sum(jnp.where(lane == (2 + d) * H + h, cols, 0.0), axis=1, keepdims=True)
            eg = jnp.exp(gcol)
            kb = k2 * bcol
            kk = [_nt(kb[sl].astype(bf16), kh[c]) for c, sl in enumerate(halves)]
            held.append((d, m, r0, k2, v2, qs, qk, kk, gcol, bcol, eg, kb))
        yield
        for d, m, r0, k2, v2, qs, qk, kk, gcol, bcol, eg, kb in held:
            grow = grow_sc[d, 0:1, pl.ds(r0, P)]
            keep = (ii >= jj) if d == 0 else (ii <= jj)
            strict = (ii > jj) if d == 0 else (ii < jj)
            rhs_uw = jnp.concatenate([v2 * bcol, kb * eg], axis=1)
            qd_sc[d, pl.ds(r0, P), :] = (qs * eg).astype(bf16)
            kd = []
            for c, sl in enumerate(halves):
                rc = pl.multiple_of(r0 + c * C, C)
                gc = gcol[sl]
                decay = jnp.exp(jnp.where(keep, gc - grow[:, sl], -jnp.inf))
                neg_a.append(jnp.where(strict, -(kk[c] * decay), 0.0))
                rhs.append(rhs_uw[sl])
                dest.append((d, rc))
                attn_sc[d, pl.ds(rc, C), :] = (qk[c] * decay).astype(bf16)
                g_last = gc[C - 1:C] if d == 0 else gc[0:1]
                kd.append(k2[sl] * jnp.exp(g_last - gc))
                gend_sc[d, pl.ds(pl.multiple_of((2 * m + c) * 8, 8), 8), :] = jnp.broadcast_to(
                    jnp.exp(g_last), (8, LANES))
            kdt_sc[d, :, pl.ds(r0, P)] = jnp.concatenate(kd, axis=0).T.astype(bf16)
        s0 = GDN_SOLVE_BASE
        same = lambda s: (ii >> (s.bit_length() - 1)) == (jj >> (s.bit_length() - 1))
        d_ = [jnp.where(same(s0), x, 0.0) for x in neg_a]
        db = [x.astype(bf16) for x in d_]
        p = [mm(x, x) for x in db]
        yield
        dp = [mm(a, b.astype(bf16)) for a, b in zip(db, p)]
        yield
        e = [a + b + ab for a, b, ab in zip(d_, p, dp)]
        s = s0
        while s < C:
            off = same(2 * s) & jnp.logical_not(same(s))
            o = [jnp.where(off, x, 0.0) for x in neg_a]
            oe = [mm(a.astype(bf16), b.astype(bf16)) for a, b in zip(o, e)]
            yield
            m_ = [a + b for a, b in zip(o, oe)]
            em = [mm(a.astype(bf16), b.astype(bf16)) for a, b in zip(e, m_)]
            yield
            e = [a + b + ab for a, b, ab in zip(e, m_, em)]
            s *= 2
        corr = [mm(a.astype(bf16), r.astype(bf16)) for a, r in zip(e, rhs)]
        yield
        for (d, rc), r, cr in zip(dest, rhs, corr):
            s = r + cr
            u_sc[d, pl.ds(rc, C), :] = s[:, :GDN_HEAD_DIM]
            w_sc[d, pl.ds(rc, C), :] = s[:, GDN_HEAD_DIM:].astype(bf16)

    def scan_operands(i):
        ops = ([], [])
        for d, m in units(i):
            r0 = pl.multiple_of(m * P, P)
            kdt = kdt_sc[d, :, pl.ds(r0, P)]
            for c in ((0, 1) if d == 0 else (1, 0)):
                rc = pl.multiple_of(r0 + c * C, C)
                g_end = gend_sc[d, pl.ds(pl.multiple_of((2 * m + c) * 8, 8), 8), :][0:1]
                ops[d].append((d, rc, u_sc[d, pl.ds(rc, C), :], w_sc[d, pl.ds(rc, C), :], qd_sc[d, pl.ds(rc, C), :],
                               attn_sc[d, pl.ds(rc, C), :], kdt[:, c * C:(c + 1) * C], g_end))
        return ops

    def scan_stages(ops, carry, result):
        s = list(carry)
        for k in range(2 * G):
            cur = [ops[d][k] for d in range(2)]
            sb = [s[d].astype(bf16) for d in range(2)]
            ws = [mm(cur[d][3], sb[d]) for d in range(2)]
            qs_ = [mm(cur[d][4], sb[d]) for d in range(2)]
            yield
            vb = [(cur[d][2] - ws[d]).astype(bf16) for d in range(2)]
            av = [mm(cur[d][5], vb[d]) for d in range(2)]
            kv = [mm(cur[d][6], vb[d]) for d in range(2)]
            yield
            for d in range(2):
                o_sc[d, pl.ds(cur[d][1], C), :] = qs_[d] + av[d]
                s[d] = s[d] * cur[d][7] + kv[d]
        result.append(tuple(s))

    def run_interleaved(*gens):
        gens = list(gens)
        while gens:
            for g in list(gens):
                try:
                    next(g)
                except StopIteration:
                    gens.remove(g)

    def step(i, carry):
        result = []
        run_interleaved(prep_stages(i + 1), scan_stages(scan_operands(i), carry, result))
        return result[0]

    run_interleaved(prep_stages(0))
    carry = lax.fori_loop(0, n_steps - 1, step, s_init)
    result = []
    run_interleaved(scan_stages(scan_operands(n_steps - 1), carry, result))
    s_f, s_b = result[0]

    lane_b = lax.broadcasted_iota(jnp.int32, (ROW_BLOCK, LANES), 1)
    for r0 in range(0, T, ROW_BLOCK):
        o = o_sc[0, r0:r0 + ROW_BLOCK, :] + o_sc[1, r0:r0 + ROW_BLOCK, :]
        z = pltpu.roll(jnp.where(lane_b >= LANES - Z_SHIFT, za_ref[0, r0:r0 + ROW_BLOCK, :],
                                 zb_ref[0, r0:r0 + ROW_BLOCK, :]), Z_SHIFT, 1)
        y = o * lax.rsqrt(jnp.mean(o * o, axis=-1, keepdims=True) + NORM_EPS) * gn_ref[...]
        y_ref[0, r0:r0 + ROW_BLOCK, :] = (y * (z * jax.nn.sigmoid(z))).astype(y_ref.dtype)
    return s_f, s_b


def _gdn_body(Tc, Tl, *refs):
    h = pl.program_id(1)
    ctx_refs = refs[0:7]
    lat_refs = refs[7:14]
    shared = refs[14:18]
    y_ctx_ref, y_lat_ref = refs[18:20]
    sc = shared + refs[20:]
    zero = jnp.zeros((GDN_HEAD_DIM, GDN_HEAD_DIM), jnp.float32)
    states = _gdn_segment(Tc, h, ctx_refs + (y_ctx_ref,), sc, (zero, zero))
    _gdn_segment(Tl, h, lat_refs + (y_lat_ref,), sc, states)


def gdn_mix(p_ctx, gates_ctx, p_lat, gates_lat, conv_w, gdn_norm):
    B, Tc, _ = p_ctx.shape
    Tl = p_lat.shape[1]
    H, dh = GDN_HEADS, GDN_HEAD_DIM

    def seg_specs(T):
        return [pl.BlockSpec((1, T, dh), lambda b, h: (b, 0, h)),
                pl.BlockSpec((1, T, dh), lambda b, h: (b, 0, H + h)),
                pl.BlockSpec((1, T, dh), lambda b, h: (b, 0, 2 * H + h)),
                pl.BlockSpec((1, T, LANES), lambda b, h: (b, 0, 0)),
                pl.BlockSpec((1, LANES, T), lambda b, h: (b, 0, 0)),
                pl.BlockSpec((1, T, LANES), lambda b, h: (b, 0, Z_TILE0 + h)),
                pl.BlockSpec((1, T, LANES), lambda b, h: (b, 0, Z_TILE0 + 1 + h))]

    shared_specs = [pl.BlockSpec((GDN_CONV, dh), lambda b, h: (0, h)),
                    pl.BlockSpec((GDN_CONV, dh), lambda b, h: (0, H + h)),
                    pl.BlockSpec((GDN_CONV, dh), lambda b, h: (0, 2 * H + h)),
                    pl.BlockSpec((1, dh), lambda b, h: (0, 0))]
    rows_c, cols_c = gates_ctx
    rows_l, cols_l = gates_lat
    f32, bf16 = jnp.float32, jnp.bfloat16
    scratch = [pltpu.VMEM((Tl + 2 * CONV_PAD, LANES), f32),
               pltpu.VMEM((Tl, dh), f32), pltpu.VMEM((Tl, dh), f32), pltpu.VMEM((Tl, dh), f32),
               pltpu.VMEM((2, Tl, dh), f32), pltpu.VMEM((2, Tl, dh), bf16), pltpu.VMEM((2, Tl, dh), bf16),
               pltpu.VMEM((2, dh, Tl), bf16), pltpu.VMEM((2, Tl, GDN_CHUNK), bf16),
               pltpu.VMEM((2, Tl // GDN_CHUNK * 8, LANES), f32), pltpu.VMEM((2, Tl, dh), f32),
               pltpu.VMEM((2, 8, Tl), f32)]
    return pl.pallas_call(
        functools.partial(_gdn_body, Tc, Tl),
        grid=(B, H),
        in_specs=seg_specs(Tc) + seg_specs(Tl) + shared_specs,
        out_specs=[pl.BlockSpec((1, Tc, dh), lambda b, h: (b, 0, h)),
                   pl.BlockSpec((1, Tl, dh), lambda b, h: (b, 0, h))],
        out_shape=[jax.ShapeDtypeStruct((B, Tc, GDN_WIDTH), bf16), jax.ShapeDtypeStruct((B, Tl, GDN_WIDTH), bf16)],
        scratch_shapes=scratch,
        compiler_params=_cparams("arbitrary", "arbitrary"), name="gdn_mix",
    )(p_ctx, p_ctx, p_ctx, cols_c, rows_c, p_ctx, p_ctx,
      p_lat, p_lat, p_lat, cols_l, rows_l, p_lat, p_lat,
      conv_w, conv_w, conv_w, gdn_norm.reshape(1, dh))


def _pool_body(x_ref, w_ref, sc_ref, y_ref, pad_sc):
    T = x_ref.shape[1]
    G = POOL_GROUP
    zeros = jnp.zeros((POOL_PAD, G), jnp.float32)
    t = lax.broadcasted_iota(jnp.int32, (ROW_BLOCK, 1), 0)
    for i, win in enumerate(POOL_WINDOWS):
        half = win // 2
        lo = (STATE_W + i * G) // LANES * LANES
        off = STATE_W + i * G - lo
        pad_sc[0:POOL_PAD, :] = zeros
        pad_sc[POOL_PAD:POOL_PAD + T, :] = x_ref[0, :, lo:lo + G + LANES][:, off:off + G]
        pad_sc[POOL_PAD + T:2 * POOL_PAD + T, :] = zeros
        wb = w_ref[i].astype(jnp.bfloat16)
        for r0 in range(0, T, ROW_BLOCK):
            tot = jnp.zeros((ROW_BLOCK, G), jnp.float32)
            for s in range(-half, half):
                tot = tot + pad_sc[POOL_PAD + r0 + s:POOL_PAD + r0 + s + ROW_BLOCK, :]
            cnt = (jnp.minimum(t + r0 + half, T) - jnp.maximum(t + r0 - half, 0)).astype(jnp.float32)
            centred = tot / cnt - pad_sc[POOL_PAD + r0:POOL_PAD + r0 + ROW_BLOCK, :]
            y = jnp.dot(centred.astype(jnp.bfloat16), wb, preferred_element_type=jnp.float32)
            y_ref[0, r0:r0 + ROW_BLOCK, i * G:(i + 1) * G] = (y * sc_ref[:, i * G:(i + 1) * G]).astype(y_ref.dtype)


def pool_mix(p, pool_w, pool_scale):
    B, T, _ = p.shape
    n = len(POOL_WINDOWS)
    assert QKV_COLS % POOL_IN_W == 0 and STATE_W + POOL_WIDTH <= POOL_IN_W
    return pl.pallas_call(
        _pool_body,
        grid=(B,),
        in_specs=[pl.BlockSpec((1, T, POOL_IN_W), lambda b: (b, 0, QKV_COLS // POOL_IN_W)),
                  pl.BlockSpec((n, POOL_GROUP, POOL_GROUP), lambda b: (0, 0, 0)),
                  pl.BlockSpec((1, POOL_WIDTH), lambda b: (0, 0))],
        out_specs=pl.BlockSpec((1, T, POOL_WIDTH), lambda b: (b, 0, 0)),
        out_shape=jax.ShapeDtypeStruct((B, T, POOL_WIDTH), jnp.bfloat16),
        scratch_shapes=[pltpu.VMEM((T + 2 * POOL_PAD, POOL_GROUP), jnp.float32)],
        compiler_params=_cparams("arbitrary"), name="pool_mix",
    )(p, pool_w, pool_scale.reshape(1, POOL_WIDTH))


def pool_gdn_mixer(h_ctx, h_lat, w_in, j, pool_w, pool_scale, conv_w, a_log_f, a_log_b, dt_bias_f, dt_bias_b,
                   gdn_norm):
    B, Tl, D = h_lat.shape
    Tc = h_ctx.shape[1]
    w_in_t = jnp.swapaxes(w_in, 1, 2)
    p_lat = proj(h_lat.reshape(B * Tl, D), w_in_t, j, ncols=IN_COLS_PADDED, tm=1024,
                 w_out_major=True).reshape(B, Tl, IN_COLS_PADDED)
    p_ctx = proj(h_ctx.reshape(B * Tc, D), w_in_t, j, ncols=IN_COLS_PADDED, tm=1024,
                 w_out_major=True).reshape(B, Tc, IN_COLS_PADDED)
    gates_ctx = gdn_gates(p_ctx, a_log_f, a_log_b, dt_bias_f, dt_bias_b)
    gates_lat = gdn_gates(p_lat, a_log_f, a_log_b, dt_bias_f, dt_bias_b)
    g_ctx, g_lat = gdn_mix(p_ctx, gates_ctx, p_lat, gates_lat, conv_w, gdn_norm)
    return [pool_mix(p_ctx, pool_w, pool_scale), g_ctx], [pool_mix(p_lat, pool_w, pool_scale), g_lat]


def axial_rope_tables(rows):
    pos_row = jnp.repeat(jnp.arange(rows), GRID_W).astype(jnp.float32)
    pos_col = jnp.tile(jnp.arange(GRID_W), rows).astype(jnp.float32)
    quarter = DIFF_HEAD_DIM // 4
    inv_freq = ROPE_BASE ** (-jnp.arange(quarter, dtype=jnp.float32) / quarter)
    ang_r = pos_row[:, None] * inv_freq[None, :]
    ang_c = pos_col[:, None] * inv_freq[None, :]
    ang = jnp.concatenate([ang_r, ang_r, ang_c, ang_c], axis=-1)
    return jnp.cos(ang), jnp.sin(ang)


def _rope(x, cos, sin_signed):
    d = x.shape[-1]
    first = (lax.broadcasted_iota(jnp.int32, x.shape, 1) & (d // 2 - 1)) < d // 4
    partner = jnp.where(first, pltpu.roll(x, d - d // 4, 1), pltpu.roll(x, d // 4, 1))
    return x * cos + partner * sin_signed


def _diff_attn_body(lambda_init, tq, q_ref, kl_ref, vl_ref, kc_ref, vc_ref, cos_ref, sin_ref, lam_ref, g_ref,
                    o_ref, k_sc, v_sc):
    d = DIFF_HEAD_DIM
    L = kl_ref.shape[1]
    scale = d ** -0.5
    lv = lam_ref[...]
    lam = (jnp.exp(jnp.sum(lv[0:1] * lv[1:2], axis=1, keepdims=True))
           - jnp.exp(jnp.sum(lv[2:3] * lv[3:4], axis=1, keepdims=True)) + lambda_init)
    cos = cos_ref[...]
    sin = sin_ref[...]
    for m in range(2):
        k_sc[0:L, m * d:(m + 1) * d] = _rope(kl_ref[0, :, m * d:(m + 1) * d], cos, sin).astype(jnp.bfloat16)
    k_sc[L:, :] = kc_ref[0].astype(jnp.bfloat16)
    v_sc[0:L, :] = vl_ref[0].astype(jnp.bfloat16)
    v_sc[L:, :] = vc_ref[0].astype(jnp.bfloat16)

    def scores(r0, m):
        q = _rope(q_ref[0, pl.ds(r0, tq), m * d:(m + 1) * d], cos_ref[pl.ds(r0, tq), :],
                  sin_ref[pl.ds(r0, tq), :]).astype(jnp.bfloat16)
        return lax.dot_general(q, k_sc[:, m * d:(m + 1) * d], (((1,), (1,)), ((), ())),
                               preferred_element_type=jnp.float32)

    def finish(r0, s_pair):
        probs = []
        for s in s_pair:
            e = jnp.exp2((s - jnp.max(s, axis=-1, keepdims=True)) * (scale * math.log2(math.e)))
            probs.append(e * (1.0 / jnp.sum(e, axis=-1, keepdims=True)))
        a = (probs[0] - lam * probs[1]).astype(jnp.bfloat16)
        o = jnp.dot(a, v_sc[...], preferred_element_type=jnp.float32)
        o = o * lax.rsqrt(jnp.mean(o * o, axis=-1, keepdims=True) + NORM_EPS) * g_ref[...]
        o_ref[0, pl.ds(r0, tq), :] = (o * (1 - lambda_init)).astype(o_ref.dtype)

    def q_tile_pair(t, carry):
        ra = pl.multiple_of(2 * t * tq, tq)
        rb = pl.multiple_of(ra + tq, tq)
        sa = [scores(ra, m) for m in range(2)]
        sb = [scores(rb, m) for m in range(2)]
        finish(ra, sa)
        finish(rb, sb)
        return carry

    lax.fori_loop(0, L // (2 * tq), q_tile_pair, 0)


def diff_attention_lat(qkv_lat, kv_ctx, cos, sin_signed, lam_vecs, subln, lambda_init, tq=256):
    B, L, _ = qkv_lat.shape
    Lc = kv_ctx.shape[1]
    H, hd = DIFF_HEADS, 2 * DIFF_HEAD_DIM
    tq = min(tq, L // 2)
    assert L % (2 * tq) == 0
    return pl.pallas_call(
        functools.partial(_diff_attn_body, lambda_init, tq),
        grid=(B, H),
        in_specs=[pl.BlockSpec((1, L, hd), lambda b, h: (b, 0, h)),
                  pl.BlockSpec((1, L, hd), lambda b, h: (b, 0, H + h)),
                  pl.BlockSpec((1, L, hd), lambda b, h: (b, 0, 2 * H + h)),
                  pl.BlockSpec((1, Lc, hd), lambda b, h: (b, 0, h)),
                  pl.BlockSpec((1, Lc, hd), lambda b, h: (b, 0, H + h)),
                  pl.BlockSpec((L, DIFF_HEAD_DIM), lambda b, h: (0, 0)),
                  pl.BlockSpec((L, DIFF_HEAD_DIM), lambda b, h: (0, 0)),
                  pl.BlockSpec((4, DIFF_HEAD_DIM), lambda b, h: (0, 0)),
                  pl.BlockSpec((1, hd), lambda b, h: (0, 0))],
        out_specs=pl.BlockSpec((1, L, hd), lambda b, h: (b, 0, h)),
        out_shape=jax.ShapeDtypeStruct((B, L, H * hd), jnp.bfloat16),
        scratch_shapes=[pltpu.VMEM((L + Lc, hd), jnp.bfloat16), pltpu.VMEM((L + Lc, hd), jnp.bfloat16)],
        compiler_params=_cparams("arbitrary", "arbitrary"), name="diff_attention",
    )(qkv_lat, qkv_lat, qkv_lat, kv_ctx, kv_ctx, cos, sin_signed, lam_vecs, subln.reshape(1, hd))


def diff_attn_mixer(h_ctx, h_lat, w_qkv, j, lam_q1, lam_k1, lam_q2, lam_k2, subln, lambda_init, cos, sin):
    B, L, D = h_lat.shape
    Lc = h_ctx.shape[1]
    qkv = proj(h_lat.reshape(B * L, D), w_qkv, j, tm=1024).reshape(B, L, 3 * D)
    kv_ctx = proj(h_ctx.reshape(B * Lc, D), w_qkv, j, col0=D, tm=1024).reshape(B, Lc, 2 * D)
    quarter = DIFF_HEAD_DIM // 4
    first = (jnp.arange(DIFF_HEAD_DIM) % (2 * quarter)) < quarter
    sin_signed = jnp.where(first[None, :], -sin, sin)
    lam_vecs = jnp.stack([lam_q1, lam_k1, lam_q2, lam_k2]).astype(jnp.float32)
    return diff_attention_lat(qkv, kv_ctx, cos, sin_signed, lam_vecs, subln, lambda_init)


def _count_before(mask):
    R, L = mask.shape
    W = PREFIX_BLOCK
    m = mask.astype(jnp.float32)
    before = (lax.broadcasted_iota(jnp.int32, (W, W), 0) < lax.broadcasted_iota(jnp.int32, (W, W), 1)
              ).astype(jnp.bfloat16)
    carry = jnp.zeros((R, 1), jnp.float32)
    out = []
    for c in range(L // W):
        blk = m[:, c * W:(c + 1) * W]
        out.append(jnp.dot(blk.astype(jnp.bfloat16), before, preferred_element_type=jnp.float32) + carry)
        carry = carry + jnp.sum(blk, axis=1, keepdims=True)
    return out[0] if len(out) == 1 else jnp.concatenate(out, axis=1)


def _select_body(cap, aff_ref, pos_e_ref, gate_e_ref, pos_t_ref):
    L = aff_ref.shape[1]
    a = aff_ref[0].T[:N_EXPERTS]
    bits = pltpu.bitcast(a, jnp.int32)
    thr = jnp.zeros((N_EXPERTS, 1), jnp.int32)
    for bit in range(30, -1, -1):
        cand = thr | (1 << bit)
        cnt = jnp.sum((bits >= cand).astype(jnp.float32), axis=1, keepdims=True)
        thr = jnp.where(cnt >= cap, cand, thr)
    gt = bits > thr
    eq = bits == thr
    need = cap - jnp.sum(gt.astype(jnp.float32), axis=1, keepdims=True)
    sel = gt | (eq & (_count_before(eq) < need))
    pos = jnp.where(sel, _count_before(sel).astype(jnp.int32), -1)
    pos_e_ref[0] = pos
    gate_e_ref[0] = jnp.where(sel, a, 0.0)
    pad = jnp.full((LANES - N_EXPERTS, L), -1, jnp.int32)
    pos_t_ref[0] = jnp.concatenate([pos, pad], axis=0).T


def ec_select(aff, cap):
    B, L, _ = aff.shape
    return pl.pallas_call(
        functools.partial(_select_body, cap),
        grid=(B,),
        in_specs=[pl.BlockSpec((1, L, LANES), lambda b: (b, 0, 0))],
        out_specs=[pl.BlockSpec((1, N_EXPERTS, L), lambda b: (b, 0, 0)),
                   pl.BlockSpec((1, N_EXPERTS, L), lambda b: (b, 0, 0)),
                   pl.BlockSpec((1, L, LANES), lambda b: (b, 0, 0))],
        out_shape=[jax.ShapeDtypeStruct((B, N_EXPERTS, L), jnp.int32),
                   jax.ShapeDtypeStruct((B, N_EXPERTS, L), jnp.float32),
                   jax.ShapeDtypeStruct((B, L, LANES), jnp.int32)],
        compiler_params=_cparams("arbitrary"), name="ec_select",
    )(aff)


def _gather_body(h_ref, pos_e_ref, gate_e_ref, xe_ref, gc_ref):
    e = pl.program_id(1)
    cap = xe_ref.shape[2]
    L = h_ref.shape[1]
    pos_row = pos_e_ref[0, pl.ds(e, 1), :]
    gate_row = gate_e_ref[0, pl.ds(e, 1), :]
    hit = lax.broadcasted_iota(jnp.int32, (cap, L), 0) == pos_row
    xe_ref[0, 0] = jnp.dot(hit.astype(jnp.bfloat16), h_ref[0],
                           preferred_element_type=jnp.float32).astype(jnp.bfloat16)
    gc_ref[0, 0] = jnp.sum(jnp.where(hit, gate_row, 0.0), axis=1, keepdims=True)


def ec_gather(h, pos_e, gate_e, cap):
    B, L, D = h.shape
    return pl.pallas_call(
        _gather_body,
        grid=(B, N_EXPERTS),
        in_specs=[pl.BlockSpec((1, L, D), lambda b, e: (b, 0, 0)),
                  pl.BlockSpec((1, N_EXPERTS, L), lambda b, e: (b, 0, 0)),
                  pl.BlockSpec((1, N_EXPERTS, L), lambda b, e: (b, 0, 0))],
        out_specs=[pl.BlockSpec((1, 1, cap, D), lambda b, e: (b, e, 0, 0)),
                   pl.BlockSpec((1, 1, cap, 1), lambda b, e: (b, e, 0, 0))],
        out_shape=[jax.ShapeDtypeStruct((B, N_EXPERTS, cap, D), jnp.bfloat16),
                   jax.ShapeDtypeStruct((B, N_EXPERTS, cap, 1), jnp.float32)],
        compiler_params=_cparams("arbitrary", "arbitrary"), name="ec_gather",
    )(h, pos_e, gate_e)


def _experts_body(nf, n_groups, *refs):
    xg = refs[:2 * n_groups]
    w1_ref, w3_ref, w2_ref = refs[2 * n_groups:2 * n_groups + 3]
    ye_refs = refs[2 * n_groups + 3:3 * n_groups + 3]
    hid_refs = refs[3 * n_groups + 3:]
    s = pl.program_id(1)

    @pl.when(s < nf)
    def _():
        w1b = w1_ref[0].astype(jnp.bfloat16)
        w3b = w3_ref[0].astype(jnp.bfloat16)
        for g in range(n_groups):
            xe_ref = xg[2 * g]
            B, _, cap, D = xe_ref.shape
            x = xe_ref[...].reshape(B * cap, D)
            h1 = jnp.dot(x, w1b, preferred_element_type=jnp.float32)
            h3 = jnp.dot(x, w3b, preferred_element_type=jnp.float32)
            hid_refs[g][s] = (h1 * jax.nn.sigmoid(h1) * h3).astype(jnp.bfloat16)

    @pl.when(s >= nf)
    def _():
        w2b = w2_ref[0].astype(jnp.bfloat16)
        for g in range(n_groups):
            hid_ref, gc_ref, ye_ref = hid_refs[g], xg[2 * g + 1], ye_refs[g]
            hid = jnp.concatenate([hid_ref[f] for f in range(nf)], axis=1) if nf > 1 else hid_ref[0]
            y = jnp.dot(hid, w2b, preferred_element_type=jnp.float32)
            y = y * gc_ref[...].reshape(y.shape[0], 1)
            ye_ref[...] = y.astype(jnp.bfloat16).reshape(ye_ref.shape)


def ec_experts(groups, w1, w3, w2, layer, tf=256, tn=1024):
    _, E, _, D = groups[0][0].shape
    F = w1.shape[3]
    tf = min(tf, F)
    tn = min(tn, D)
    nf, nn = F // tf, D // tn
    in_specs, args, out_specs, out_shape, scratch = [], [], [], [], []
    for xe, gc in groups:
        B, _, cap, _ = xe.shape
        in_specs += [pl.BlockSpec((B, 1, cap, D), lambda e, s: (0, e, 0, 0), pipeline_mode=pl.Buffered(1)),
                     pl.BlockSpec((B, 1, cap, 1), lambda e, s: (0, e, 0, 0))]
        args += [xe, gc]
        out_specs.append(pl.BlockSpec((B, 1, cap, tn), lambda e, s: (0, e, 0, jnp.maximum(s - nf, 0))))
        out_shape.append(jax.ShapeDtypeStruct((B, E, cap, D), jnp.bfloat16))
        scratch.append(pltpu.VMEM((nf, B * cap, tf), jnp.bfloat16))
    in_specs += [pl.BlockSpec((None, 1, D, tf), lambda e, s: (layer, e, 0, jnp.minimum(s, nf - 1))),
                 pl.BlockSpec((None, 1, D, tf), lambda e, s: (layer, e, 0, jnp.minimum(s, nf - 1))),
                 pl.BlockSpec((None, 1, F, tn), lambda e, s: (layer, e, 0, jnp.maximum(s - nf, 0)))]
    return pl.pallas_call(
        functools.partial(_experts_body, nf, len(groups)),
        grid=(E, nf + nn),
        in_specs=in_specs, out_specs=out_specs, out_shape=out_shape, scratch_shapes=scratch,
        compiler_params=_cparams("arbitrary", "arbitrary"), name="ec_experts",
    )(*args, w1, w3, w2)


def _combine_body(ctx_row, cap, pos_t_ref, ye_ref, x_ref, gate_ref, o_ref):
    row = pl.program_id(0) if ctx_row is None else ctx_row
    K = ye_ref.shape[1]
    assert cap & (cap - 1) == 0 and cap <= 256
    col_expert = lax.broadcasted_iota(jnp.int32, (LANES, K), 1) >> (cap.bit_length() - 1)
    spread = (col_expert == lax.broadcasted_iota(jnp.int32, (LANES, K), 0)).astype(jnp.bfloat16)
    pos = jnp.dot(pos_t_ref[0].astype(jnp.float32).astype(jnp.bfloat16), spread,
                  preferred_element_type=jnp.float32)
    col_slot = (lax.broadcasted_iota(jnp.int32, (1, K), 1) & (cap - 1)).astype(jnp.float32)
    hit = (pos == col_slot).astype(jnp.bfloat16)
    y = jnp.dot(hit, ye_ref[0], preferred_element_type=jnp.float32)
    o_ref[0] = x_ref[0] + _mod_row(gate_ref, row) * y


def ec_combine(x, pos_t, ye, mod, k_gate, cap, ctx_row=None, tt=512, tn=1024):
    B, L, D = x.shape
    tt = min(tt, L)
    tn = min(tn, D)
    K = N_EXPERTS * cap
    return pl.pallas_call(
        functools.partial(_combine_body, ctx_row, cap),
        grid=(B, D // tn, L // tt),
        in_specs=[pl.BlockSpec((1, tt, LANES), lambda b, n, t: (b, t, 0)),
                  pl.BlockSpec((1, K, tn), lambda b, n, t: (b, 0, n)),
                  pl.BlockSpec((1, tt, tn), lambda b, n, t: (b, t, n)),
                  pl.BlockSpec((MOD_ROWS, tn), lambda b, n, t: (0, k_gate * (D // tn) + n))],
        out_specs=pl.BlockSpec((1, tt, tn), lambda b, n, t: (b, t, n)),
        out_shape=jax.ShapeDtypeStruct((B, L, D), jnp.float32),
        compiler_params=_cparams("arbitrary", "arbitrary", "arbitrary"), name="ec_combine",
    )(pos_t, ye.reshape(B, K, D), x, mod)


def moe_blocks(streams, gain, mod, router, w1, w3, w2, layer):
    routed = []
    for x, ctx_row in streams:
        cap = EC_CAPACITY * x.shape[1] // N_EXPERTS
        h, aff = normmod(x, gain, mod, 3, 4, ctx_row=ctx_row, router=router)
        pos_e, gate_e, pos_t = ec_select(aff, cap)
        routed.append((ec_gather(h, pos_e, gate_e, cap), pos_t, cap))
    ye = ec_experts([r[0] for r in routed], w1, w3, w2, layer)
    return [ec_combine(x, pos_t, y, mod, 5, cap, ctx_row=ctx_row)
            for (x, ctx_row), (_, pos_t, cap), y in zip(streams, routed, ye)]


def kernel(x, c, ctx, c_ctx, ada_w, ada_b, norm_mix, norm_ffn, w_in, pool_w, pool_scale, conv_w, a_log_f, a_log_b, dt_bias_f, dt_bias_b, gdn_norm, w_out_ab, w_qkv, lam_q1, lam_k1, lam_q2, lam_k2, subln, w_out_c, router, w1, w3, w2, final_norm):
    depth = ada_w.shape[0]
    B, L, D = x.shape
    Lc = ctx.shape[1]
    assert B < MOD_ROWS
    ctx_row = B
    cos, sin = axial_rope_tables(L // GRID_W)
    cond = jnp.zeros((MOD_ROWS, D), jnp.float32).at[:B].set(jax.nn.silu(c)).at[ctx_row].set(jax.nn.silu(c_ctx))
    cond = cond.astype(jnp.bfloat16)
    x_lat, x_ctx = x, ctx
    for layer in range(depth):
        last = layer == depth - 1
        j = layer // 2
        mod = proj(cond, ada_w, layer) + ada_b[layer]
        h_lat = normmod(x_lat, norm_mix[layer], mod, 0, 1)
        h_ctx = normmod(x_ctx, norm_mix[layer], mod, 0, 1, ctx_row=ctx_row)
        if layer % 2 == 0:
            y_ctx, y_lat = pool_gdn_mixer(h_ctx, h_lat, w_in, j, pool_w[j], pool_scale[j], conv_w[j],
                                          a_log_f[j], a_log_b[j], dt_bias_f[j], dt_bias_b[j], gdn_norm[j])
            w_out = w_out_ab
        else:
            assert last, "context outputs of an attention layer are only skipped when no later layer reads them"
            lambda_init = 0.8 - 0.6 * math.exp(-0.3 * layer)
            y_ctx = None
            y_lat = [diff_attn_mixer(h_ctx, h_lat, w_qkv, j, lam_q1[j], lam_k1[j], lam_q2[j], lam_k2[j],
                                     subln[j], lambda_init, cos, sin)]
            w_out = w_out_c
        x_lat = proj([p.reshape(B * L, -1) for p in y_lat], w_out, j, tm=1024,
                     residual=(x_lat.reshape(B * L, D), mod, 2, L, None)).reshape(B, L, D)
        streams = [(x_lat, None)]
        if not last:
            x_ctx = proj([p.reshape(B * Lc, -1) for p in y_ctx], w_out, j, tm=1024,
                         residual=(x_ctx.reshape(B * Lc, D), mod, 2, Lc, ctx_row)).reshape(B, Lc, D)
            streams.append((x_ctx, ctx_row))
        outs = moe_blocks(streams, norm_ffn[layer], mod, router[layer], w1, w3, w2, layer)
        x_lat = outs[0]
        if not last:
            x_ctx = outs[1]
    return final_rmsnorm(x_lat, final_norm)
```

```python
import functools
import math

import jax
import jax.numpy as jnp
from jax import lax
from jax.experimental import pallas as pl
from jax.experimental.pallas import tpu as pltpu

D_MODEL = 4096
GRID_W = 64
NORM_EPS = 1e-6
POOL_WINDOWS = (2, 4, 8, 16)
POOL_GROUP = D_MODEL // 16
POOL_WIDTH = len(POOL_WINDOWS) * POOL_GROUP
GDN_HEAD_DIM = 128
GDN_HEADS = (D_MODEL - POOL_WIDTH) // GDN_HEAD_DIM
GDN_WIDTH = GDN_HEADS * GDN_HEAD_DIM
GDN_CONV = 5
GDN_CHUNK = 64
QKV_COLS = 3 * GDN_WIDTH
DIFF_HEAD_DIM = 128
DIFF_HEADS = D_MODEL // (2 * DIFF_HEAD_DIM)
ROPE_BASE = 10000.0
N_EXPERTS = 16
EC_CAPACITY = 2

LANES = 128
MOD_ROWS = 8
PREFIX_BLOCK = 256
VMEM_LIMIT_BYTES = 56 * 1024 * 1024

PROJ_TN = 512
GDN_PAIR = 2 * GDN_CHUNK
GDN_PAIRS_PER_STEP = 4
GDN_SOLVE_BASE = 4
CONV_PAD = 8
ROW_BLOCK = 128
STATE_W = 4 * GDN_HEADS
REST_TILE0 = QKV_COLS // LANES
IN_COLS_PADDED = -(-(QKV_COLS + STATE_W + POOL_WIDTH + GDN_WIDTH) // PROJ_TN) * PROJ_TN
Z_TILE0 = REST_TILE0 + (STATE_W + POOL_WIDTH) // LANES
Z_SHIFT = LANES - (STATE_W + POOL_WIDTH) % LANES
POOL_PAD = 16
POOL_IN_W = 1152


def _cparams(*sem):
    return pltpu.CompilerParams(dimension_semantics=sem, vmem_limit_bytes=VMEM_LIMIT_BYTES)


def _mod_row(mod_ref, row):
    return mod_ref[pl.ds(row, 1), :]


def _proj_body(n_parts, rows_per_sample, ctx_row, valid_cols, w_out_major, *refs):
    a_refs, w_ref = refs[:n_parts], refs[n_parts]
    if rows_per_sample is None:
        o_ref, wb_ref = refs[n_parts + 1:]
    else:
        x_ref, gate_ref, o_ref, wb_ref = refs[n_parts + 1:]
    i = pl.program_id(1)
    out_axis = 0 if w_out_major else 1

    @pl.when(i == 0)
    def _():
        w = w_ref[...]
        if valid_cols is not None:
            col = pl.program_id(0) * w.shape[out_axis] + lax.broadcasted_iota(jnp.int32, w.shape, out_axis)
            w = jnp.where(col < valid_cols, w, 0.0)
        wb_ref[...] = w.astype(jnp.bfloat16)

    y, k0 = None, 0
    for a_ref in a_refs:
        k1 = k0 + a_ref.shape[1]
        if w_out_major:
            part = _nt(a_ref[...], wb_ref[:, k0:k1])
        else:
            part = jnp.dot(a_ref[...], wb_ref[k0:k1, :], preferred_element_type=jnp.float32)
        y = part if y is None else y + part
        k0 = k1
    if rows_per_sample is None:
        o_ref[...] = y.astype(o_ref.dtype)
    else:
        row = (i * o_ref.shape[0]) // rows_per_sample if ctx_row is None else ctx_row
        o_ref[...] = x_ref[...] + _mod_row(gate_ref, row) * y


def proj(a, w, layer, *, col0=0, ncols=None, tm=512, tn=PROJ_TN, out_dtype=jnp.float32, residual=None,
         w_out_major=False):
    parts = list(a) if isinstance(a, (list, tuple)) else [a]
    M = parts[0].shape[0]
    K = sum(p.shape[1] for p in parts)
    n_out = w.shape[1] if w_out_major else w.shape[2]
    ncols = n_out - col0 if ncols is None else ncols
    tm = min(tm, M)
    assert M % tm == 0 and ncols % tn == 0 and col0 % tn == 0, (M, ncols, col0, tm, tn)
    j0 = col0 // tn
    valid_cols = n_out - col0 if col0 + ncols > n_out else None
    in_specs = [pl.BlockSpec((tm, p.shape[1]), lambda j, i: (i, 0)) for p in parts]
    if w_out_major:
        in_specs.append(pl.BlockSpec((None, tn, K), lambda j, i: (layer, j0 + j, 0)))
    else:
        in_specs.append(pl.BlockSpec((None, K, tn), lambda j, i: (layer, 0, j0 + j)))
    args = parts + [w]
    rows_per_sample = ctx_row = None
    if residual is not None:
        x, mod, k_gate, rows_per_sample, ctx_row = residual
        assert ctx_row is not None or rows_per_sample % tm == 0
        in_specs += [pl.BlockSpec((tm, tn), lambda j, i: (i, j)),
                     pl.BlockSpec((MOD_ROWS, tn), lambda j, i: (0, k_gate * (ncols // tn) + j))]
        args += [x, mod]
    return pl.pallas_call(
        functools.partial(_proj_body, len(parts), rows_per_sample, ctx_row, valid_cols, w_out_major),
        grid=(ncols // tn, M // tm),
        in_specs=in_specs,
        out_specs=pl.BlockSpec((tm, tn), lambda j, i: (i, j)),
        out_shape=jax.ShapeDtypeStruct((M, ncols), out_dtype),
        scratch_shapes=[pltpu.VMEM((tn, K) if w_out_major else (K, tn), jnp.bfloat16)],
        compiler_params=_cparams("arbitrary", "arbitrary"), name="proj",
    )(*args)


def _normmod_body(ctx_row, with_router, x_ref, g_ref, shift_ref, scale_ref, *rest):
    row = pl.program_id(0) if ctx_row is None else ctx_row
    x = x_ref[0]
    y = x * lax.rsqrt(jnp.mean(x * x, axis=-1, keepdims=True) + NORM_EPS) * g_ref[...]
    h = (y * (1.0 + _mod_row(scale_ref, row)) + _mod_row(shift_ref, row)).astype(jnp.bfloat16)
    if not with_router:
        (h_ref,) = rest
        h_ref[0] = h
        return
    r_ref, h_ref, aff_ref = rest
    h_ref[0] = h
    logits = jnp.dot(h, r_ref[...], preferred_element_type=jnp.float32)
    lane = lax.broadcasted_iota(jnp.int32, logits.shape, 1)
    logits = jnp.where(lane < N_EXPERTS, logits, -jnp.inf)
    e = jnp.exp(logits - jnp.max(logits, axis=-1, keepdims=True))
    aff_ref[0] = e / jnp.sum(e, axis=-1, keepdims=True)


def normmod(x, gain, mod, k_shift, k_scale, ctx_row=None, router=None, tr=256):
    B, L, D = x.shape
    tr = min(tr, L)
    in_specs = [pl.BlockSpec((1, tr, D), lambda b, t: (b, t, 0)),
                pl.BlockSpec((1, D), lambda b, t: (0, 0)),
                pl.BlockSpec((MOD_ROWS, D), lambda b, t: (0, k_shift)),
                pl.BlockSpec((MOD_ROWS, D), lambda b, t: (0, k_scale))]
    args = [x, gain.reshape(1, D), mod, mod]
    out_specs = [pl.BlockSpec((1, tr, D), lambda b, t: (b, t, 0))]
    out_shape = [jax.ShapeDtypeStruct((B, L, D), jnp.bfloat16)]
    if router is not None:
        rp = jnp.pad(router.astype(jnp.bfloat16), ((0, 0), (0, LANES - router.shape[1])))
        in_specs.append(pl.BlockSpec((D, LANES), lambda b, t: (0, 0)))
        args.append(rp)
        out_specs.append(pl.BlockSpec((1, tr, LANES), lambda b, t: (b, t, 0)))
        out_shape.append(jax.ShapeDtypeStruct((B, L, LANES), jnp.float32))
    out = pl.pallas_call(
        functools.partial(_normmod_body, ctx_row, router is not None),
        grid=(B, L // tr), in_specs=in_specs, out_specs=out_specs, out_shape=out_shape,
        compiler_params=_cparams("arbitrary", "arbitrary"), name="normmod",
    )(*args)
    return out if router is not None else out[0]


def _final_norm_body(x_ref, g_ref, o_ref):
    x = x_ref[...]
    o_ref[...] = x * lax.rsqrt(jnp.mean(x * x, axis=-1, keepdims=True) + NORM_EPS) * g_ref[...]


def final_rmsnorm(x, gain, tr=512):
    B, L, D = x.shape
    M = B * L
    tr = min(tr, M)
    return pl.pallas_call(
        _final_norm_body,
        grid=(M // tr,),
        in_specs=[pl.BlockSpec((tr, D), lambda i: (i, 0)), pl.BlockSpec((1, D), lambda i: (0, 0))],
        out_specs=pl.BlockSpec((tr, D), lambda i: (i, 0)),
        out_shape=jax.ShapeDtypeStruct((M, D), x.dtype),
        compiler_params=_cparams("arbitrary"), name="final_rmsnorm",
    )(x.reshape(M, D), gain.reshape(1, D)).reshape(B, L, D)


def _gates_body(st_ref, alog_ref, dtb_ref, rows_ref, cols_ref):
    T = st_ref.shape[1]
    H = GDN_HEADS
    x = st_ref[0].T
    t = x + dtb_ref[...]
    softplus = jnp.maximum(t, 0.0) + jnp.log1p(jnp.exp(-jnp.abs(t)))
    r_full = lax.broadcasted_iota(jnp.int32, x.shape, 0)
    val = jnp.where(r_full < 2 * H, -jnp.exp(alog_ref[...]) * softplus, jax.nn.sigmoid(x))
    pos = lax.broadcasted_iota(jnp.int32, (LANES, LANES), 1) & (GDN_CHUNK - 1)
    r = lax.broadcasted_iota(jnp.int32, (LANES, LANES), 0)
    tiles = []
    for c in range(T // LANES):
        v = val[:, c * LANES:(c + 1) * LANES]
        pre, suf, s = v, v, 1
        while s < GDN_CHUNK:
            pre = pre + jnp.where(pos >= s, pltpu.roll(pre, s, 1), 0.0)
            suf = suf + jnp.where(pos < GDN_CHUNK - s, pltpu.roll(suf, LANES - s, 1), 0.0)
            s *= 2
        tiles.append(jnp.where(r < H, pre, jnp.where(r < 2 * H, suf, v)))
    out = jnp.concatenate(tiles, axis=1) if len(tiles) > 1 else tiles[0]
    rows_ref[0] = out
    cols_ref[0] = out.T


def gdn_gates(p, a_log_f, a_log_b, dt_bias_f, dt_bias_b):
    B, T, _ = p.shape
    pad = jnp.zeros((LANES - 2 * GDN_HEADS,), jnp.float32)
    alog = jnp.concatenate([a_log_f, a_log_b, pad]).astype(jnp.float32).reshape(LANES, 1)
    dtb = jnp.concatenate([dt_bias_f, dt_bias_b, pad]).astype(jnp.float32).reshape(LANES, 1)
    return pl.pallas_call(
        _gates_body,
        grid=(B,),
        in_specs=[pl.BlockSpec((1, T, LANES), lambda b: (b, 0, REST_TILE0)),
                  pl.BlockSpec((LANES, 1), lambda b: (0, 0)),
                  pl.BlockSpec((LANES, 1), lambda b: (0, 0))],
        out_specs=[pl.BlockSpec((1, LANES, T), lambda b: (b, 0, 0)),
                   pl.BlockSpec((1, T, LANES), lambda b: (b, 0, 0))],
        out_shape=[jax.ShapeDtypeStruct((B, LANES, T), jnp.float32),
                   jax.ShapeDtypeStruct((B, T, LANES), jnp.float32)],
        compiler_params=_cparams("arbitrary"), name="gdn_gates",
    )(p, alog, dtb)


def _conv_silu(x_ref, w_ref, pad_sc, dst_sc, T, unit_norm):
    half = GDN_CONV // 2
    zeros = jnp.zeros((CONV_PAD, LANES), jnp.float32)
    pad_sc[0:CONV_PAD, :] = zeros
    pad_sc[CONV_PAD:CONV_PAD + T, :] = x_ref[0]
    pad_sc[CONV_PAD + T:2 * CONV_PAD + T, :] = zeros
    w = w_ref[...]
    for r0 in range(0, T, ROW_BLOCK):
        y = jnp.zeros((ROW_BLOCK, LANES), jnp.float32)
        for j in range(GDN_CONV):
            s = CONV_PAD + r0 + j - half
            y = y + w[j:j + 1, :] * pad_sc[s:s + ROW_BLOCK, :]
        y = y * jax.nn.sigmoid(y)
        if unit_norm:
            y = y * lax.rsqrt(jnp.sum(y * y, axis=-1, keepdims=True) + NORM_EPS)
        dst_sc[r0:r0 + ROW_BLOCK, :] = y


def _nt(a, b):
    return lax.dot_general(a, b, (((1,), (1,)), ((), ())), preferred_element_type=jnp.float32)


def _gdn_segment(T, h, refs, sc, s_init):
    pq_ref, pk_ref, pv_ref, cols_ref, rows_ref, za_ref, zb_ref, y_ref = refs
    (cwq_ref, cwk_ref, cwv_ref, gn_ref, pad_sc, q_sc, k_sc, v_sc, u_sc, w_sc, qd_sc, kdt_sc, attn_sc, gend_sc,
     o_sc, grow_sc) = sc
    C, P, H = GDN_CHUNK, GDN_PAIR, GDN_HEADS
    n_pairs = T // P
    bf16 = jnp.bfloat16
    scale = GDN_HEAD_DIM ** -0.5
    for d in range(2):
        grow_sc[d, 0:1, 0:T] = rows_ref[0, pl.ds(d * H + h, 1), :]
    _conv_silu(pq_ref, cwq_ref, pad_sc, q_sc, T, True)
    _conv_silu(pk_ref, cwk_ref, pad_sc, k_sc, T, True)
    _conv_silu(pv_ref, cwv_ref, pad_sc, v_sc, T, False)

    lane = lax.broadcasted_iota(jnp.int32, (P, LANES), 1)
    ii = lax.broadcasted_iota(jnp.int32, (C, C), 0)
    jj = lax.broadcasted_iota(jnp.int32, (C, C), 1)

    halves = [slice(c * C, (c + 1) * C) for c in range(2)]

    G = min(GDN_PAIRS_PER_STEP, n_pairs)
    assert n_pairs % G == 0
    n_steps = n_pairs // G

    def units(i):
        fwd = [(0, G * i + g) for g in range(G)]
        return fwd + [(1, n_pairs - 1 - m) for _, m in fwd]

    def mm(a, b):
        return jnp.dot(a, b, preferred_element_type=jnp.float32)


    def prep_stages(i):
        neg_a, rhs, dest, held = [], [], [], []
        for d, m in units(i):
            r0 = pl.multiple_of(m * P, P)
            q2 = q_sc[pl.ds(r0, P), :]
            k2 = k_sc[pl.ds(r0, P), :]
            v2 = v_sc[pl.ds(r0, P), :]
            cols = cols_ref[0, pl.ds(r0, P), :]
            qs = q2 * scale
            kh = [k2[sl].astype(bf16) for sl in halves]
            qk = [_nt(qs[sl].astype(bf16), kh[c]) for c, sl in enumerate(halves)]
            gcol = jnp.sum(jnp.where(lane == d * H + h, cols, 0.0), axis=1, keepdims=True)
            bcol = jnp.sum(jnp.where(lane == (2 + d) * H + h, cols, 0.0), axis=1, keepdims=True)
            eg = jnp.exp(gcol)
            kb = k2 * bcol
            kk = [_nt(kb[sl].astype(bf16), kh[c]) for c, sl in enumerate(halves)]
            held.append((d, m, r0, k2, v2, qs, qk, kk, gcol, bcol, eg, kb))
        yield
        for d, m, r0, k2, v2, qs, qk, kk, gcol, bcol, eg, kb in held:
            grow = grow_sc[d, 0:1, pl.ds(r0, P)]
            keep = (ii >= jj) if d == 0 else (ii <= jj)
            strict = (ii > jj) if d == 0 else (ii < jj)
            rhs_uw = jnp.concatenate([v2 * bcol, kb * eg], axis=1)
            qd_sc[d, pl.ds(r0, P), :] = (qs * eg).astype(bf16)
            kd = []
            for c, sl in enumerate(halves):
                rc = pl.multiple_of(r0 + c * C, C)
                gc = gcol[sl]
                decay = jnp.exp(jnp.where(keep, gc - grow[:, sl], -jnp.inf))
                neg_a.append(jnp.where(strict, -(kk[c] * decay), 0.0))
                rhs.append(rhs_uw[sl])
                dest.append((d, rc))
                attn_sc[d, pl.ds(rc, C), :] = (qk[c] * decay).astype(bf16)
                g_last = gc[C - 1:C] if d == 0 else gc[0:1]
                kd.append(k2[sl] * jnp.exp(g_last - gc))
                gend_sc[d, pl.ds(pl.multiple_of((2 * m + c) * 8, 8), 8), :] = jnp.broadcast_to(
                    jnp.exp(g_last), (8, LANES))
            kdt_sc[d, :, pl.ds(r0, P)] = jnp.concatenate(kd, axis=0).T.astype(bf16)
        s0 = GDN_SOLVE_BASE
        same = lambda s: (ii >> (s.bit_length() - 1)) == (jj >> (s.bit_length() - 1))
        d_ = [jnp.where(same(s0), x, 0.0) for x in neg_a]
        db = [x.astype(bf16) for x in d_]
        p = [mm(x, x) for x in db]
        yield
        dp = [mm(a, b.astype(bf16)) for a, b in zip(db, p)]
        yield
        e = [a + b + ab for a, b, ab in zip(d_, p, dp)]
        s = s0
        while s < C:
            off = same(2 * s) & jnp.logical_not(same(s))
            o = [jnp.where(off, x, 0.0) for x in neg_a]
            oe = [mm(a.astype(bf16), b.astype(bf16)) for a, b in zip(o, e)]
            yield
            m_ = [a + b for a, b in zip(o, oe)]
            em = [mm(a.astype(bf16), b.astype(bf16)) for a, b in zip(e, m_)]
            yield
            e = [a + b + ab for a, b, ab in zip(e, m_, em)]
            s *= 2
        corr = [mm(a.astype(bf16), r.astype(bf16)) for a, r in zip(e, rhs)]
        yield
        for (d, rc), r, cr in zip(dest, rhs, corr):
            s = r + cr
            u_sc[d, pl.ds(rc, C), :] = s[:, :GDN_HEAD_DIM]
            w_sc[d, pl.ds(rc, C), :] = s[:, GDN_HEAD_DIM:].astype(bf16)

    def scan_operands(i):
        ops = ([], [])
        for d, m in units(i):
            r0 = pl.multiple_of(m * P, P)
            kdt = kdt_sc[d, :, pl.ds(r0, P)]
            for c in ((0, 1) if d == 0 else (1, 0)):
                rc = pl.multiple_of(r0 + c * C, C)
                g_end = gend_sc[d, pl.ds(pl.multiple_of((2 * m + c) * 8, 8), 8), :][0:1]
                ops[d].append((d, rc, u_sc[d, pl.ds(rc, C), :], w_sc[d, pl.ds(rc, C), :], qd_sc[d, pl.ds(rc, C), :],
                               attn_sc[d, pl.ds(rc, C), :], kdt[:, c * C:(c + 1) * C], g_end))
        return ops

    def scan_stages(ops, carry, result):
        s = list(carry)
        for k in range(2 * G):
            cur = [ops[d][k] for d in range(2)]
            sb = [s[d].astype(bf16) for d in range(2)]
            ws = [mm(cur[d][3], sb[d]) for d in range(2)]
            qs_ = [mm(cur[d][4], sb[d]) for d in range(2)]
            yield
            vb = [(cur[d][2] - ws[d]).astype(bf16) for d in range(2)]
            av = [mm(cur[d][5], vb[d]) for d in range(2)]
            kv = [mm(cur[d][6], vb[d]) for d in range(2)]
            yield
            for d in range(2):
                o_sc[d, pl.ds(cur[d][1], C), :] = qs_[d] + av[d]
                s[d] = s[d] * cur[d][7] + kv[d]
        result.append(tuple(s))

    def run_interleaved(*gens):
        gens = list(gens)
        while gens:
            for g in list(gens):
                try:
                    next(g)
                except StopIteration:
                    gens.remove(g)

    def step(i, carry):
        result = []
        run_interleaved(prep_stages(i + 1), scan_stages(scan_operands(i), carry, result))
        return result[0]

    run_interleaved(prep_stages(0))
    carry = lax.fori_loop(0, n_steps - 1, step, s_init)
    result = []
    run_interleaved(scan_stages(scan_operands(n_steps - 1), carry, result))
    s_f, s_b = result[0]

    lane_b = lax.broadcasted_iota(jnp.int32, (ROW_BLOCK, LANES), 1)
    for r0 in range(0, T, ROW_BLOCK):
        o = o_sc[0, r0:r0 + ROW_BLOCK, :] + o_sc[1, r0:r0 + ROW_BLOCK, :]
        z = pltpu.roll(jnp.where(lane_b >= LANES - Z_SHIFT, za_ref[0, r0:r0 + ROW_BLOCK, :],
                                 zb_ref[0, r0:r0 + ROW_BLOCK, :]), Z_SHIFT, 1)
        y = o * lax.rsqrt(jnp.mean(o * o, axis=-1, keepdims=True) + NORM_EPS) * gn_ref[...]
        y_ref[0, r0:r0 + ROW_BLOCK, :] = (y * (z * jax.nn.sigmoid(z))).astype(y_ref.dtype)
    return s_f, s_b


def _gdn_body(Tc, Tl, *refs):
    h = pl.program_id(1)
    ctx_refs = refs[0:7]
    lat_refs = refs[7:14]
    shared = refs[14:18]
    y_ctx_ref, y_lat_ref = refs[18:20]
    sc = shared + refs[20:]
    zero = jnp.zeros((GDN_HEAD_DIM, GDN_HEAD_DIM), jnp.float32)
    states = _gdn_segment(Tc, h, ctx_refs + (y_ctx_ref,), sc, (zero, zero))
    _gdn_segment(Tl, h, lat_refs + (y_lat_ref,), sc, states)


def gdn_mix(p_ctx, gates_ctx, p_lat, gates_lat, conv_w, gdn_norm):
    B, Tc, _ = p_ctx.shape
    Tl = p_lat.shape[1]
    H, dh = GDN_HEADS, GDN_HEAD_DIM

    def seg_specs(T):
        return [pl.BlockSpec((1, T, dh), lambda b, h: (b, 0, h)),
                pl.BlockSpec((1, T, dh), lambda b, h: (b, 0, H + h)),
                pl.BlockSpec((1, T, dh), lambda b, h: (b, 0, 2 * H + h)),
                pl.BlockSpec((1, T, LANES), lambda b, h: (b, 0, 0)),
                pl.BlockSpec((1, LANES, T), lambda b, h: (b, 0, 0)),
                pl.BlockSpec((1, T, LANES), lambda b, h: (b, 0, Z_TILE0 + h)),
                pl.BlockSpec((1, T, LANES), lambda b, h: (b, 0, Z_TILE0 + 1 + h))]

    shared_specs = [pl.BlockSpec((GDN_CONV, dh), lambda b, h: (0, h)),
                    pl.BlockSpec((GDN_CONV, dh), lambda b, h: (0, H + h)),
                    pl.BlockSpec((GDN_CONV, dh), lambda b, h: (0, 2 * H + h)),
                    pl.BlockSpec((1, dh), lambda b, h: (0, 0))]
    rows_c, cols_c = gates_ctx
    rows_l, cols_l = gates_lat
    f32, bf16 = jnp.float32, jnp.bfloat16
    scratch = [pltpu.VMEM((Tl + 2 * CONV_PAD, LANES), f32),
               pltpu.VMEM((Tl, dh), f32), pltpu.VMEM((Tl, dh), f32), pltpu.VMEM((Tl, dh), f32),
               pltpu.VMEM((2, Tl, dh), f32), pltpu.VMEM((2, Tl, dh), bf16), pltpu.VMEM((2, Tl, dh), bf16),
               pltpu.VMEM((2, dh, Tl), bf16), pltpu.VMEM((2, Tl, GDN_CHUNK), bf16),
               pltpu.VMEM((2, Tl // GDN_CHUNK * 8, LANES), f32), pltpu.VMEM((2, Tl, dh), f32),
               pltpu.VMEM((2, 8, Tl), f32)]
    return pl.pallas_call(
        functools.partial(_gdn_body, Tc, Tl),
        grid=(B, H),
        in_specs=seg_specs(Tc) + seg_specs(Tl) + shared_specs,
        out_specs=[pl.BlockSpec((1, Tc, dh), lambda b, h: (b, 0, h)),
                   pl.BlockSpec((1, Tl, dh), lambda b, h: (b, 0, h))],
        out_shape=[jax.ShapeDtypeStruct((B, Tc, GDN_WIDTH), bf16), jax.ShapeDtypeStruct((B, Tl, GDN_WIDTH), bf16)],
        scratch_shapes=scratch,
        compiler_params=_cparams("arbitrary", "arbitrary"), name="gdn_mix",
    )(p_ctx, p_ctx, p_ctx, cols_c, rows_c, p_ctx, p_ctx,
      p_lat, p_lat, p_lat, cols_l, rows_l, p_lat, p_lat,
      conv_w, conv_w, conv_w, gdn_norm.reshape(1, dh))


def _pool_body(x_ref, w_ref, sc_ref, y_ref, pad_sc):
    T = x_ref.shape[1]
    G = POOL_GROUP
    zeros = jnp.zeros((POOL_PAD, G), jnp.float32)
    t = lax.broadcasted_iota(jnp.int32, (ROW_BLOCK, 1), 0)
    for i, win in enumerate(POOL_WINDOWS):
        half = win // 2
        lo = (STATE_W + i * G) // LANES * LANES
        off = STATE_W + i * G - lo
        pad_sc[0:POOL_PAD, :] = zeros
        pad_sc[POOL_PAD:POOL_PAD + T, :] = x_ref[0, :, lo:lo + G + LANES][:, off:off + G]
        pad_sc[POOL_PAD + T:2 * POOL_PAD + T, :] = zeros
        wb = w_ref[i].astype(jnp.bfloat16)
        for r0 in range(0, T, ROW_BLOCK):
            tot = jnp.zeros((ROW_BLOCK, G), jnp.float32)
            for s in range(-half, half):
                tot = tot + pad_sc[POOL_PAD + r0 + s:POOL_PAD + r0 + s + ROW_BLOCK, :]
            cnt = (jnp.minimum(t + r0 + half, T) - jnp.maximum(t + r0 - half, 0)).astype(jnp.float32)
            centred = tot / cnt - pad_sc[POOL_PAD + r0:POOL_PAD + r0 + ROW_BLOCK, :]
            y = jnp.dot(centred.astype(jnp.bfloat16), wb, preferred_element_type=jnp.float32)
            y_ref[0, r0:r0 + ROW_BLOCK, i * G:(i + 1) * G] = (y * sc_ref[:, i * G:(i + 1) * G]).astype(y_ref.dtype)


def pool_mix(p, pool_w, pool_scale):
    B, T, _ = p.shape
    n = len(POOL_WINDOWS)
    assert QKV_COLS % POOL_IN_W == 0 and STATE_W + POOL_WIDTH <= POOL_IN_W
    return pl.pallas_call(
        _pool_body,
        grid=(B,),
        in_specs=[pl.BlockSpec((1, T, POOL_IN_W), lambda b: (b, 0, QKV_COLS // POOL_IN_W)),
                  pl.BlockSpec((n, POOL_GROUP, POOL_GROUP), lambda b: (0, 0, 0)),
                  pl.BlockSpec((1, POOL_WIDTH), lambda b: (0, 0))],
        out_specs=pl.BlockSpec((1, T, POOL_WIDTH), lambda b: (b, 0, 0)),
        out_shape=jax.ShapeDtypeStruct((B, T, POOL_WIDTH), jnp.bfloat16),
        scratch_shapes=[pltpu.VMEM((T + 2 * POOL_PAD, POOL_GROUP), jnp.float32)],
        compiler_params=_cparams("arbitrary"), name="pool_mix",
    )(p, pool_w, pool_scale.reshape(1, POOL_WIDTH))


def pool_gdn_mixer(h_ctx, h_lat, w_in, j, pool_w, pool_scale, conv_w, a_log_f, a_log_b, dt_bias_f, dt_bias_b,
                   gdn_norm):
    B, Tl, D = h_lat.shape
    Tc = h_ctx.shape[1]
    w_in_t = jnp.swapaxes(w_in, 1, 2)
    p_lat = proj(h_lat.reshape(B * Tl, D), w_in_t, j, ncols=IN_COLS_PADDED, tm=1024,
                 w_out_major=True).reshape(B, Tl, IN_COLS_PADDED)
    p_ctx = proj(h_ctx.reshape(B * Tc, D), w_in_t, j, ncols=IN_COLS_PADDED, tm=1024,
                 w_out_major=True).reshape(B, Tc, IN_COLS_PADDED)
    gates_ctx = gdn_gates(p_ctx, a_log_f, a_log_b, dt_bias_f, dt_bias_b)
    gates_lat = gdn_gates(p_lat, a_log_f, a_log_b, dt_bias_f, dt_bias_b)
    g_ctx, g_lat = gdn_mix(p_ctx, gates_ctx, p_lat, gates_lat, conv_w, gdn_norm)
    return [pool_mix(p_ctx, pool_w, pool_scale), g_ctx], [pool_mix(p_lat, pool_w, pool_scale), g_lat]


def axial_rope_tables(rows):
    pos_row = jnp.repeat(jnp.arange(rows), GRID_W).astype(jnp.float32)
    pos_col = jnp.tile(jnp.arange(GRID_W), rows).astype(jnp.float32)
    quarter = DIFF_HEAD_DIM // 4
    inv_freq = ROPE_BASE ** (-jnp.arange(quarter, dtype=jnp.float32) / quarter)
    ang_r = pos_row[:, None] * inv_freq[None, :]
    ang_c = pos_col[:, None] * inv_freq[None, :]
    ang = jnp.concatenate([ang_r, ang_r, ang_c, ang_c], axis=-1)
    return jnp.cos(ang), jnp.sin(ang)


def _rope(x, cos, sin_signed):
    d = x.shape[-1]
    first = (lax.broadcasted_iota(jnp.int32, x.shape, 1) & (d // 2 - 1)) < d // 4
    partner = jnp.where(first, pltpu.roll(x, d - d // 4, 1), pltpu.roll(x, d // 4, 1))
    return x * cos + partner * sin_signed


def _diff_attn_body(lambda_init, tq, q_ref, kl_ref, vl_ref, kc_ref, vc_ref, cos_ref, sin_ref, lam_ref, g_ref,
                    o_ref, k_sc, v_sc):
    d = DIFF_HEAD_DIM
    L = kl_ref.shape[1]
    scale = d ** -0.5
    lv = lam_ref[...]
    lam = (jnp.exp(jnp.sum(lv[0:1] * lv[1:2], axis=1, keepdims=True))
           - jnp.exp(jnp.sum(lv[2:3] * lv[3:4], axis=1, keepdims=True)) + lambda_init)
    cos = cos_ref[...]
    sin = sin_ref[...]
    for m in range(2):
        k_sc[0:L, m * d:(m + 1) * d] = _rope(kl_ref[0, :, m * d:(m + 1) * d], cos, sin).astype(jnp.bfloat16)
    k_sc[L:, :] = kc_ref[0].astype(jnp.bfloat16)
    v_sc[0:L, :] = vl_ref[0].astype(jnp.bfloat16)
    v_sc[L:, :] = vc_ref[0].astype(jnp.bfloat16)

    def scores(r0, m):
        q = _rope(q_ref[0, pl.ds(r0, tq), m * d:(m + 1) * d], cos_ref[pl.ds(r0, tq), :],
                  sin_ref[pl.ds(r0, tq), :]).astype(jnp.bfloat16)
        return lax.dot_general(q, k_sc[:, m * d:(m + 1) * d], (((1,), (1,)), ((), ())),
                               preferred_element_type=jnp.float32)

    def finish(r0, s_pair):
        probs = []
        for s in s_pair:
            e = jnp.exp2((s - jnp.max(s, axis=-1, keepdims=True)) * (scale * math.log2(math.e)))
            probs.append(e * (1.0 / jnp.sum(e, axis=-1, keepdims=True)))
        a = (probs[0] - lam * probs[1]).astype(jnp.bfloat16)
        o = jnp.dot(a, v_sc[...], preferred_element_type=jnp.float32)
        o = o * lax.rsqrt(jnp.mean(o * o, axis=-1, keepdims=True) + NORM_EPS) * g_ref[...]
        o_ref[0, pl.ds(r0, tq), :] = (o * (1 - lambda_init)).astype(o_ref.dtype)

    def q_tile_pair(t, carry):
        ra = pl.multiple_of(2 * t * tq, tq)
        rb = pl.multiple_of(ra + tq, tq)
        sa = [scores(ra, m) for m in range(2)]
        sb = [scores(rb, m) for m in range(2)]
        finish(ra, sa)
        finish(rb, sb)
        return carry

    lax.fori_loop(0, L // (2 * tq), q_tile_pair, 0)


def diff_attention_lat(qkv_lat, kv_ctx, cos, sin_signed, lam_vecs, subln, lambda_init, tq=256):
    B, L, _ = qkv_lat.shape
    Lc = kv_ctx.shape[1]
    H, hd = DIFF_HEADS, 2 * DIFF_HEAD_DIM
    tq = min(tq, L // 2)
    assert L % (2 * tq) == 0
    return pl.pallas_call(
        functools.partial(_diff_attn_body, lambda_init, tq),
        grid=(B, H),
        in_specs=[pl.BlockSpec((1, L, hd), lambda b, h: (b, 0, h)),
                  pl.BlockSpec((1, L, hd), lambda b, h: (b, 0, H + h)),
                  pl.BlockSpec((1, L, hd), lambda b, h: (b, 0, 2 * H + h)),
                  pl.BlockSpec((1, Lc, hd), lambda b, h: (b, 0, h)),
                  pl.BlockSpec((1, Lc, hd), lambda b, h: (b, 0, H + h)),
                  pl.BlockSpec((L, DIFF_HEAD_DIM), lambda b, h: (0, 0)),
                  pl.BlockSpec((L, DIFF_HEAD_DIM), lambda b, h: (0, 0)),
                  pl.BlockSpec((4, DIFF_HEAD_DIM), lambda b, h: (0, 0)),
                  pl.BlockSpec((1, hd), lambda b, h: (0, 0))],
        out_specs=pl.BlockSpec((1, L, hd), lambda b, h: (b, 0, h)),
        out_shape=jax.ShapeDtypeStruct((B, L, H * hd), jnp.bfloat16),
        scratch_shapes=[pltpu.VMEM((L + Lc, hd), jnp.bfloat16), pltpu.VMEM((L + Lc, hd), jnp.bfloat16)],
        compiler_params=_cparams("arbitrary", "arbitrary"), name="diff_attention",
    )(qkv_lat, qkv_lat, qkv_lat, kv_ctx, kv_ctx, cos, sin_signed, lam_vecs, subln.reshape(1, hd))


def diff_attn_mixer(h_ctx, h_lat, w_qkv, j, lam_q1, lam_k1, lam_q2, lam_k2, subln, lambda_init, cos, sin):
    B, L, D = h_lat.shape
    Lc = h_ctx.shape[1]
    qkv = proj(h_lat.reshape(B * L, D), w_qkv, j, tm=1024).reshape(B, L, 3 * D)
    kv_ctx = proj(h_ctx.reshape(B * Lc, D), w_qkv, j, col0=D, tm=1024).reshape(B, Lc, 2 * D)
    quarter = DIFF_HEAD_DIM // 4
    first = (jnp.arange(DIFF_HEAD_DIM) % (2 * quarter)) < quarter
    sin_signed = jnp.where(first[None, :], -sin, sin)
    lam_vecs = jnp.stack([lam_q1, lam_k1, lam_q2, lam_k2]).astype(jnp.float32)
    return diff_attention_lat(qkv, kv_ctx, cos, sin_signed, lam_vecs, subln, lambda_init)


def _count_before(mask):
    R, L = mask.shape
    W = PREFIX_BLOCK
    m = mask.astype(jnp.float32)
    before = (lax.broadcasted_iota(jnp.int32, (W, W), 0) < lax.broadcasted_iota(jnp.int32, (W, W), 1)
              ).astype(jnp.bfloat16)
    carry = jnp.zeros((R, 1), jnp.float32)
    out = []
    for c in range(L // W):
        blk = m[:, c * W:(c + 1) * W]
        out.append(jnp.dot(blk.astype(jnp.bfloat16), before, preferred_element_type=jnp.float32) + carry)
        carry = carry + jnp.sum(blk, axis=1, keepdims=True)
    return out[0] if len(out) == 1 else jnp.concatenate(out, axis=1)


def _select_body(cap, aff_ref, pos_e_ref, gate_e_ref, pos_t_ref):
    L = aff_ref.shape[1]
    a = aff_ref[0].T[:N_EXPERTS]
    bits = pltpu.bitcast(a, jnp.int32)
    thr = jnp.zeros((N_EXPERTS, 1), jnp.int32)
    for bit in range(30, -1, -1):
        cand = thr | (1 << bit)
        cnt = jnp.sum((bits >= cand).astype(jnp.float32), axis=1, keepdims=True)
        thr = jnp.where(cnt >= cap, cand, thr)
    gt = bits > thr
    eq = bits == thr
    need = cap - jnp.sum(gt.astype(jnp.float32), axis=1, keepdims=True)
    sel = gt | (eq & (_count_before(eq) < need))
    pos = jnp.where(sel, _count_before(sel).astype(jnp.int32), -1)
    pos_e_ref[0] = pos
    gate_e_ref[0] = jnp.where(sel, a, 0.0)
    pad = jnp.full((LANES - N_EXPERTS, L), -1, jnp.int32)
    pos_t_ref[0] = jnp.concatenate([pos, pad], axis=0).T


def ec_select(aff, cap):
    B, L, _ = aff.shape
    return pl.pallas_call(
        functools.partial(_select_body, cap),
        grid=(B,),
        in_specs=[pl.BlockSpec((1, L, LANES), lambda b: (b, 0, 0))],
        out_specs=[pl.BlockSpec((1, N_EXPERTS, L), lambda b: (b, 0, 0)),
                   pl.BlockSpec((1, N_EXPERTS, L), lambda b: (b, 0, 0)),
                   pl.BlockSpec((1, L, LANES), lambda b: (b, 0, 0))],
        out_shape=[jax.ShapeDtypeStruct((B, N_EXPERTS, L), jnp.int32),
                   jax.ShapeDtypeStruct((B, N_EXPERTS, L), jnp.float32),
                   jax.ShapeDtypeStruct((B, L, LANES), jnp.int32)],
        compiler_params=_cparams("arbitrary"), name="ec_select",
    )(aff)


def _gather_body(h_ref, pos_e_ref, gate_e_ref, xe_ref, gc_ref):
    e = pl.program_id(1)
    cap = xe_ref.shape[2]
    L = h_ref.shape[1]
    pos_row = pos_e_ref[0, pl.ds(e, 1), :]
    gate_row = gate_e_ref[0, pl.ds(e, 1), :]
    hit = lax.broadcasted_iota(jnp.int32, (cap, L), 0) == pos_row
    xe_ref[0, 0] = jnp.dot(hit.astype(jnp.bfloat16), h_ref[0],
                           preferred_element_type=jnp.float32).astype(jnp.bfloat16)
    gc_ref[0, 0] = jnp.sum(jnp.where(hit, gate_row, 0.0), axis=1, keepdims=True)


def ec_gather(h, pos_e, gate_e, cap):
    B, L, D = h.shape
    return pl.pallas_call(
        _gather_body,
        grid=(B, N_EXPERTS),
        in_specs=[pl.BlockSpec((1, L, D), lambda b, e: (b, 0, 0)),
                  pl.BlockSpec((1, N_EXPERTS, L), lambda b, e: (b, 0, 0)),
                  pl.BlockSpec((1, N_EXPERTS, L), lambda b, e: (b, 0, 0))],
        out_specs=[pl.BlockSpec((1, 1, cap, D), lambda b, e: (b, e, 0, 0)),
                   pl.BlockSpec((1, 1, cap, 1), lambda b, e: (b, e, 0, 0))],
        out_shape=[jax.ShapeDtypeStruct((B, N_EXPERTS, cap, D), jnp.bfloat16),
                   jax.ShapeDtypeStruct((B, N_EXPERTS, cap, 1), jnp.float32)],
        compiler_params=_cparams("arbitrary", "arbitrary"), name="ec_gather",
    )(h, pos_e, gate_e)


def _experts_body(nf, n_groups, *refs):
    xg = refs[:2 * n_groups]
    w1_ref, w3_ref, w2_ref = refs[2 * n_groups:2 * n_groups + 3]
    ye_refs = refs[2 * n_groups + 3:3 * n_groups + 3]
    hid_refs = refs[3 * n_groups + 3:]
    s = pl.program_id(1)

    @pl.when(s < nf)
    def _():
        w1b = w1_ref[0].astype(jnp.bfloat16)
        w3b = w3_ref[0].astype(jnp.bfloat16)
        for g in range(n_groups):
            xe_ref = xg[2 * g]
            B, _, cap, D = xe_ref.shape
            x = xe_ref[...].reshape(B * cap, D)
            h1 = jnp.dot(x, w1b, preferred_element_type=jnp.float32)
            h3 = jnp.dot(x, w3b, preferred_element_type=jnp.float32)
            hid_refs[g][s] = (h1 * jax.nn.sigmoid(h1) * h3).astype(jnp.bfloat16)

    @pl.when(s >= nf)
    def _():
        w2b = w2_ref[0].astype(jnp.bfloat16)
        for g in range(n_groups):
            hid_ref, gc_ref, ye_ref = hid_refs[g], xg[2 * g + 1], ye_refs[g]
            hid = jnp.concatenate([hid_ref[f] for f in range(nf)], axis=1) if nf > 1 else hid_ref[0]
            y = jnp.dot(hid, w2b, preferred_element_type=jnp.float32)
            y = y * gc_ref[...].reshape(y.shape[0], 1)
            ye_ref[...] = y.astype(jnp.bfloat16).reshape(ye_ref.shape)


def ec_experts(groups, w1, w3, w2, layer, tf=256, tn=1024):
    _, E, _, D = groups[0][0].shape
    F = w1.shape[3]
    tf = min(tf, F)
    tn = min(tn, D)
    nf, nn = F // tf, D // tn
    in_specs, args, out_specs, out_shape, scratch = [], [], [], [], []
    for xe, gc in groups:
        B, _, cap, _ = xe.shape
        in_specs += [pl.BlockSpec((B, 1, cap, D), lambda e, s: (0, e, 0, 0), pipeline_mode=pl.Buffered(1)),
                     pl.BlockSpec((B, 1, cap, 1), lambda e, s: (0, e, 0, 0))]
        args += [xe, gc]
        out_specs.append(pl.BlockSpec((B, 1, cap, tn), lambda e, s: (0, e, 0, jnp.maximum(s - nf, 0))))
        out_shape.append(jax.ShapeDtypeStruct((B, E, cap, D), jnp.bfloat16))
        scratch.append(pltpu.VMEM((nf, B * cap, tf), jnp.bfloat16))
    in_specs += [pl.BlockSpec((None, 1, D, tf), lambda e, s: (layer, e, 0, jnp.minimum(s, nf - 1))),
                 pl.BlockSpec((None, 1, D, tf), lambda e, s: (layer, e, 0, jnp.minimum(s, nf - 1))),
                 pl.BlockSpec((None, 1, F, tn), lambda e, s: (layer, e, 0, jnp.maximum(s - nf, 0)))]
    return pl.pallas_call(
        functools.partial(_experts_body, nf, len(groups)),
        grid=(E, nf + nn),
        in_specs=in_specs, out_specs=out_specs, out_shape=out_shape, scratch_shapes=scratch,
        compiler_params=_cparams("arbitrary", "arbitrary"), name="ec_experts",
    )(*args, w1, w3, w2)


def _combine_body(ctx_row, cap, pos_t_ref, ye_ref, x_ref, gate_ref, o_ref):
    row = pl.program_id(0) if ctx_row is None else ctx_row
    K = ye_ref.shape[1]
    assert cap & (cap - 1) == 0 and cap <= 256
    col_expert = lax.broadcasted_iota(jnp.int32, (LANES, K), 1) >> (cap.bit_length() - 1)
    spread = (col_expert == lax.broadcasted_iota(jnp.int32, (LANES, K), 0)).astype(jnp.bfloat16)
    pos = jnp.dot(pos_t_ref[0].astype(jnp.float32).astype(jnp.bfloat16), spread,
                  preferred_element_type=jnp.float32)
    col_slot = (lax.broadcasted_iota(jnp.int32, (1, K), 1) & (cap - 1)).astype(jnp.float32)
    hit = (pos == col_slot).astype(jnp.bfloat16)
    y = jnp.dot(hit, ye_ref[0], preferred_element_type=jnp.float32)
    o_ref[0] = x_ref[0] + _mod_row(gate_ref, row) * y


def ec_combine(x, pos_t, ye, mod, k_gate, cap, ctx_row=None, tt=512, tn=1024):
    B, L, D = x.shape
    tt = min(tt, L)
    tn = min(tn, D)
    K = N_EXPERTS * cap
    return pl.pallas_call(
        functools.partial(_combine_body, ctx_row, cap),
        grid=(B, D // tn, L // tt),
        in_specs=[pl.BlockSpec((1, tt, LANES), lambda b, n, t: (b, t, 0)),
                  pl.BlockSpec((1, K, tn), lambda b, n, t: (b, 0, n)),
                  pl.BlockSpec((1, tt, tn), lambda b, n, t: (b, t, n)),
                  pl.BlockSpec((MOD_ROWS, tn), lambda b, n, t: (0, k_gate * (D // tn) + n))],
        out_specs=pl.BlockSpec((1, tt, tn), lambda b, n, t: (b, t, n)),
        out_shape=jax.ShapeDtypeStruct((B, L, D), jnp.float32),
        compiler_params=_cparams("arbitrary", "arbitrary", "arbitrary"), name="ec_combine",
    )(pos_t, ye.reshape(B, K, D), x, mod)


def moe_blocks(streams, gain, mod, router, w1, w3, w2, layer):
    routed = []
    for x, ctx_row in streams:
        cap = EC_CAPACITY * x.shape[1] // N_EXPERTS
        h, aff = normmod(x, gain, mod, 3, 4, ctx_row=ctx_row, router=router)
        pos_e, gate_e, pos_t = ec_select(aff, cap)
        routed.append((ec_gather(h, pos_e, gate_e, cap), pos_t, cap))
    ye = ec_experts([r[0] for r in routed], w1, w3, w2, layer)
    return [ec_combine(x, pos_t, y, mod, 5, cap, ctx_row=ctx_row)
            for (x, ctx_row), (_, pos_t, cap), y in zip(streams, routed, ye)]


def kernel(x, c, ctx, c_ctx, ada_w, ada_b, norm_mix, norm_ffn, w_in, pool_w, pool_scale, conv_w, a_log_f, a_log_b, dt_bias_f, dt_bias_b, gdn_norm, w_out_ab, w_qkv, lam_q1, lam_k1, lam_q2, lam_k2, subln, w_out_c, router, w1, w3, w2, final_norm):
    depth = ada_w.shape[0]
    B, L, D = x.shape
    Lc = ctx.shape[1]
    assert B < MOD_ROWS
    ctx_row = B
    cos, sin = axial_rope_tables(L // GRID_W)
    cond = jnp.zeros((MOD_ROWS, D), jnp.float32).at[:B].set(jax.nn.silu(c)).at[ctx_row].set(jax.nn.silu(c_ctx))
    cond = cond.astype(jnp.bfloat16)
    x_lat, x_ctx = x, ctx
    for layer in range(depth):
        last = layer == depth - 1
        j = layer // 2
        mod = proj(cond, ada_w, layer) + ada_b[layer]
        h_lat = normmod(x_lat, norm_mix[layer], mod, 0, 1)
        h_ctx = normmod(x_ctx, norm_mix[layer], mod, 0, 1, ctx_row=ctx_row)
        if layer % 2 == 0:
            y_ctx, y_lat = pool_gdn_mixer(h_ctx, h_lat, w_in, j, pool_w[j], pool_scale[j], conv_w[j],
                                          a_log_f[j], a_log_b[j], dt_bias_f[j], dt_bias_b[j], gdn_norm[j])
            w_out = w_out_ab
        else:
            assert last, "context outputs of an attention layer are only skipped when no later layer reads them"
            lambda_init = 0.8 - 0.6 * math.exp(-0.3 * layer)
            y_ctx = None
            y_lat = [diff_attn_mixer(h_ctx, h_lat, w_qkv, j, lam_q1[j], lam_k1[j], lam_q2[j], lam_k2[j],
                                     subln[j], lambda_init, cos, sin)]
            w_out = w_out_c
        x_lat = proj([p.reshape(B * L, -1) for p in y_lat], w_out, j, tm=1024,
                     residual=(x_lat.reshape(B * L, D), mod, 2, L, None)).reshape(B, L, D)
        streams = [(x_lat, None)]
        if not last:
            x_ctx = proj([p.reshape(B * Lc, -1) for p in y_ctx], w_out, j, tm=1024,
                         residual=(x_ctx.reshape(B * Lc, D), mod, 2, Lc, ctx_row)).reshape(B, Lc, D)
            streams.append((x_ctx, ctx_row))
        outs = moe_blocks(streams, norm_ffn[layer], mod, router[layer], w1, w3, w2, layer)
        x_lat = outs[0]
        if not last:
            x_ctx = outs[1]
    return final_rmsnorm(x_lat, final_norm)
```

```python
import functools
import math

import jax
import jax.numpy as jnp
from jax import lax
from jax.experimental import pallas as pl
from jax.experimental.pallas import tpu as pltpu

D_MODEL = 4096
GRID_W = 64
NORM_EPS = 1e-6
POOL_WINDOWS = (2, 4, 8, 16)
POOL_GROUP = D_MODEL // 16
POOL_WIDTH = len(POOL_WINDOWS) * POOL_GROUP
GDN_HEAD_DIM = 128
GDN_HEADS = (D_MODEL - POOL_WIDTH) // GDN_HEAD_DIM
GDN_WIDTH = GDN_HEADS * GDN_HEAD_DIM
GDN_CONV = 5
GDN_CHUNK = 64
QKV_COLS = 3 * GDN_WIDTH
DIFF_HEAD_DIM = 128
DIFF_HEADS = D_MODEL // (2 * DIFF_HEAD_DIM)
ROPE_BASE = 10000.0
N_EXPERTS = 16
EC_CAPACITY = 2

LANES = 128
MOD_ROWS = 8
PREFIX_BLOCK = 256
VMEM_LIMIT_BYTES = 56 * 1024 * 1024

PROJ_TN = 512
GDN_PAIR = 2 * GDN_CHUNK
GDN_PAIRS_PER_STEP = 4
GDN_SOLVE_BASE = 4
CONV_PAD = 8
ROW_BLOCK = 128
STATE_W = 4 * GDN_HEADS
REST_TILE0 = QKV_COLS // LANES
IN_COLS_PADDED = -(-(QKV_COLS + STATE_W + POOL_WIDTH + GDN_WIDTH) // PROJ_TN) * PROJ_TN
Z_TILE0 = REST_TILE0 + (STATE_W + POOL_WIDTH) // LANES
Z_SHIFT = LANES - (STATE_W + POOL_WIDTH) % LANES
POOL_PAD = 16
POOL_IN_W = 1152


def _cparams(*sem):
    return pltpu.CompilerParams(dimension_semantics=sem, vmem_limit_bytes=VMEM_LIMIT_BYTES)


def _mod_row(mod_ref, row):
    return mod_ref[pl.ds(row, 1), :]


def _proj_body(n_parts, rows_per_sample, ctx_row, valid_cols, w_out_major, *refs):
    a_refs, w_ref = refs[:n_parts], refs[n_parts]
    if rows_per_sample is None:
        o_ref, wb_ref = refs[n_parts + 1:]
    else:
        x_ref, gate_ref, o_ref, wb_ref = refs[n_parts + 1:]
    i = pl.program_id(1)
    out_axis = 0 if w_out_major else 1

    @pl.when(i == 0)
    def _():
        w = w_ref[...]
        if valid_cols is not None:
            col = pl.program_id(0) * w.shape[out_axis] + lax.broadcasted_iota(jnp.int32, w.shape, out_axis)
            w = jnp.where(col < valid_cols, w, 0.0)
        wb_ref[...] = w.astype(jnp.bfloat16)

    y, k0 = None, 0
    for a_ref in a_refs:
        k1 = k0 + a_ref.shape[1]
        if w_out_major:
            part = _nt(a_ref[...], wb_ref[:, k0:k1])
        else:
            part = jnp.dot(a_ref[...], wb_ref[k0:k1, :], preferred_element_type=jnp.float32)
        y = part if y is None else y + part
        k0 = k1
    if rows_per_sample is None:
        o_ref[...] = y.astype(o_ref.dtype)
    else:
        row = (i * o_ref.shape[0]) // rows_per_sample if ctx_row is None else ctx_row
        o_ref[...] = x_ref[...] + _mod_row(gate_ref, row) * y


def proj(a, w, layer, *, col0=0, ncols=None, tm=512, tn=PROJ_TN, out_dtype=jnp.float32, residual=None,
         w_out_major=False):
    parts = list(a) if isinstance(a, (list, tuple)) else [a]
    M = parts[0].shape[0]
    K = sum(p.shape[1] for p in parts)
    n_out = w.shape[1] if w_out_major else w.shape[2]
    ncols = n_out - col0 if ncols is None else ncols
    tm = min(tm, M)
    assert M % tm == 0 and ncols % tn == 0 and col0 % tn == 0, (M, ncols, col0, tm, tn)
    j0 = col0 // tn
    valid_cols = n_out - col0 if col0 + ncols > n_out else None
    in_specs = [pl.BlockSpec((tm, p.shape[1]), lambda j, i: (i, 0)) for p in parts]
    if w_out_major:
        in_specs.append(pl.BlockSpec((None, tn, K), lambda j, i: (layer, j0 + j, 0)))
    else:
        in_specs.append(pl.BlockSpec((None, K, tn), lambda j, i: (layer, 0, j0 + j)))
    args = parts + [w]
    rows_per_sample = ctx_row = None
    if residual is not None:
        x, mod, k_gate, rows_per_sample, ctx_row = residual
        assert ctx_row is not None or rows_per_sample % tm == 0
        in_specs += [pl.BlockSpec((tm, tn), lambda j, i: (i, j)),
                     pl.BlockSpec((MOD_ROWS, tn), lambda j, i: (0, k_gate * (ncols // tn) + j))]
        args += [x, mod]
    return pl.pallas_call(
        functools.partial(_proj_body, len(parts), rows_per_sample, ctx_row, valid_cols, w_out_major),
        grid=(ncols // tn, M // tm),
        in_specs=in_specs,
        out_specs=pl.BlockSpec((tm, tn), lambda j, i: (i, j)),
        out_shape=jax.ShapeDtypeStruct((M, ncols), out_dtype),
        scratch_shapes=[pltpu.VMEM((tn, K) if w_out_major else (K, tn), jnp.bfloat16)],
        compiler_params=_cparams("arbitrary", "arbitrary"), name="proj",
    )(*args)


def _normmod_body(ctx_row, with_router, x_ref, g_ref, shift_ref, scale_ref, *rest):
    row = pl.program_id(0) if ctx_row is None else ctx_row
    x = x_ref[0]
    y = x * lax.rsqrt(jnp.mean(x * x, axis=-1, keepdims=True) + NORM_EPS) * g_ref[...]
    h = (y * (1.0 + _mod_row(scale_ref, row)) + _mod_row(shift_ref, row)).astype(jnp.bfloat16)
    if not with_router:
        (h_ref,) = rest
        h_ref[0] = h
        return
    r_ref, h_ref, aff_ref = rest
    h_ref[0] = h
    logits = jnp.dot(h, r_ref[...], preferred_element_type=jnp.float32)
    lane = lax.broadcasted_iota(jnp.int32, logits.shape, 1)
    logits = jnp.where(lane < N_EXPERTS, logits, -jnp.inf)
    e = jnp.exp(logits - jnp.max(logits, axis=-1, keepdims=True))
    aff_ref[0] = e / jnp.sum(e, axis=-1, keepdims=True)


def normmod(x, gain, mod, k_shift, k_scale, ctx_row=None, router=None, tr=256):
    B, L, D = x.shape
    tr = min(tr, L)
    in_specs = [pl.BlockSpec((1, tr, D), lambda b, t: (b, t, 0)),
                pl.BlockSpec((1, D), lambda b, t: (0, 0)),
                pl.BlockSpec((MOD_ROWS, D), lambda b, t: (0, k_shift)),
                pl.BlockSpec((MOD_ROWS, D), lambda b, t: (0, k_scale))]
    args = [x, gain.reshape(1, D), mod, mod]
    out_specs = [pl.BlockSpec((1, tr, D), lambda b, t: (b, t, 0))]
    out_shape = [jax.ShapeDtypeStruct((B, L, D), jnp.bfloat16)]
    if router is not None:
        rp = jnp.pad(router.astype(jnp.bfloat16), ((0, 0), (0, LANES - router.shape[1])))
        in_specs.append(pl.BlockSpec((D, LANES), lambda b, t: (0, 0)))
        args.append(rp)
        out_specs.append(pl.BlockSpec((1, tr, LANES), lambda b, t: (b, t, 0)))
        out_shape.append(jax.ShapeDtypeStruct((B, L, LANES), jnp.float32))
    out = pl.pallas_call(
        functools.partial(_normmod_body, ctx_row, router is not None),
        grid=(B, L // tr), in_specs=in_specs, out_specs=out_specs, out_shape=out_shape,
        compiler_params=_cparams("arbitrary", "arbitrary"), name="normmod",
    )(*args)
    return out if router is not None else out[0]


def _final_norm_body(x_ref, g_ref, o_ref):
    x = x_ref[...]
    o_ref[...] = x * lax.rsqrt(jnp.mean(x * x, axis=-1, keepdims=True) + NORM_EPS) * g_ref[...]


def final_rmsnorm(x, gain, tr=512):
    B, L, D = x.shape
    M = B * L
    tr = min(tr, M)
    return pl.pallas_call(
        _final_norm_body,
        grid=(M // tr,),
        in_specs=[pl.BlockSpec((tr, D), lambda i: (i, 0)), pl.BlockSpec((1, D), lambda i: (0, 0))],
        out_specs=pl.BlockSpec((tr, D), lambda i: (i, 0)),
        out_shape=jax.ShapeDtypeStruct((M, D), x.dtype),
        compiler_params=_cparams("arbitrary"), name="final_rmsnorm",
    )(x.reshape(M, D), gain.reshape(1, D)).reshape(B, L, D)


def _gates_body(st_ref, alog_ref, dtb_ref, rows_ref, cols_ref):
    T = st_ref.shape[1]
    H = GDN_HEADS
    x = st_ref[0].T
    t = x + dtb_ref[...]
    softplus = jnp.maximum(t, 0.0) + jnp.log1p(jnp.exp(-jnp.abs(t)))
    r_full = lax.broadcasted_iota(jnp.int32, x.shape, 0)
    val = jnp.where(r_full < 2 * H, -jnp.exp(alog_ref[...]) * softplus, jax.nn.sigmoid(x))
    pos = lax.broadcasted_iota(jnp.int32, (LANES, LANES), 1) & (GDN_CHUNK - 1)
    r = lax.broadcasted_iota(jnp.int32, (LANES, LANES), 0)
    tiles = []
    for c in range(T // LANES):
        v = val[:, c * LANES:(c + 1) * LANES]
        pre, suf, s = v, v, 1
        while s < GDN_CHUNK:
            pre = pre + jnp.where(pos >= s, pltpu.roll(pre, s, 1), 0.0)
            suf = suf + jnp.where(pos < GDN_CHUNK - s, pltpu.roll(suf, LANES - s, 1), 0.0)
            s *= 2
        tiles.append(jnp.where(r < H, pre, jnp.where(r < 2 * H, suf, v)))
    out = jnp.concatenate(tiles, axis=1) if len(tiles) > 1 else tiles[0]
    rows_ref[0] = out
    cols_ref[0] = out.T


def gdn_gates(p, a_log_f, a_log_b, dt_bias_f, dt_bias_b):
    B, T, _ = p.shape
    pad = jnp.zeros((LANES - 2 * GDN_HEADS,), jnp.float32)
    alog = jnp.concatenate([a_log_f, a_log_b, pad]).astype(jnp.float32).reshape(LANES, 1)
    dtb = jnp.concatenate([dt_bias_f, dt_bias_b, pad]).astype(jnp.float32).reshape(LANES, 1)
    return pl.pallas_call(
        _gates_body,
        grid=(B,),
        in_specs=[pl.BlockSpec((1, T, LANES), lambda b: (b, 0, REST_TILE0)),
                  pl.BlockSpec((LANES, 1), lambda b: (0, 0)),
                  pl.BlockSpec((LANES, 1), lambda b: (0, 0))],
        out_specs=[pl.BlockSpec((1, LANES, T), lambda b: (b, 0, 0)),
                   pl.BlockSpec((1, T, LANES), lambda b: (b, 0, 0))],
        out_shape=[jax.ShapeDtypeStruct((B, LANES, T), jnp.float32),
                   jax.ShapeDtypeStruct((B, T, LANES), jnp.float32)],
        compiler_params=_cparams("arbitrary"), name="gdn_gates",
    )(p, alog, dtb)


def _conv_silu(x_ref, w_ref, pad_sc, dst_sc, T, unit_norm):
    half = GDN_CONV // 2
    zeros = jnp.zeros((CONV_PAD, LANES), jnp.float32)
    pad_sc[0:CONV_PAD, :] = zeros
    pad_sc[CONV_PAD:CONV_PAD + T, :] = x_ref[0]
    pad_sc[CONV_PAD + T:2 * CONV_PAD + T, :] = zeros
    w = w_ref[...]
    for r0 in range(0, T, ROW_BLOCK):
        y = jnp.zeros((ROW_BLOCK, LANES), jnp.float32)
        for j in range(GDN_CONV):
            s = CONV_PAD + r0 + j - half
            y = y + w[j:j + 1, :] * pad_sc[s:s + ROW_BLOCK, :]
        y = y * jax.nn.sigmoid(y)
        if unit_norm:
            y = y * lax.rsqrt(jnp.sum(y * y, axis=-1, keepdims=True) + NORM_EPS)
        dst_sc[r0:r0 + ROW_BLOCK, :] = y


def _nt(a, b):
    return lax.dot_general(a, b, (((1,), (1,)), ((), ())), preferred_element_type=jnp.float32)


def _gdn_segment(T, h, refs, sc, s_init):
    pq_ref, pk_ref, pv_ref, cols_ref, rows_ref, za_ref, zb_ref, y_ref = refs
    (cwq_ref, cwk_ref, cwv_ref, gn_ref, pad_sc, q_sc, k_sc, v_sc, u_sc, w_sc, qd_sc, kdt_sc, attn_sc, gend_sc,
     o_sc, grow_sc) = sc
    C, P, H = GDN_CHUNK, GDN_PAIR, GDN_HEADS
    n_pairs = T // P
    bf16 = jnp.bfloat16
    scale = GDN_HEAD_DIM ** -0.5
    for d in range(2):
        grow_sc[d, 0:1, 0:T] = rows_ref[0, pl.ds(d * H + h, 1), :]
    _conv_silu(pq_ref, cwq_ref, pad_sc, q_sc, T, True)
    _conv_silu(pk_ref, cwk_ref, pad_sc, k_sc, T, True)
    _conv_silu(pv_ref, cwv_ref, pad_sc, v_sc, T, False)

    lane = lax.broadcasted_iota(jnp.int32, (P, LANES), 1)
    ii = lax.broadcasted_iota(jnp.int32, (C, C), 0)
    jj = lax.broadcasted_iota(jnp.int32, (C, C), 1)

    halves = [slice(c * C, (c + 1) * C) for c in range(2)]

    G = min(GDN_PAIRS_PER_STEP, n_pairs)
    assert n_pairs % G == 0
    n_steps = n_pairs // G

    def units(i):
        fwd = [(0, G * i + g) for g in range(G)]
        return fwd + [(1, n_pairs - 1 - m) for _, m in fwd]

    def mm(a, b):
        return jnp.dot(a, b, preferred_element_type=jnp.float32)


    def prep_stages(i):
        neg_a, rhs, dest, held = [], [], [], []
        for d, m in units(i):
            r0 = pl.multiple_of(m * P, P)
            q2 = q_sc[pl.ds(r0, P), :]
            k2 = k_sc[pl.ds(r0, P), :]
            v2 = v_sc[pl.ds(r0, P), :]
            cols = cols_ref[0, pl.ds(r0, P), :]
            qs = q2 * scale
            kh = [k2[sl].astype(bf16) for sl in halves]
            qk = [_nt(qs[sl].astype(bf16), kh[c]) for c, sl in enumerate(halves)]
            gcol = jnp.sum(jnp.where(lane == d * H + h, cols, 0.0), axis=1, keepdims=True)
            bcol = jnp.sum(jnp.where(lane == (2 + d) * H + h, cols, 0.0), axis=1, keepdims=True)
            eg = jnp.exp(gcol)
            kb = k2 * bcol
            kk = [_nt(kb[sl].astype(bf16), kh[c]) for c, sl in enumerate(halves)]
            held.append((d, m, r0, k2, v2, qs, qk, kk, gcol, bcol, eg, kb))
        yield
        for d, m, r0, k2, v2, qs, qk, kk, gcol, bcol, eg, kb in held:
            grow = grow_sc[d, 0:1, pl.ds(r0, P)]
            keep = (ii >= jj) if d == 0 else (ii <= jj)
            strict = (ii > jj) if d == 0 else (ii < jj)
            rhs_uw = jnp.concatenate([v2 * bcol, kb * eg], axis=1)
            qd_sc[d, pl.ds(r0, P), :] = (qs * eg).astype(bf16)
            kd = []
            for c, sl in enumerate(halves):
                rc = pl.multiple_of(r0 + c * C, C)
                gc = gcol[sl]
                decay = jnp.exp(jnp.where(keep, gc - grow[:, sl], -jnp.inf))
                neg_a.append(jnp.where(strict, -(kk[c] * decay), 0.0))
                rhs.append(rhs_uw[sl])
                dest.append((d, rc))
                attn_sc[d, pl.ds(rc, C), :] = (qk[c] * decay).astype(bf16)
                g_last = gc[C - 1:C] if d == 0 else gc[0:1]
                kd.append(k2[sl] * jnp.exp(g_last - gc))
                gend_sc[d, pl.ds(pl.multiple_of((2 * m + c) * 8, 8), 8), :] = jnp.broadcast_to(
                    jnp.exp(g_last), (8, LANES))
            kdt_sc[d, :, pl.ds(r0, P)] = jnp.concatenate(kd, axis=0).T.astype(bf16)
        s0 = GDN_SOLVE_BASE
        same = lambda s: (ii >> (s.bit_length() - 1)) == (jj >> (s.bit_length() - 1))
        d_ = [jnp.where(same(s0), x, 0.0) for x in neg_a]
        db = [x.astype(bf16) for x in d_]
        p = [mm(x, x) for x in db]
        yield
        dp = [mm(a, b.astype(bf16)) for a, b in zip(db, p)]
        yield
        e = [a + b + ab for a, b, ab in zip(d_, p, dp)]
        s = s0
        while s < C:
            off = same(2 * s) & jnp.logical_not(same(s))
            o = [jnp.where(off, x, 0.0) for x in neg_a]
            oe = [mm(a.astype(bf16), b.astype(bf16)) for a, b in zip(o, e)]
            yield
            m_ = [a + b for a, b in zip(o, oe)]
            em = [mm(a.astype(bf16), b.astype(bf16)) for a, b in zip(e, m_)]
            yield
            e = [a + b + ab for a, b, ab in zip(e, m_, em)]
            s *= 2
        corr = [mm(a.astype(bf16), r.astype(bf16)) for a, r in zip(e, rhs)]
        yield
        for (d, rc), r, cr in zip(dest, rhs, corr):
            s = r + cr
            u_sc[d, pl.ds(rc, C), :] = s[:, :GDN_HEAD_DIM]
            w_sc[d, pl.ds(rc, C), :] = s[:, GDN_HEAD_DIM:].astype(bf16)

    def scan_operands(i):
        ops = ([], [])
        for d, m in units(i):
            r0 = pl.multiple_of(m * P, P)
            kdt = kdt_sc[d, :, pl.ds(r0, P)]
            for c in ((0, 1) if d == 0 else (1, 0)):
                rc = pl.multiple_of(r0 + c * C, C)
                g_end = gend_sc[d, pl.ds(pl.multiple_of((2 * m + c) * 8, 8), 8), :][0:1]
                ops[d].append((d, rc, u_sc[d, pl.ds(rc, C), :], w_sc[d, pl.ds(rc, C), :], qd_sc[d, pl.ds(rc, C), :],
                               attn_sc[d, pl.ds(rc, C), :], kdt[:, c * C:(c + 1) * C], g_end))
        return ops

    def scan_stages(ops, carry, result):
        s = list(carry)
        for k in range(2 * G):
            cur = [ops[d][k] for d in range(2)]
            sb = [s[d].astype(bf16) for d in range(2)]
            ws = [mm(cur[d][3], sb[d]) for d in range(2)]
            qs_ = [mm(cur[d][4], sb[d]) for d in range(2)]
            yield
            vb = [(cur[d][2] - ws[d]).astype(bf16) for d in range(2)]
            av = [mm(cur[d][5], vb[d]) for d in range(2)]
            kv = [mm(cur[d][6], vb[d]) for d in range(2)]
            yield
            for d in range(2):
                o_sc[d, pl.ds(cur[d][1], C), :] = qs_[d] + av[d]
                s[d] = s[d] * cur[d][7] + kv[d]
        result.append(tuple(s))

    def run_interleaved(*gens):
        gens = list(gens)
        while gens:
            for g in list(gens):
                try:
                    next(g)
                except StopIteration:
                    gens.remove(g)

    def step(i, carry):
        result = []
        run_interleaved(prep_stages(i + 1), scan_stages(scan_operands(i), carry, result))
        return result[0]

    run_interleaved(prep_stages(0))
    carry = lax.fori_loop(0, n_steps - 1, step, s_init)
    result = []
    run_interleaved(scan_stages(scan_operands(n_steps - 1), carry, result))
    s_f, s_b = result[0]

    lane_b = lax.broadcasted_iota(jnp.int32, (ROW_BLOCK, LANES), 1)
    for r0 in range(0, T, ROW_BLOCK):
        o = o_sc[0, r0:r0 + ROW_BLOCK, :] + o_sc[1, r0:r0 + ROW_BLOCK, :]
        z = pltpu.roll(jnp.where(lane_b >= LANES - Z_SHIFT, za_ref[0, r0:r0 + ROW_BLOCK, :],
                                 zb_ref[0, r0:r0 + ROW_BLOCK, :]), Z_SHIFT, 1)
        y = o * lax.rsqrt(jnp.mean(o * o, axis=-1, keepdims=True) + NORM_EPS) * gn_ref[...]
        y_ref[0, r0:r0 + ROW_BLOCK, :] = (y * (z * jax.nn.sigmoid(z))).astype(y_ref.dtype)
    return s_f, s_b


def _gdn_body(Tc, Tl, *refs):
    h = pl.program_id(1)
    ctx_refs = refs[0:7]
    lat_refs = refs[7:14]
    shared = refs[14:18]
    y_ctx_ref, y_lat_ref = refs[18:20]
    sc = shared + refs[20:]
    zero = jnp.zeros((GDN_HEAD_DIM, GDN_HEAD_DIM), jnp.float32)
    states = _gdn_segment(Tc, h, ctx_refs + (y_ctx_ref,), sc, (zero, zero))
    _gdn_segment(Tl, h, lat_refs + (y_lat_ref,), sc, states)


def gdn_mix(p_ctx, gates_ctx, p_lat, gates_lat, conv_w, gdn_norm):
    B, Tc, _ = p_ctx.shape
    Tl = p_lat.shape[1]
    H, dh = GDN_HEADS, GDN_HEAD_DIM

    def seg_specs(T):
        return [pl.BlockSpec((1, T, dh), lambda b, h: (b, 0, h)),
                pl.BlockSpec((1, T, dh), lambda b, h: (b, 0, H + h)),
                pl.BlockSpec((1, T, dh), lambda b, h: (b, 0, 2 * H + h)),
                pl.BlockSpec((1, T, LANES), lambda b, h: (b, 0, 0)),
                pl.BlockSpec((1, LANES, T), lambda b, h: (b, 0, 0)),
                pl.BlockSpec((1, T, LANES), lambda b, h: (b, 0, Z_TILE0 + h)),
                pl.BlockSpec((1, T, LANES), lambda b, h: (b, 0, Z_TILE0 + 1 + h))]

    shared_specs = [pl.BlockSpec((GDN_CONV, dh), lambda b, h: (0, h)),
                    pl.BlockSpec((GDN_CONV, dh), lambda b, h: (0, H + h)),
                    pl.BlockSpec((GDN_CONV, dh), lambda b, h: (0, 2 * H + h)),
                    pl.BlockSpec((1, dh), lambda b, h: (0, 0))]
    rows_c, cols_c = gates_ctx
    rows_l, cols_l = gates_lat
    f32, bf16 = jnp.float32, jnp.bfloat16
    scratch = [pltpu.VMEM((Tl + 2 * CONV_PAD, LANES), f32),
               pltpu.VMEM((Tl, dh), f32), pltpu.VMEM((Tl, dh), f32), pltpu.VMEM((Tl, dh), f32),
               pltpu.VMEM((2, Tl, dh), f32), pltpu.VMEM((2, Tl, dh), bf16), pltpu.VMEM((2, Tl, dh), bf16),
               pltpu.VMEM((2, dh, Tl), bf16), pltpu.VMEM((2, Tl, GDN_CHUNK), bf16),
               pltpu.VMEM((2, Tl // GDN_CHUNK * 8, LANES), f32), pltpu.VMEM((2, Tl, dh), f32),
               pltpu.VMEM((2, 8, Tl), f32)]
    return pl.pallas_call(
        functools.partial(_gdn_body, Tc, Tl),
        grid=(B, H),
        in_specs=seg_specs(Tc) + seg_specs(Tl) + shared_specs,
        out_specs=[pl.BlockSpec((1, Tc, dh), lambda b, h: (b, 0, h)),
                   pl.BlockSpec((1, Tl, dh), lambda b, h: (b, 0, h))],
        out_shape=[jax.ShapeDtypeStruct((B, Tc, GDN_WIDTH), bf16), jax.ShapeDtypeStruct((B, Tl, GDN_WIDTH), bf16)],
        scratch_shapes=scratch,
        compiler_params=_cparams("arbitrary", "arbitrary"), name="gdn_mix",
    )(p_ctx, p_ctx, p_ctx, cols_c, rows_c, p_ctx, p_ctx,
      p_lat, p_lat, p_lat, cols_l, rows_l, p_lat, p_lat,
      conv_w, conv_w, conv_w, gdn_norm.reshape(1, dh))


def _pool_body(x_ref, w_ref, sc_ref, y_ref, pad_sc):
    T = x_ref.shape[1]
    G = POOL_GROUP
    zeros = jnp.zeros((POOL_PAD, G), jnp.float32)
    t = lax.broadcasted_iota(jnp.int32, (ROW_BLOCK, 1), 0)
    for i, win in enumerate(POOL_WINDOWS):
        half = win // 2
        lo = (STATE_W + i * G) // LANES * LANES
        off = STATE_W + i * G - lo
        pad_sc[0:POOL_PAD, :] = zeros
        pad_sc[POOL_PAD:POOL_PAD + T, :] = x_ref[0, :, lo:lo + G + LANES][:, off:off + G]
        pad_sc[POOL_PAD + T:2 * POOL_PAD + T, :] = zeros
        wb = w_ref[i].astype(jnp.bfloat16)
        for r0 in range(0, T, ROW_BLOCK):
            tot = jnp.zeros((ROW_BLOCK, G), jnp.float32)
            for s in range(-half, half):
                tot = tot + pad_sc[POOL_PAD + r0 + s:POOL_PAD + r0 + s + ROW_BLOCK, :]
            cnt = (jnp.minimum(t + r0 + half, T) - jnp.maximum(t + r0 - half, 0)).astype(jnp.float32)
            centred = tot / cnt - pad_sc[POOL_PAD + r0:POOL_PAD + r0 + ROW_BLOCK, :]
            y = jnp.dot(centred.astype(jnp.bfloat16), wb, preferred_element_type=jnp.float32)
            y_ref[0, r0:r0 + ROW_BLOCK, i * G:(i + 1) * G] = (y * sc_ref[:, i * G:(i + 1) * G]).astype(y_ref.dtype)


def pool_mix(p, pool_w, pool_scale):
    B, T, _ = p.shape
    n = len(POOL_WINDOWS)
    assert QKV_COLS % POOL_IN_W == 0 and STATE_W + POOL_WIDTH <= POOL_IN_W
    return pl.pallas_call(
        _pool_body,
        grid=(B,),
        in_specs=[pl.BlockSpec((1, T, POOL_IN_W), lambda b: (b, 0, QKV_COLS // POOL_IN_W)),
                  pl.BlockSpec((n, POOL_GROUP, POOL_GROUP), lambda b: (0, 0, 0)),
                  pl.BlockSpec((1, POOL_WIDTH), lambda b: (0, 0))],
        out_specs=pl.BlockSpec((1, T, POOL_WIDTH), lambda b: (b, 0, 0)),
        out_shape=jax.ShapeDtypeStruct((B, T, POOL_WIDTH), jnp.bfloat16),
        scratch_shapes=[pltpu.VMEM((T + 2 * POOL_PAD, POOL_GROUP), jnp.float32)],
        compiler_params=_cparams("arbitrary"), name="pool_mix",
    )(p, pool_w, pool_scale.reshape(1, POOL_WIDTH))


def pool_gdn_mixer(h_ctx, h_lat, w_in, j, pool_w, pool_scale, conv_w, a_log_f, a_log_b, dt_bias_f, dt_bias_b,
                   gdn_norm):
    B, Tl, D = h_lat.shape
    Tc = h_ctx.shape[1]
    w_in_t = jnp.swapaxes(w_in, 1, 2)
    p_lat = proj(h_lat.reshape(B * Tl, D), w_in_t, j, ncols=IN_COLS_PADDED, tm=1024,
                 w_out_major=True).reshape(B, Tl, IN_COLS_PADDED)
    p_ctx = proj(h_ctx.reshape(B * Tc, D), w_in_t, j, ncols=IN_COLS_PADDED, tm=1024,
                 w_out_major=True).reshape(B, Tc, IN_COLS_PADDED)
    gates_ctx = gdn_gates(p_ctx, a_log_f, a_log_b, dt_bias_f, dt_bias_b)
    gates_lat = gdn_gates(p_lat, a_log_f, a_log_b, dt_bias_f, dt_bias_b)
    g_ctx, g_lat = gdn_mix(p_ctx, gates_ctx, p_lat, gates_lat, conv_w, gdn_norm)
    return [pool_mix(p_ctx, pool_w, pool_scale), g_ctx], [pool_mix(p_lat, pool_w, pool_scale), g_lat]


def axial_rope_tables(rows):
    pos_row = jnp.repeat(jnp.arange(rows), GRID_W).astype(jnp.float32)
    pos_col = jnp.tile(jnp.arange(GRID_W), rows).astype(jnp.float32)
    quarter = DIFF_HEAD_DIM // 4
    inv_freq = ROPE_BASE ** (-jnp.arange(quarter, dtype=jnp.float32) / quarter)
    ang_r = pos_row[:, None] * inv_freq[None, :]
    ang_c = pos_col[:, None] * inv_freq[None, :]
    ang = jnp.concatenate([ang_r, ang_r, ang_c, ang_c], axis=-1)
    return jnp.cos(ang), jnp.sin(ang)


def _rope(x, cos, sin_signed):
    d = x.shape[-1]
    first = (lax.broadcasted_iota(jnp.int32, x.shape, 1) & (d // 2 - 1)) < d // 4
    partner = jnp.where(first, pltpu.roll(x, d - d // 4, 1), pltpu.roll(x, d // 4, 1))
    return x * cos + partner * sin_signed


def _diff_attn_body(lambda_init, tq, q_ref, kl_ref, vl_ref, kc_ref, vc_ref, cos_ref, sin_ref, lam_ref, g_ref,
                    o_ref, k_sc, v_sc):
    d = DIFF_HEAD_DIM
    L = kl_ref.shape[1]
    scale = d ** -0.5
    lv = lam_ref[...]
    lam = (jnp.exp(jnp.sum(lv[0:1] * lv[1:2], axis=1, keepdims=True))
           - jnp.exp(jnp.sum(lv[2:3] * lv[3:4], axis=1, keepdims=True)) + lambda_init)
    cos = cos_ref[...]
    sin = sin_ref[...]
    for m in range(2):
        k_sc[0:L, m * d:(m + 1) * d] = _rope(kl_ref[0, :, m * d:(m + 1) * d], cos, sin).astype(jnp.bfloat16)
    k_sc[L:, :] = kc_ref[0].astype(jnp.bfloat16)
    v_sc[0:L, :] = vl_ref[0].astype(jnp.bfloat16)
    v_sc[L:, :] = vc_ref[0].astype(jnp.bfloat16)

    def scores(r0, m):
        q = _rope(q_ref[0, pl.ds(r0, tq), m * d:(m + 1) * d], cos_ref[pl.ds(r0, tq), :],
                  sin_ref[pl.ds(r0, tq), :]).astype(jnp.bfloat16)
        return lax.dot_general(q, k_sc[:, m * d:(m + 1) * d], (((1,), (1,)), ((), ())),
                               preferred_element_type=jnp.float32)

    def finish(r0, s_pair):
        probs = []
        for s in s_pair:
            e = jnp.exp2((s - jnp.max(s, axis=-1, keepdims=True)) * (scale * math.log2(math.e)))
            probs.append(e * (1.0 / jnp.sum(e, axis=-1, keepdims=True)))
        a = (probs[0] - lam * probs[1]).astype(jnp.bfloat16)
        o = jnp.dot(a, v_sc[...], preferred_element_type=jnp.float32)
        o = o * lax.rsqrt(jnp.mean(o * o, axis=-1, keepdims=True) + NORM_EPS) * g_ref[...]
        o_ref[0, pl.ds(r0, tq), :] = (o * (1 - lambda_init)).astype(o_ref.dtype)

    def q_tile_pair(t, carry):
        ra = pl.multiple_of(2 * t * tq, tq)
        rb = pl.multiple_of(ra + tq, tq)
        sa = [scores(ra, m) for m in range(2)]
        sb = [scores(rb, m) for m in range(2)]
        finish(ra, sa)
        finish(rb, sb)
        return carry

    lax.fori_loop(0, L // (2 * tq), q_tile_pair, 0)


def diff_attention_lat(qkv_lat, kv_ctx, cos, sin_signed, lam_vecs, subln, lambda_init, tq=512):
    B, L, _ = qkv_lat.shape
    Lc = kv_ctx.shape[1]
    H, hd = DIFF_HEADS, 2 * DIFF_HEAD_DIM
    tq = min(tq, L // 2)
    assert L % (2 * tq) == 0
    return pl.pallas_call(
        functools.partial(_diff_attn_body, lambda_init, tq),
        grid=(B, H),
        in_specs=[pl.BlockSpec((1, L, hd), lambda b, h: (b, 0, h)),
                  pl.BlockSpec((1, L, hd), lambda b, h: (b, 0, H + h)),
                  pl.BlockSpec((1, L, hd), lambda b, h: (b, 0, 2 * H + h)),
                  pl.BlockSpec((1, Lc, hd), lambda b, h: (b, 0, h)),
                  pl.BlockSpec((1, Lc, hd), lambda b, h: (b, 0, H + h)),
                  pl.BlockSpec((L, DIFF_HEAD_DIM), lambda b, h: (0, 0)),
                  pl.BlockSpec((L, DIFF_HEAD_DIM), lambda b, h: (0, 0)),
                  pl.BlockSpec((4, DIFF_HEAD_DIM), lambda b, h: (0, 0)),
                  pl.BlockSpec((1, hd), lambda b, h: (0, 0))],
        out_specs=pl.BlockSpec((1, L, hd), lambda b, h: (b, 0, h)),
        out_shape=jax.ShapeDtypeStruct((B, L, H * hd), jnp.bfloat16),
        scratch_shapes=[pltpu.VMEM((L + Lc, hd), jnp.bfloat16), pltpu.VMEM((L + Lc, hd), jnp.bfloat16)],
        compiler_params=_cparams("arbitrary", "arbitrary"), name="diff_attention",
    )(qkv_lat, qkv_lat, qkv_lat, kv_ctx, kv_ctx, cos, sin_signed, lam_vecs, subln.reshape(1, hd))


def diff_attn_mixer(h_ctx, h_lat, w_qkv, j, lam_q1, lam_k1, lam_q2, lam_k2, subln, lambda_init, cos, sin):
    B, L, D = h_lat.shape
    Lc = h_ctx.shape[1]
    qkv = proj(h_lat.reshape(B * L, D), w_qkv, j, tm=1024).reshape(B, L, 3 * D)
    kv_ctx = proj(h_ctx.reshape(B * Lc, D), w_qkv, j, col0=D, tm=1024).reshape(B, Lc, 2 * D)
    quarter = DIFF_HEAD_DIM // 4
    first = (jnp.arange(DIFF_HEAD_DIM) % (2 * quarter)) < quarter
    sin_signed = jnp.where(first[None, :], -sin, sin)
    lam_vecs = jnp.stack([lam_q1, lam_k1, lam_q2, lam_k2]).astype(jnp.float32)
    return diff_attention_lat(qkv, kv_ctx, cos, sin_signed, lam_vecs, subln, lambda_init)


def _count_before(mask):
    R, L = mask.shape
    W = PREFIX_BLOCK
    m = mask.astype(jnp.float32)
    before = (lax.broadcasted_iota(jnp.int32, (W, W), 0) < lax.broadcasted_iota(jnp.int32, (W, W), 1)
              ).astype(jnp.bfloat16)
    carry = jnp.zeros((R, 1), jnp.float32)
    out = []
    for c in range(L // W):
        blk = m[:, c * W:(c + 1) * W]
        out.append(jnp.dot(blk.astype(jnp.bfloat16), before, preferred_element_type=jnp.float32) + carry)
        carry = carry + jnp.sum(blk, axis=1, keepdims=True)
    return out[0] if len(out) == 1 else jnp.concatenate(out, axis=1)


def _select_body(cap, aff_ref, pos_e_ref, gate_e_ref, pos_t_ref):
    L = aff_ref.shape[1]
    a = aff_ref[0].T[:N_EXPERTS]
    bits = pltpu.bitcast(a, jnp.int32)
    thr = jnp.zeros((N_EXPERTS, 1), jnp.int32)
    for bit in range(30, -1, -1):
        cand = thr | (1 << bit)
        cnt = jnp.sum((bits >= cand).astype(jnp.float32), axis=1, keepdims=True)
        thr = jnp.where(cnt >= cap, cand, thr)
    gt = bits > thr
    eq = bits == thr
    need = cap - jnp.sum(gt.astype(jnp.float32), axis=1, keepdims=True)
    sel = gt | (eq & (_count_before(eq) < need))
    pos = jnp.where(sel, _count_before(sel).astype(jnp.int32), -1)
    pos_e_ref[0] = pos
    gate_e_ref[0] = jnp.where(sel, a, 0.0)
    pad = jnp.full((LANES - N_EXPERTS, L), -1, jnp.int32)
    pos_t_ref[0] = jnp.concatenate([pos, pad], axis=0).T


def ec_select(aff, cap):
    B, L, _ = aff.shape
    return pl.pallas_call(
        functools.partial(_select_body, cap),
        grid=(B,),
        in_specs=[pl.BlockSpec((1, L, LANES), lambda b: (b, 0, 0))],
        out_specs=[pl.BlockSpec((1, N_EXPERTS, L), lambda b: (b, 0, 0)),
                   pl.BlockSpec((1, N_EXPERTS, L), lambda b: (b, 0, 0)),
                   pl.BlockSpec((1, L, LANES), lambda b: (b, 0, 0))],
        out_shape=[jax.ShapeDtypeStruct((B, N_EXPERTS, L), jnp.int32),
                   jax.ShapeDtypeStruct((B, N_EXPERTS, L), jnp.float32),
                   jax.ShapeDtypeStruct((B, L, LANES), jnp.int32)],
        compiler_params=_cparams("arbitrary"), name="ec_select",
    )(aff)


def _gather_body(h_ref, pos_e_ref, gate_e_ref, xe_ref, gc_ref):
    e = pl.program_id(1)
    cap = xe_ref.shape[2]
    L = h_ref.shape[1]
    pos_row = pos_e_ref[0, pl.ds(e, 1), :]
    gate_row = gate_e_ref[0, pl.ds(e, 1), :]
    hit = lax.broadcasted_iota(jnp.int32, (cap, L), 0) == pos_row
    xe_ref[0, 0] = jnp.dot(hit.astype(jnp.bfloat16), h_ref[0],
                           preferred_element_type=jnp.float32).astype(jnp.bfloat16)
    gc_ref[0, 0] = jnp.sum(jnp.where(hit, gate_row, 0.0), axis=1, keepdims=True)


def ec_gather(h, pos_e, gate_e, cap):
    B, L, D = h.shape
    return pl.pallas_call(
        _gather_body,
        grid=(B, N_EXPERTS),
        in_specs=[pl.BlockSpec((1, L, D), lambda b, e: (b, 0, 0)),
                  pl.BlockSpec((1, N_EXPERTS, L), lambda b, e: (b, 0, 0)),
                  pl.BlockSpec((1, N_EXPERTS, L), lambda b, e: (b, 0, 0))],
        out_specs=[pl.BlockSpec((1, 1, cap, D), lambda b, e: (b, e, 0, 0)),
                   pl.BlockSpec((1, 1, cap, 1), lambda b, e: (b, e, 0, 0))],
        out_shape=[jax.ShapeDtypeStruct((B, N_EXPERTS, cap, D), jnp.bfloat16),
                   jax.ShapeDtypeStruct((B, N_EXPERTS, cap, 1), jnp.float32)],
        compiler_params=_cparams("arbitrary", "arbitrary"), name="ec_gather",
    )(h, pos_e, gate_e)


def _experts_body(nf, n_groups, *refs):
    xg = refs[:2 * n_groups]
    w1_ref, w3_ref, w2_ref = refs[2 * n_groups:2 * n_groups + 3]
    ye_refs = refs[2 * n_groups + 3:3 * n_groups + 3]
    hid_refs = refs[3 * n_groups + 3:]
    s = pl.program_id(1)

    @pl.when(s < nf)
    def _():
        w1b = w1_ref[0].astype(jnp.bfloat16)
        w3b = w3_ref[0].astype(jnp.bfloat16)
        for g in range(n_groups):
            xe_ref = xg[2 * g]
            B, _, cap, D = xe_ref.shape
            x = xe_ref[...].reshape(B * cap, D)
            h1 = jnp.dot(x, w1b, preferred_element_type=jnp.float32)
            h3 = jnp.dot(x, w3b, preferred_element_type=jnp.float32)
            hid_refs[g][s] = (h1 * jax.nn.sigmoid(h1) * h3).astype(jnp.bfloat16)

    @pl.when(s >= nf)
    def _():
        w2b = w2_ref[0].astype(jnp.bfloat16)
        for g in range(n_groups):
            hid_ref, gc_ref, ye_ref = hid_refs[g], xg[2 * g + 1], ye_refs[g]
            hid = jnp.concatenate([hid_ref[f] for f in range(nf)], axis=1) if nf > 1 else hid_ref[0]
            y = jnp.dot(hid, w2b, preferred_element_type=jnp.float32)
            y = y * gc_ref[...].reshape(y.shape[0], 1)
            ye_ref[...] = y.astype(jnp.bfloat16).reshape(ye_ref.shape)


def ec_experts(groups, w1, w3, w2, layer, tf=256, tn=1024):
    _, E, _, D = groups[0][0].shape
    F = w1.shape[3]
    tf = min(tf, F)
    tn = min(tn, D)
    nf, nn = F // tf, D // tn
    in_specs, args, out_specs, out_shape, scratch = [], [], [], [], []
    for xe, gc in groups:
        B, _, cap, _ = xe.shape
        in_specs += [pl.BlockSpec((B, 1, cap, D), lambda e, s: (0, e, 0, 0), pipeline_mode=pl.Buffered(1)),
                     pl.BlockSpec((B, 1, cap, 1), lambda e, s: (0, e, 0, 0))]
        args += [xe, gc]
        out_specs.append(pl.BlockSpec((B, 1, cap, tn), lambda e, s: (0, e, 0, jnp.maximum(s - nf, 0))))
        out_shape.append(jax.ShapeDtypeStruct((B, E, cap, D), jnp.bfloat16))
        scratch.append(pltpu.VMEM((nf, B * cap, tf), jnp.bfloat16))
    in_specs += [pl.BlockSpec((None, 1, D, tf), lambda e, s: (layer, e, 0, jnp.minimum(s, nf - 1))),
                 pl.BlockSpec((None, 1, D, tf), lambda e, s: (layer, e, 0, jnp.minimum(s, nf - 1))),
                 pl.BlockSpec((None, 1, F, tn), lambda e, s: (layer, e, 0, jnp.maximum(s - nf, 0)))]
    return pl.pallas_call(
        functools.partial(_experts_body, nf, len(groups)),
        grid=(E, nf + nn),
        in_specs=in_specs, out_specs=out_specs, out_shape=out_shape, scratch_shapes=scratch,
        compiler_params=_cparams("arbitrary", "arbitrary"), name="ec_experts",
    )(*args, w1, w3, w2)


def _combine_body(ctx_row, cap, pos_t_ref, ye_ref, x_ref, gate_ref, o_ref):
    row = pl.program_id(0) if ctx_row is None else ctx_row
    K = ye_ref.shape[1]
    assert cap & (cap - 1) == 0 and cap <= 256
    col_expert = lax.broadcasted_iota(jnp.int32, (LANES, K), 1) >> (cap.bit_length() - 1)
    spread = (col_expert == lax.broadcasted_iota(jnp.int32, (LANES, K), 0)).astype(jnp.bfloat16)
    pos = jnp.dot(pos_t_ref[0].astype(jnp.float32).astype(jnp.bfloat16), spread,
                  preferred_element_type=jnp.float32)
    col_slot = (lax.broadcasted_iota(jnp.int32, (1, K), 1) & (cap - 1)).astype(jnp.float32)
    hit = (pos == col_slot).astype(jnp.bfloat16)
    y = jnp.dot(hit, ye_ref[0], preferred_element_type=jnp.float32)
    o_ref[0] = x_ref[0] + _mod_row(gate_ref, row) * y


def ec_combine(x, pos_t, ye, mod, k_gate, cap, ctx_row=None, tt=512, tn=1024):
    B, L, D = x.shape
    tt = min(tt, L)
    tn = min(tn, D)
    K = N_EXPERTS * cap
    return pl.pallas_call(
        functools.partial(_combine_body, ctx_row, cap),
        grid=(B, D // tn, L // tt),
        in_specs=[pl.BlockSpec((1, tt, LANES), lambda b, n, t: (b, t, 0)),
                  pl.BlockSpec((1, K, tn), lambda b, n, t: (b, 0, n)),
                  pl.BlockSpec((1, tt, tn), lambda b, n, t: (b, t, n)),
                  pl.BlockSpec((MOD_ROWS, tn), lambda b, n, t: (0, k_gate * (D // tn) + n))],
        out_specs=pl.BlockSpec((1, tt, tn), lambda b, n, t: (b, t, n)),
        out_shape=jax.ShapeDtypeStruct((B, L, D), jnp.float32),
        compiler_params=_cparams("arbitrary", "arbitrary", "arbitrary"), name="ec_combine",
    )(pos_t, ye.reshape(B, K, D), x, mod)


def moe_blocks(streams, gain, mod, router, w1, w3, w2, layer):
    routed = []
    for x, ctx_row in streams:
        cap = EC_CAPACITY * x.shape[1] // N_EXPERTS
        h, aff = normmod(x, gain, mod, 3, 4, ctx_row=ctx_row, router=router)
        pos_e, gate_e, pos_t = ec_select(aff, cap)
        routed.append((ec_gather(h, pos_e, gate_e, cap), pos_t, cap))
    ye = ec_experts([r[0] for r in routed], w1, w3, w2, layer)
    return [ec_combine(x, pos_t, y, mod, 5, cap, ctx_row=ctx_row)
            for (x, ctx_row), (_, pos_t, cap), y in zip(streams, routed, ye)]


def kernel(x, c, ctx, c_ctx, ada_w, ada_b, norm_mix, norm_ffn, w_in, pool_w, pool_scale, conv_w, a_log_f, a_log_b, dt_bias_f, dt_bias_b, gdn_norm, w_out_ab, w_qkv, lam_q1, lam_k1, lam_q2, lam_k2, subln, w_out_c, router, w1, w3, w2, final_norm):
    depth = ada_w.shape[0]
    B, L, D = x.shape
    Lc = ctx.shape[1]
    assert B < MOD_ROWS
    ctx_row = B
    cos, sin = axial_rope_tables(L // GRID_W)
    cond = jnp.zeros((MOD_ROWS, D), jnp.float32).at[:B].set(jax.nn.silu(c)).at[ctx_row].set(jax.nn.silu(c_ctx))
    cond = cond.astype(jnp.bfloat16)
    x_lat, x_ctx = x, ctx
    for layer in range(depth):
        last = layer == depth - 1
        j = layer // 2
        mod = proj(cond, ada_w, layer) + ada_b[layer]
        h_lat = normmod(x_lat, norm_mix[layer], mod, 0, 1)
        h_ctx = normmod(x_ctx, norm_mix[layer], mod, 0, 1, ctx_row=ctx_row)
        if layer % 2 == 0:
            y_ctx, y_lat = pool_gdn_mixer(h_ctx, h_lat, w_in, j, pool_w[j], pool_scale[j], conv_w[j],
                                          a_log_f[j], a_log_b[j], dt_bias_f[j], dt_bias_b[j], gdn_norm[j])
            w_out = w_out_ab
        else:
            assert last, "context outputs of an attention layer are only skipped when no later layer reads them"
            lambda_init = 0.8 - 0.6 * math.exp(-0.3 * layer)
            y_ctx = None
            y_lat = [diff_attn_mixer(h_ctx, h_lat, w_qkv, j, lam_q1[j], lam_k1[j], lam_q2[j], lam_k2[j],
                                     subln[j], lambda_init, cos, sin)]
            w_out = w_out_c
        x_lat = proj([p.reshape(B * L, -1) for p in y_lat], w_out, j, tm=1024,
                     residual=(x_lat.reshape(B * L, D), mod, 2, L, None)).reshape(B, L, D)
        streams = [(x_lat, None)]
        if not last:
            x_ctx = proj([p.reshape(B * Lc, -1) for p in y_ctx], w_out, j, tm=1024,
                         residual=(x_ctx.reshape(B * Lc, D), mod, 2, Lc, ctx_row)).reshape(B, Lc, D)
            streams.append((x_ctx, ctx_row))
        outs = moe_blocks(streams, norm_ffn[layer], mod, router[layer], w1, w3, w2, layer)
        x_lat = outs[0]
        if not last:
            x_ctx = outs[1]
    return final_rmsnorm(x_lat, final_norm)
```
